```python
import math
import jax, jax.numpy as jnp
from jax import lax
import numpy as np

D_MODEL = 1024
BATCH = 2
SEQ = 8192
DEPTH = 4
DEC_BATCH = 128
DEC_SEQ = 4
PAST_LEN = 2048
PAGE_SIZE = 128

N_BRANCH = 4
BR_W = 512
DA_H = 4
DA_DH = 64
DA_DV = 2 * DA_DH
DA_ROT = DA_DH // 4
ROPE_THETA = 500000.0
Q_BLOCK = 128
RT_H = 4
RT_DK = 64
RT_DV = 128
RET_THETA = 10000.0
GL_H = 4
GL_DK = 64
GL_DV = 128
GL_RANK = 16
GLA_TEMP = 16.0
SSD_H = 8
SSD_P = 64
SSD_DI = SSD_H * SSD_P
SSD_N = 128
SSD_G = 2
SSD_CONV = 4
CONV_CH = SSD_DI + 2 * SSD_G * SSD_N
CHUNK = 64
N_MEM = 256
MA_H = 4
MA_DH = 128
D_FF = 4 * D_MODEL
DN_ALPHA = (2 * DEPTH) ** 0.25
DN_BETA = (8 * DEPTH) ** -0.25
IN_SPLITS = (DA_H * 2 * DA_DH, DA_H * 2 * DA_DH, DA_H * DA_DV,
             RT_H * RT_DK, RT_H * RT_DK, RT_H * RT_DV, RT_H * RT_DV,
             GL_H * GL_DK, GL_H * GL_DK, GL_H * GL_DV, GL_H * GL_DV, GL_RANK,
             SSD_DI, CONV_CH, SSD_H,
             N_BRANCH * D_MODEL)
IN_COLS = sum(IN_SPLITS)

kernel_name = 'hybrid_diffattn_retnet_gla_ssd_decoder_step'


def layer_norm(x, g, b, eps=1e-5):
    xf = x.astype(jnp.float32)
    mu = jnp.mean(xf, axis=-1, keepdims=True)
    var = jnp.mean(jnp.square(xf - mu), axis=-1, keepdims=True)
    return ((xf - mu) * lax.rsqrt(var + eps) * g + b).astype(x.dtype)


def head_rmsnorm(o, eps=1e-5):
    of = o.astype(jnp.float32)
    return (of * lax.rsqrt(jnp.mean(of * of, axis=-1, keepdims=True) + eps)).astype(o.dtype)


def gated_rmsnorm(y, z, w, eps=1e-5):
    g = (y * jax.nn.silu(z)).astype(jnp.float32)
    B, L, C = g.shape
    g = g.reshape(B, L, SSD_G, C // SSD_G)
    g = g * lax.rsqrt(jnp.mean(g * g, axis=-1, keepdims=True) + eps)
    return (g.reshape(B, L, C) * w).astype(y.dtype)


def rope(x, pos, rot_dim, theta):
    half = rot_dim // 2
    inv = theta ** (-jnp.arange(half, dtype=jnp.float32) / half)
    ang = pos.astype(jnp.float32)[:, None] * inv[None, :]
    shape = (1, ang.shape[0]) + (1,) * (x.ndim - 3) + (half,)
    cos = jnp.cos(ang).reshape(shape).astype(x.dtype)
    sin = jnp.sin(ang).reshape(shape).astype(x.dtype)
    x1, x2, rest = x[..., :half], x[..., half:rot_dim], x[..., rot_dim:]
    return jnp.concatenate([x1 * cos - x2 * sin, x1 * sin + x2 * cos, rest], axis=-1)


def split_cols(proj):
    idx = np.cumsum(np.array(IN_SPLITS))[:-1].tolist()
    return jnp.split(proj, idx, axis=-1)


def chunk_scan(q, k, v, log_a, s0):
    B, L, H, _ = q.shape
    dv = v.shape[-1]
    c = math.gcd(L, CHUNK)
    n = L // c

    def to_chunks(t):
        return jnp.moveaxis(t.astype(jnp.float32).reshape((B, n, c) + t.shape[2:]), 1, 0)

    mask = jnp.tril(jnp.ones((c, c), dtype=bool))
    scalar_decay = log_a.shape[-1] == 1

    def step(S, inp):
        qc, kc, vc, ac = inp
        b = jnp.cumsum(ac, axis=1)
        diff = b[:, :, None] - b[:, None, :]
        w = jnp.exp(jnp.where(mask[None, :, :, None, None], diff, -jnp.inf))
        if scalar_decay:
            scores = jnp.einsum('bthd,bshd->btsh', qc, kc) * w[..., 0]
        else:
            scores = jnp.einsum('bthd,bshd,btshd->btsh', qc, kc, w)
        o = (jnp.einsum('btsh,bshe->bthe', scores, vc)
             + jnp.einsum('bthd,bhde->bthe', qc * jnp.exp(b), S))
        b_last = b[:, -1:]
        S = (jnp.exp(b_last[:, 0])[..., None] * S
             + jnp.einsum('bshd,bshe->bhde', kc * jnp.exp(b_last - b), vc))
        return S, o

    S, o = lax.scan(step, s0.astype(jnp.float32),
                    (to_chunks(q), to_chunks(k), to_chunks(v), to_chunks(log_a)))
    o = jnp.moveaxis(o, 0, 1).reshape(B, L, H, dv)
    return o.astype(v.dtype), S.astype(s0.dtype)


def diff_attend(q, k, v, q_pos, k_pos, lam):
    s = jnp.einsum('bqhcd,bkhcd->bhcqk', q, k).astype(jnp.float32) * (DA_DH ** -0.5)
    causal = k_pos[None, :] <= q_pos[:, None]
    s = jnp.where(causal[None, None, None], s, -jnp.inf)
    a = jax.nn.softmax(s, axis=-1)
    a = a[:, :, 0] - lam * a[:, :, 1]
    return jnp.einsum('bhqk,bkhe->bqhe', a.astype(v.dtype), v)


def cross_attend(h, mk, mv, w_cq, w_co):
    B, L, _ = h.shape
    q = (h @ w_cq).reshape(B, L, MA_H, MA_DH)
    s = jnp.einsum('bqhd,bkhd->bhqk', q, mk).astype(jnp.float32) * (MA_DH ** -0.5)
    a = jax.nn.softmax(s, axis=-1)
    o = jnp.einsum('bhqk,bkhd->bqhd', a.astype(mv.dtype), mv).reshape(B, L, MA_H * MA_DH)
    return o @ w_co


def squared_relu_mlp(h, w1, w2):
    return jnp.square(jax.nn.relu(h @ w1)) @ w2


def mixer_sublayer(l, h, pos, P, past, s_ret, s_gla, s_ssm, conv_buf):
    B, L, _ = h.shape
    (da_q, da_k, da_v, rt_q, rt_k, rt_v, rt_g, gl_q, gl_k, gl_v, gl_r, gl_lr,
     sd_z, sd_xbc, sd_dt, gate) = split_cols(h @ P['w_in'][l])

    q = rope(da_q.reshape(B, L, DA_H, 2, DA_DH), pos, DA_ROT, ROPE_THETA)
    k = rope(da_k.reshape(B, L, DA_H, 2, DA_DH), pos, DA_ROT, ROPE_THETA)
    v = da_v.reshape(B, L, DA_H, DA_DV)
    lam_init = 0.8 - 0.6 * math.exp(-0.3 * l)
    lam = (jnp.exp(jnp.sum((P['da_lq1'][l] * P['da_lk1'][l]).astype(jnp.float32)))
           - jnp.exp(jnp.sum((P['da_lq2'][l] * P['da_lk2'][l]).astype(jnp.float32))) + lam_init)
    if past is None:
        blk = math.gcd(L, Q_BLOCK)
        nb = L // blk
        qb = jnp.moveaxis(q.reshape(B, nb, blk, DA_H, 2, DA_DH), 1, 0)
        o = lax.map(lambda a: diff_attend(a[0], k, v, a[1], pos, lam), (qb, pos.reshape(nb, blk)))
        o = jnp.moveaxis(o, 0, 1).reshape(B, L, DA_H, DA_DV)
    else:
        pk, pv = past
        kk = jnp.concatenate([pk, k], axis=1)
        vv = jnp.concatenate([pv, v], axis=1)
        k_pos = jnp.concatenate([jnp.arange(pk.shape[1], dtype=jnp.int32), pos])
        o = diff_attend(q, kk, vv, pos, k_pos, lam)
    o_da = (head_rmsnorm(o) * (1.0 - lam_init)).reshape(B, L, BR_W)
    k_rows = k.reshape(B, L, DA_H, 2 * DA_DH)

    rq = rope(rt_q.reshape(B, L, RT_H, RT_DK), pos, RT_DK, RET_THETA)
    rk = rope(rt_k.reshape(B, L, RT_H, RT_DK), pos, RT_DK, RET_THETA) * (RT_DK ** -0.5)
    log_gamma = jnp.log1p(-jnp.exp2(-5.0 - jnp.arange(RT_H, dtype=jnp.float32)))
    o, s_ret_new = chunk_scan(rq, rk, rt_v.reshape(B, L, RT_H, RT_DV),
                              jnp.broadcast_to(log_gamma[:, None], (B, L, RT_H, 1)), s_ret)
    o_rt = head_rmsnorm(o).reshape(B, L, BR_W) * jax.nn.silu(rt_g)

    la = jax.nn.log_sigmoid((gl_lr @ P['gla_w_a2'][l] + P['gla_b_a'][l]).astype(jnp.float32)) / GLA_TEMP
    o, s_gla_new = chunk_scan(gl_q.reshape(B, L, GL_H, GL_DK),
                              gl_k.reshape(B, L, GL_H, GL_DK) * (GL_DK ** -0.5),
                              gl_v.reshape(B, L, GL_H, GL_DV),
                              la.reshape(B, L, GL_H, GL_DK), s_gla)
    o_gl = head_rmsnorm(o).reshape(B, L, BR_W) * jax.nn.silu(gl_r)

    xpad = jnp.concatenate([conv_buf, sd_xbc], axis=1)
    cw = P['ssd_conv_w'][l]
    xbc = jax.nn.silu(sum(xpad[:, j:j + L] * cw[j] for j in range(SSD_CONV)) + P['ssd_conv_b'][l])
    conv_new = xpad[:, L:]
    xs, Bm, Cm = jnp.split(xbc, [SSD_DI, SSD_DI + SSD_G * SSD_N], axis=-1)
    dt = jax.nn.softplus(sd_dt.astype(jnp.float32) + P['ssd_dt_bias'][l])
    A = -jnp.exp(P['ssd_a_log'][l].astype(jnp.float32))
    xh = xs.reshape(B, L, SSD_H, SSD_P)
    rep = SSD_H // SSD_G
    Bh = jnp.repeat(Bm.reshape(B, L, SSD_G, SSD_N), rep, axis=2)
    Ch = jnp.repeat(Cm.reshape(B, L, SSD_G, SSD_N), rep, axis=2)
    y, s_ssm_new = chunk_scan(Ch, Bh, xh * dt[..., None], (dt * A)[..., None], s_ssm)
    y = y + P['ssd_d'][l][:, None] * xh
    o_sd = gated_rmsnorm(y.reshape(B, L, SSD_DI), sd_z, P['ssd_norm_w'][l])

    br = jnp.stack([o_da, o_rt, o_gl, o_sd.astype(o_da.dtype)], axis=2)
    up = jnp.einsum('blnc,ncd->blnd', br, P['w_up'][l])
    g = jax.nn.sigmoid(gate.reshape(B, L, N_BRANCH, D_MODEL))
    out = jnp.einsum('bld,de->ble', jnp.sum(g * up, axis=2), P['w_o'][l])
    return out, (k_rows, v, s_ret_new, s_gla_new, s_ssm_new, conv_new)


def layer(l, x, pos, P, mk, mv, past, s_ret, s_gla, s_ssm, conv_buf):
    y, st = mixer_sublayer(l, x, pos, P, past, s_ret, s_gla, s_ssm, conv_buf)
    x = layer_norm(DN_ALPHA * x + y, P['ln1_g'][l], P['ln1_b'][l])
    x = layer_norm(DN_ALPHA * x + cross_attend(x, mk, mv, P['w_cq'][l], P['w_co'][l]),
                   P['ln2_g'][l], P['ln2_b'][l])
    x = layer_norm(DN_ALPHA * x + squared_relu_mlp(x, P['w_mlp1'][l], P['w_mlp2'][l]),
                   P['ln3_g'][l], P['ln3_b'][l])
    return x, st


def setup_inputs(seed: int = 0) -> dict:
    key = jax.random.key(seed)
    ks = iter(jax.random.split(key, 64))

    def nrm(shape, scale):
        return jax.random.normal(next(ks), shape, jnp.float32) * scale

    n_pages = PAST_LEN // PAGE_SIZE
    n_pool = (5 * DEC_BATCH * n_pages) // 4
    page_table = jax.random.permutation(next(ks), n_pool)[:DEC_BATCH * n_pages]
    page_table = page_table.reshape(DEC_BATCH, n_pages).astype(jnp.int32)

    dt0 = jnp.exp(jax.random.uniform(next(ks), (DEPTH, SSD_H), jnp.float32,
                                     math.log(0.001), math.log(0.1)))
    ssd_dt_bias = dt0 + jnp.log(-jnp.expm1(-dt0))
    ssd_a_log = jnp.log(jax.random.uniform(next(ks), (DEPTH, SSD_H), jnp.float32, 1.0, 16.0))

    return {
        'x_prompt': nrm((BATCH, SEQ, D_MODEL), 1.0),
        'x_sample': nrm((DEC_BATCH, DEC_SEQ, D_MODEL), 1.0),
        'mem_prompt': nrm((BATCH, N_MEM, D_MODEL), 1.0),
        'cache_diff_k': nrm((DEPTH, n_pool, PAGE_SIZE, DA_H, 2 * DA_DH), 1.0),
        'cache_diff_v': nrm((DEPTH, n_pool, PAGE_SIZE, DA_H, DA_DV), 1.0),
        'page_table': page_table,
        'state_ret': nrm((DEPTH, DEC_BATCH, RT_H, RT_DK, RT_DV), 0.5),
        'state_gla': nrm((DEPTH, DEC_BATCH, GL_H, GL_DK, GL_DV), 0.5),
        'state_ssm': nrm((DEPTH, DEC_BATCH, SSD_H, SSD_N, SSD_P), 0.5),
        'state_conv': nrm((DEPTH, DEC_BATCH, SSD_CONV - 1, CONV_CH), 1.0),
        'cache_mem_k': nrm((DEPTH, DEC_BATCH, N_MEM, MA_H, MA_DH), 1.0),
        'cache_mem_v': nrm((DEPTH, DEC_BATCH, N_MEM, MA_H, MA_DH), 1.0),
        'w_in': nrm((DEPTH, D_MODEL, IN_COLS), D_MODEL ** -0.5),
        'da_lq1': nrm((DEPTH, DA_DH), 0.1),
        'da_lk1': nrm((DEPTH, DA_DH), 0.1),
        'da_lq2': nrm((DEPTH, DA_DH), 0.1),
        'da_lk2': nrm((DEPTH, DA_DH), 0.1),
        'gla_w_a2': nrm((DEPTH, GL_RANK, GL_H * GL_DK), GL_RANK ** -0.5),
        'gla_b_a': nrm((DEPTH, GL_H * GL_DK), 0.1),
        'ssd_conv_w': nrm((DEPTH, SSD_CONV, CONV_CH), SSD_CONV ** -0.5),
        'ssd_conv_b': nrm((DEPTH, CONV_CH), 0.02),
        'ssd_dt_bias': ssd_dt_bias,
        'ssd_a_log': ssd_a_log,
        'ssd_d': 1.0 + nrm((DEPTH, SSD_H), 0.02),
        'ssd_norm_w': 1.0 + nrm((DEPTH, SSD_DI), 0.02),
        'w_up': nrm((DEPTH, N_BRANCH, BR_W, D_MODEL), BR_W ** -0.5),
        'w_o': nrm((DEPTH, D_MODEL, D_MODEL), DN_BETA * D_MODEL ** -0.5),
        'ln1_g': 1.0 + nrm((DEPTH, D_MODEL), 0.02),
        'ln1_b': nrm((DEPTH, D_MODEL), 0.02),
        'w_cq': nrm((DEPTH, D_MODEL, MA_H * MA_DH), D_MODEL ** -0.5),
        'w_ck': nrm((DEPTH, D_MODEL, MA_H * MA_DH), D_MODEL ** -0.5),
        'w_cv': nrm((DEPTH, D_MODEL, MA_H * MA_DH), D_MODEL ** -0.5),
        'w_co': nrm((DEPTH, MA_H * MA_DH, D_MODEL), DN_BETA * (MA_H * MA_DH) ** -0.5),
        'ln2_g': 1.0 + nrm((DEPTH, D_MODEL), 0.02),
        'ln2_b': nrm((DEPTH, D_MODEL), 0.02),
        'w_mlp1': nrm((DEPTH, D_MODEL, D_FF), D_MODEL ** -0.5),
        'w_mlp2': nrm((DEPTH, D_FF, D_MODEL), DN_BETA * D_FF ** -0.5),
        'ln3_g': 1.0 + nrm((DEPTH, D_MODEL), 0.02),
        'ln3_b': nrm((DEPTH, D_MODEL), 0.02),
    }


def reference(x_prompt, x_sample, mem_prompt, cache_diff_k, cache_diff_v, page_table,
              state_ret, state_gla, state_ssm, state_conv, cache_mem_k, cache_mem_v,
              w_in, da_lq1, da_lk1, da_lq2, da_lk2, gla_w_a2, gla_b_a,
              ssd_conv_w, ssd_conv_b, ssd_dt_bias, ssd_a_log, ssd_d, ssd_norm_w,
              w_up, w_o, ln1_g, ln1_b, w_cq, w_ck, w_cv, w_co, ln2_g, ln2_b,
              w_mlp1, w_mlp2, ln3_g, ln3_b):
    P = dict(w_in=w_in, da_lq1=da_lq1, da_lk1=da_lk1, da_lq2=da_lq2, da_lk2=da_lk2,
             gla_w_a2=gla_w_a2, gla_b_a=gla_b_a, ssd_conv_w=ssd_conv_w, ssd_conv_b=ssd_conv_b,
             ssd_dt_bias=ssd_dt_bias, ssd_a_log=ssd_a_log, ssd_d=ssd_d, ssd_norm_w=ssd_norm_w,
             w_up=w_up, w_o=w_o, ln1_g=ln1_g, ln1_b=ln1_b, w_cq=w_cq, w_co=w_co,
             ln2_g=ln2_g, ln2_b=ln2_b, w_mlp1=w_mlp1, w_mlp2=w_mlp2, ln3_g=ln3_g, ln3_b=ln3_b)

    B, L, _ = x_prompt.shape
    pos_p = jnp.arange(L, dtype=jnp.int32)
    x = x_prompt
    pk_l, pv_l, pr_l, pg_l, ps_l, pc_l, pmk_l, pmv_l = [], [], [], [], [], [], [], []
    for l in range(DEPTH):
        mk = (mem_prompt @ w_ck[l]).reshape(B, -1, MA_H, MA_DH)
        mv = (mem_prompt @ w_cv[l]).reshape(B, -1, MA_H, MA_DH)
        x, (kr, vr, sr, sg, ss, cb) = layer(
            l, x, pos_p, P, mk, mv, None,
            jnp.zeros((B, RT_H, RT_DK, RT_DV), x.dtype),
            jnp.zeros((B, GL_H, GL_DK, GL_DV), x.dtype),
            jnp.zeros((B, SSD_H, SSD_N, SSD_P), x.dtype),
            jnp.zeros((B, SSD_CONV - 1, CONV_CH), x.dtype))
        pk_l.append(kr); pv_l.append(vr); pr_l.append(sr); pg_l.append(sg)
        ps_l.append(ss); pc_l.append(cb); pmk_l.append(mk); pmv_l.append(mv)
    y_prompt = x

    Bd, Ld, _ = x_sample.shape
    past_len = page_table.shape[1] * cache_diff_k.shape[2]
    pos_s = past_len + jnp.arange(Ld, dtype=jnp.int32)
    x = x_sample
    sk_l, sv_l, sr_l, sg_l, ss_l, sc_l = [], [], [], [], [], []
    for l in range(DEPTH):
        pk = cache_diff_k[l, page_table].reshape(Bd, past_len, DA_H, 2, DA_DH)
        pv = cache_diff_v[l, page_table].reshape(Bd, past_len, DA_H, DA_DV)
        x, (kr, vr, sr, sg, ss, cb) = layer(
            l, x, pos_s, P, cache_mem_k[l], cache_mem_v[l], (pk, pv),
            state_ret[l], state_gla[l], state_ssm[l], state_conv[l])
        sk_l.append(kr); sv_l.append(vr); sr_l.append(sr); sg_l.append(sg)
        ss_l.append(ss); sc_l.append(cb)
    y_sample = x

    new_diff_k_prompt = jnp.stack(pk_l)
    new_diff_v_prompt = jnp.stack(pv_l)
    new_state_ret_prompt = jnp.stack(pr_l)
    new_state_gla_prompt = jnp.stack(pg_l)
    new_state_ssm_prompt = jnp.stack(ps_l)
    new_state_conv_prompt = jnp.stack(pc_l)
    new_cache_mem_k_prompt = jnp.stack(pmk_l)
    new_cache_mem_v_prompt = jnp.stack(pmv_l)
    new_diff_k_sample = jnp.stack(sk_l)
    new_diff_v_sample = jnp.stack(sv_l)
    new_state_ret_sample = jnp.stack(sr_l)
    new_state_gla_sample = jnp.stack(sg_l)
    new_state_ssm_sample = jnp.stack(ss_l)
    new_state_conv_sample = jnp.stack(sc_l)
    return (y_prompt, y_sample,
            new_diff_k_prompt, new_diff_v_prompt, new_state_ret_prompt, new_state_gla_prompt,
            new_state_ssm_prompt, new_state_conv_prompt, new_cache_mem_k_prompt, new_cache_mem_v_prompt,
            new_diff_k_sample, new_diff_v_sample, new_state_ret_sample, new_state_gla_sample,
            new_state_ssm_sample, new_state_conv_sample)
```

```python
import functools
import math

import jax
import jax.numpy as jnp
from jax import lax
from jax.experimental import pallas as pl
from jax.experimental.pallas import tpu as pltpu

F32 = jnp.float32
BF16 = jnp.bfloat16

DA_H, DA_DH, DA_DV = 4, 64, 128
DA_ROT = DA_DH // 4
ROPE_THETA = 500000.0
RT_H, RT_DK, RT_DV = 4, 64, 128
RET_THETA = 10000.0
GL_H, GL_DK, GL_DV, GL_RANK = 4, 64, 128, 16
GLA_TEMP = 16.0
SSD_H, SSD_P, SSD_N, SSD_G, SSD_CONV = 8, 64, 128, 2, 4
SSD_DI = SSD_H * SSD_P
CONV_CH = SSD_DI + 2 * SSD_G * SSD_N
MA_H, MA_DH = 4, 128
N_BRANCH, BR_W = 4, 512
EPS = 1e-5
NEG = -1e30

VMEM_LIMIT_BYTES = 56 * 1024 * 1024
LANES = 128
SUBLANES = 8

_NT = (((1,), (1,)), ((), ()))
_TN = (((0,), (0,)), ((), ()))


def _cparams(*sem):
    return pltpu.CompilerParams(dimension_semantics=sem, vmem_limit_bytes=VMEM_LIMIT_BYTES)


def _pick(n, cands):
    for c in cands:
        if n % c == 0:
            return c
    return n


def _split3(x):
    h = x.astype(BF16)
    r = x - h.astype(F32)
    m = r.astype(BF16)
    lo = (r - m.astype(F32)).astype(BF16)
    return h, m, lo


def _dot(a, b, dims=None):
    if dims is None:
        return jnp.dot(a, b, preferred_element_type=F32)
    return lax.dot_general(a, b, dims, preferred_element_type=F32)


def _mm_kernel(x_ref, w_ref, o_ref, *, act):
    acc = _dot(x_ref[...], w_ref[...])
    if act == "relu2":
        acc = jnp.square(jnp.maximum(acc, 0.0))
    o_ref[...] = acc.astype(o_ref.dtype)


def _mm(x, w, act=None, out_dtype=F32):
    M, K = x.shape
    N = w.shape[1]
    tm = _pick(M, (1024, 512, 256, 128, 64, 32, 16, 8))
    tn = _pick(N, (1024, 768, 512, 384, 256, 128))
    return pl.pallas_call(
        functools.partial(_mm_kernel, act=act),
        out_shape=jax.ShapeDtypeStruct((M, N), out_dtype),
        grid=(N // tn, M // tm),
        in_specs=[pl.BlockSpec((tm, K), lambda j, i: (i, 0)),
                  pl.BlockSpec((K, tn), lambda j, i: (0, j))],
        out_specs=pl.BlockSpec((tm, tn), lambda j, i: (i, j)),
        compiler_params=_cparams("parallel", "parallel"),
        name="mm",
    )(x, w)


def _ln_rows(z, g, b):
    mu = jnp.mean(z, axis=-1, keepdims=True)
    zc = z - mu
    var = jnp.mean(zc * zc, axis=-1, keepdims=True)
    return zc * lax.rsqrt(var + EPS) * g + b


def _mm_res_ln_kernel(x_ref, w_ref, r_ref, g_ref, b_ref, o_ref, ob_ref, *, alpha):
    y = _dot(x_ref[...], w_ref[...])
    out = _ln_rows(alpha * r_ref[...] + y, g_ref[...], b_ref[...])
    o_ref[...] = out
    ob_ref[...] = out.astype(BF16)


def _mm_res_ln(x, w, res, g, b, alpha):
    M, K = x.shape
    N = w.shape[1]
    tm = _pick(M, (512, 256, 128, 64, 32, 16, 8))
    return pl.pallas_call(
        functools.partial(_mm_res_ln_kernel, alpha=alpha),
        out_shape=(jax.ShapeDtypeStruct((M, N), F32), jax.ShapeDtypeStruct((M, N), BF16)),
        grid=(M // tm,),
        in_specs=[pl.BlockSpec((tm, K), lambda i: (i, 0)),
                  pl.BlockSpec((K, N), lambda i: (0, 0)),
                  pl.BlockSpec((tm, N), lambda i: (i, 0)),
                  pl.BlockSpec((1, N), lambda i: (0, 0)),
                  pl.BlockSpec((1, N), lambda i: (0, 0))],
        out_specs=(pl.BlockSpec((tm, N), lambda i: (i, 0)), pl.BlockSpec((tm, N), lambda i: (i, 0))),
        compiler_params=_cparams("parallel"),
        name="mm_res_ln",
    )(x, w, res, g.reshape(1, N), b.reshape(1, N))


def _merge_ln_kernel(br_ref, gate_ref, wup_ref, wo_ref, r_ref, g_ref, b_ref, o_ref, ob_ref, *, alpha):
    d = wo_ref.shape[0]
    acc = None
    for n in range(N_BRANCH):
        up = _dot(br_ref[:, n * BR_W:(n + 1) * BR_W], wup_ref[n])
        term = jax.nn.sigmoid(gate_ref[:, n * d:(n + 1) * d]) * up
        acc = term if acc is None else acc + term
    y = _dot(acc.astype(BF16), wo_ref[...])
    out = _ln_rows(alpha * r_ref[...] + y, g_ref[...], b_ref[...])
    o_ref[...] = out
    ob_ref[...] = out.astype(BF16)


def _merge_ln(br, gate, w_up, w_o, res, g, b, alpha):
    M = br.shape[0]
    D = w_o.shape[0]
    tm = _pick(M, (256, 128, 64, 32, 16, 8))
    return pl.pallas_call(
        functools.partial(_merge_ln_kernel, alpha=alpha),
        out_shape=(jax.ShapeDtypeStruct((M, D), F32), jax.ShapeDtypeStruct((M, D), BF16)),
        grid=(M // tm,),
        in_specs=[pl.BlockSpec((tm, N_BRANCH * BR_W), lambda i: (i, 0)),
                  pl.BlockSpec((tm, N_BRANCH * D), lambda i: (i, 0)),
                  pl.BlockSpec((N_BRANCH, BR_W, D), lambda i: (0, 0, 0)),
                  pl.BlockSpec((D, D), lambda i: (0, 0)),
                  pl.BlockSpec((tm, D), lambda i: (i, 0)),
                  pl.BlockSpec((1, D), lambda i: (0, 0)),
                  pl.BlockSpec((1, D), lambda i: (0, 0))],
        out_specs=(pl.BlockSpec((tm, D), lambda i: (i, 0)), pl.BlockSpec((tm, D), lambda i: (i, 0))),
        compiler_params=_cparams("parallel"),
        name="merge_ln",
    )(br, gate, w_up, w_o, res, g.reshape(1, D), b.reshape(1, D))


def _xattn_kernel(li_ref, q_ref, k_ref, v_ref, o_ref):
    del li_ref
    scale = MA_DH ** -0.5
    for h in range(MA_H):
        sl = slice(h * MA_DH, (h + 1) * MA_DH)
        q = q_ref[0, :, sl].astype(BF16)
        k = k_ref[0, 0, :, sl].astype(BF16)
        v = v_ref[0, 0, :, sl].astype(BF16)
        s = _dot(q, k, _NT) * scale
        m = jnp.max(s, axis=-1, keepdims=True)
        p = jnp.exp(s - m)
        a = p / jnp.sum(p, axis=-1, keepdims=True)
        o_ref[0, :, sl] = _dot(a.astype(BF16), v).astype(o_ref.dtype)


def _xattn(q, mk, mv, li):
    B, L, W = q.shape
    nm = mk.shape[2]
    tq = _pick(L, (512, 256, 128, 64, 32, 16, 8))
    return pl.pallas_call(
        _xattn_kernel,
        out_shape=jax.ShapeDtypeStruct((B, L, W), BF16),
        grid_spec=pltpu.PrefetchScalarGridSpec(
            num_scalar_prefetch=1,
            grid=(B, L // tq),
            in_specs=[pl.BlockSpec((1, tq, W), lambda b, i, li: (b, i, 0)),
                      pl.BlockSpec((1, 1, nm, W), lambda b, i, li: (li[0], b, 0, 0)),
                      pl.BlockSpec((1, 1, nm, W), lambda b, i, li: (li[0], b, 0, 0))],
            out_specs=pl.BlockSpec((1, tq, W), lambda b, i, li: (b, i, 0))),
        compiler_params=_cparams("parallel", "parallel"),
        name="xattn",
    )(li, q, mk, mv)


def _head_rms(o):
    return o * lax.rsqrt(jnp.mean(o * o, axis=-1, keepdims=True) + EPS)


def _flash_da_kernel(sc_ref, q_ref, k_ref, v_ref, o_ref, *, tq, tk):
    i = pl.program_id(2)
    lam = sc_ref[0]
    out_scale = sc_ref[1]
    q = q_ref[0]
    lo = lax.broadcasted_iota(jnp.int32, (1, LANES), 1) < DA_DH
    zero = jnp.zeros_like(q)
    q12 = jnp.concatenate([jnp.where(lo, q, zero), jnp.where(lo, zero, q)], axis=0)
    row = lax.broadcasted_iota(jnp.int32, (2 * tq, 1), 0)
    qpos = i * tq + jnp.where(row >= tq, row - tq, row)

    def step(j, carry, masked):
        m, l, acc = carry
        start = pl.multiple_of(j * tk, tk)
        kb = k_ref[0, pl.ds(start, tk), :]
        vb = v_ref[0, pl.ds(start, tk), :]
        s = _dot(q12, kb, _NT)
        if masked:
            kpos = j * tk + lax.broadcasted_iota(jnp.int32, (1, tk), 1)
            s = jnp.where(kpos <= qpos, s, NEG)
        m_new = jnp.maximum(m, jnp.max(s, axis=-1, keepdims=True))
        alpha = jnp.exp(m - m_new)
        p = jnp.exp(s - m_new)
        l = alpha * l + jnp.sum(p, axis=-1, keepdims=True)
        acc = alpha * acc + _dot(p.astype(BF16), vb)
        return m_new, l, acc

    init = (jnp.full((2 * tq, 1), NEG, F32), jnp.zeros((2 * tq, 1), F32), jnp.zeros((2 * tq, LANES), F32))
    per = tq // tk
    nfull = i * per
    carry = lax.fori_loop(0, nfull, lambda j, c: step(j, c, False), init)
    for jj in range(per):
        carry = step(nfull + jj, carry, True)
    _, l, acc = carry
    o12 = acc / l
    o = o12[:tq] - lam * o12[tq:]
    o_ref[0] = (_head_rms(o) * out_scale).astype(o_ref.dtype)


def _flash_da(q, k, v, sc):
    B, L, W = q.shape
    tq = _pick(L, (256, 128, 64, 32, 16, 8))
    tk = tq
    return pl.pallas_call(
        functools.partial(_flash_da_kernel, tq=tq, tk=tk),
        out_shape=jax.ShapeDtypeStruct((B, L, W), BF16),
        grid=(B, DA_H, L // tq),
        in_specs=[pl.BlockSpec(memory_space=pltpu.SMEM),
                  pl.BlockSpec((1, tq, LANES), lambda b, h, i: (b, i, h)),
                  pl.BlockSpec((1, L, LANES), lambda b, h, i: (b, 0, h)),
                  pl.BlockSpec((1, L, LANES), lambda b, h, i: (b, 0, h))],
        out_specs=pl.BlockSpec((1, tq, LANES), lambda b, h, i: (b, i, h)),
        compiler_params=_cparams("parallel", "parallel", "arbitrary"),
        name="flash_da",
    )(sc, q, k, v)


def _decode_da_kernel(pt_ref, li_ref, sc_ref, qb_ref, kn_ref, vn_ref, *refs, n_pages, t_pad):
    del pt_ref, li_ref
    k_refs = refs[:n_pages]
    v_refs = refs[n_pages:2 * n_pages]
    o_ref = refs[2 * n_pages]
    lam = sc_ref[0]
    out_scale = sc_ref[1]
    qb = qb_ref[0]
    w = DA_H * DA_DV

    s_pages = [_dot(k_refs[p][0, 0].astype(BF16), qb) for p in range(n_pages)]
    kn = kn_ref[0]
    vn = vn_ref[0]
    s_new = _dot(kn.astype(BF16), qb)
    key = lax.broadcasted_iota(jnp.int32, (t_pad, LANES), 0)
    col = lax.broadcasted_iota(jnp.int32, (t_pad, LANES), 1)
    s_new = jnp.where(key <= col % t_pad, s_new, NEG)

    m = jnp.max(s_new, axis=0, keepdims=True)
    for s in s_pages:
        m = jnp.maximum(m, jnp.max(s, axis=0, keepdims=True))

    ones = jnp.ones((k_refs[0].shape[2], LANES), BF16)
    p_new = jnp.exp(s_new - m)
    acc = _dot(p_new, vn, _TN)
    den = _dot(p_new, jnp.ones((t_pad, LANES), F32), _TN)
    for p in range(n_pages):
        pp = jnp.exp(s_pages[p] - m).astype(BF16)
        acc = acc + _dot(pp, v_refs[p][0, 0].astype(BF16), _TN)
        den = den + _dot(pp, ones, _TN)
    o12 = acc / jnp.concatenate([den] * (w // LANES), axis=1)
    half = DA_H * t_pad
    d = o12[0:half] - lam * o12[half:2 * half]
    for h in range(DA_H):
        oh = d[h * t_pad:(h + 1) * t_pad, h * DA_DV:(h + 1) * DA_DV]
        o_ref[0, :, h * DA_DV:(h + 1) * DA_DV] = _head_rms(oh) * out_scale


def _decode_da(qblk, k_new, v_new, cache_k, cache_v, page_table, li, sc):
    bd, t_pad, w = k_new.shape
    n_pages = page_table.shape[1]
    page = cache_k.shape[2]

    def page_spec(p):
        return pl.BlockSpec((1, 1, page, w), lambda b, pt, li: (li[0], pt[b, p], 0, 0))

    in_specs = [pl.BlockSpec(memory_space=pltpu.SMEM),
                pl.BlockSpec((1, w, LANES), lambda b, pt, li: (b, 0, 0)),
                pl.BlockSpec((1, t_pad, w), lambda b, pt, li: (b, 0, 0)),
                pl.BlockSpec((1, t_pad, w), lambda b, pt, li: (b, 0, 0))]
    in_specs += [page_spec(p) for p in range(n_pages)] * 2
    return pl.pallas_call(
        functools.partial(_decode_da_kernel, n_pages=n_pages, t_pad=t_pad),
        out_shape=jax.ShapeDtypeStruct((bd, t_pad, w), F32),
        grid_spec=pltpu.PrefetchScalarGridSpec(
            num_scalar_prefetch=2,
            grid=(bd,),
            in_specs=in_specs,
            out_specs=pl.BlockSpec((1, t_pad, w), lambda b, pt, li: (b, 0, 0))),
        compiler_params=_cparams("parallel"),
        name="decode_da",
    )(page_table, li, sc, qblk, k_new, v_new, *([cache_k] * n_pages), *([cache_v] * n_pages))


def _cumsum_rows(x, tri):
    h, m, lo = _split3(x)
    return _dot(tri, h) + _dot(tri, m) + _dot(tri, lo)


def _scan_qk_kernel(li_ref, q_ref, k_ref, v_ref, la_ref, s0_ref, o_ref, sn_ref, s_scr, *, c, nch, lv):
    del li_ref
    t = pl.program_id(1)

    @pl.when(t == 0)
    def _():
        s_scr[...] = s0_ref[0, 0]

    mxu = BF16 if c >= 16 else F32
    row = lax.broadcasted_iota(jnp.int32, (c, 1), 0)
    colc = lax.broadcasted_iota(jnp.int32, (1, c), 1)
    causal = row >= colc
    tri = jnp.where(causal, 1.0, 0.0).astype(BF16)
    lo = lax.broadcasted_iota(jnp.int32, (1, LANES), 1) < RT_DK
    mid = c // 2 - 1

    def chunk(ci, carry):
        r0 = pl.multiple_of(ci * c, c)
        q = q_ref[0, pl.ds(r0, c), :]
        k = k_ref[0, pl.ds(r0, c), :]
        v = v_ref[0, pl.ds(r0, c), :]
        la = la_ref[0, pl.ds(r0, c), :]
        if lv < c:
            valid = row < lv
            la = jnp.where(valid, la, 0.0)
            k = jnp.where(valid, k, 0.0)
        b = _cumsum_rows(la, tri)
        bm = b[mid:mid + 1, :]
        bl = b[c - 1:c, :]
        qs = q * jnp.exp(b - bm)
        ks = k * jnp.exp(bm - b)
        kh = k * jnp.exp(bl - b)
        qe = q * jnp.exp(b)
        for p in range(2):
            sl = slice(p * LANES, (p + 1) * LANES)
            S = s_scr[p]
            Sm = S.astype(mxu)
            ksp = ks[:, sl].astype(mxu)
            qsp = qs[:, sl]
            qep = qe[:, sl]
            for hh in range(2):
                h = 2 * p + hh
                msk = lo if hh == 0 else jnp.logical_not(lo)
                s = _dot(jnp.where(msk, qsp, 0.0).astype(mxu), ksp, _NT)
                s = jnp.where(causal, s, 0.0)
                vh = v[:, h * RT_DV:(h + 1) * RT_DV].astype(mxu)
                o = _dot(s.astype(mxu), vh) + _dot(jnp.where(msk, qep, 0.0).astype(mxu), Sm)
                o_ref[0, pl.ds(r0, c), h * RT_DV:(h + 1) * RT_DV] = _head_rms(o)
            upd = _dot(kh[:, sl].astype(mxu), v[:, 2 * p * RT_DV:(2 * p + 2) * RT_DV].astype(mxu), _TN)
            dcol = jnp.exp(b[:, sl].T[:, c - 1:c])
            s_scr[p] = dcol * S + jnp.concatenate(
                [upd[0:RT_DK, 0:RT_DV], upd[RT_DK:2 * RT_DK, RT_DV:2 * RT_DV]], axis=0)
        return carry

    lax.fori_loop(0, nch, chunk, 0)

    @pl.when(t == pl.num_programs(1) - 1)
    def _():
        sn_ref[0] = s_scr[...]


def _scan_qk(q, k, v, la, s0, li, lv):
    B, L, _ = q.shape
    c = _pick(L, (128, 64, 32, 16, 8))
    tl = _pick(L, (512, 256, 128, 64, 32, 16, 8))
    wqk = RT_H * RT_DK
    wv = RT_H * RT_DV
    if lv is None:
        lv = c
    return pl.pallas_call(
        functools.partial(_scan_qk_kernel, c=c, nch=tl // c, lv=lv),
        out_shape=(jax.ShapeDtypeStruct((B, L, wv), F32),
                   jax.ShapeDtypeStruct((B, 2, 2 * RT_DK, RT_DV), F32)),
        grid_spec=pltpu.PrefetchScalarGridSpec(
            num_scalar_prefetch=1,
            grid=(B, L // tl),
            in_specs=[pl.BlockSpec((1, tl, wqk), lambda b, t, li: (b, t, 0)),
                      pl.BlockSpec((1, tl, wqk), lambda b, t, li: (b, t, 0)),
                      pl.BlockSpec((1, tl, wv), lambda b, t, li: (b, t, 0)),
                      pl.BlockSpec((1, tl, wqk), lambda b, t, li: (b, t, 0)),
                      pl.BlockSpec((1, 1, 2, 2 * RT_DK, RT_DV), lambda b, t, li: (li[0], b, 0, 0, 0))],
            out_specs=(pl.BlockSpec((1, tl, wv), lambda b, t, li: (b, t, 0)),
                       pl.BlockSpec((1, 2, 2 * RT_DK, RT_DV), lambda b, t, li: (b, 0, 0, 0))),
            scratch_shapes=[pltpu.VMEM((2, 2 * RT_DK, RT_DV), F32)]),
        compiler_params=_cparams("parallel", "arbitrary"),
        name="scan_qk",
    )(li, q, k, v, la, s0)


def _scan_ssd_kernel(li_ref, c_ref, b_ref, x_ref, a_ref, s0_ref, y_ref, sn_ref, s_scr, *, c, nch, lv):
    del li_ref
    t = pl.program_id(1)

    @pl.when(t == 0)
    def _():
        s_scr[...] = s0_ref[0, 0]

    mxu = BF16 if c >= 16 else F32
    row = lax.broadcasted_iota(jnp.int32, (c, 1), 0)
    colc = lax.broadcasted_iota(jnp.int32, (1, c), 1)
    causal = row >= colc
    tri = jnp.where(causal, 1.0, 0.0).astype(BF16)
    tri_t = jnp.where(row <= colc, 1.0, 0.0).astype(BF16)
    hpg = SSD_H // SSD_G

    def chunk(ci, carry):
        r0 = pl.multiple_of(ci * c, c)
        cm = c_ref[0, pl.ds(r0, c), :]
        bm = b_ref[0, pl.ds(r0, c), :]
        x = x_ref[0, pl.ds(r0, c), :]
        a = a_ref[0, pl.ds(r0, c), :]
        if lv < c:
            valid = row < lv
            a = jnp.where(valid, a, 0.0)
            bm = jnp.where(valid, bm, 0.0)
        ah, am, al = _split3(a)
        bcol = _dot(tri, ah) + _dot(tri, am) + _dot(tri, al)
        brow = _dot(ah, tri_t, _TN) + _dot(am, tri_t, _TN) + _dot(al, tri_t, _TN)
        for g in range(SSD_G):
            cg = cm[:, g * SSD_N:(g + 1) * SSD_N]
            bg = bm[:, g * SSD_N:(g + 1) * SSD_N]
            cgm = cg.astype(mxu)
            gmat = _dot(cgm, bg.astype(mxu), _NT)
            for hh in range(hpg):
                h = g * hpg + hh
                bc = bcol[:, h:h + 1]
                br = brow[h:h + 1, :]
                w = jnp.exp(jnp.where(causal, bc - br, NEG))
                xh = x[:, h * SSD_P:(h + 1) * SSD_P].astype(mxu)
                S = s_scr[h]
                y = _dot((gmat * w).astype(mxu), xh) + jnp.exp(bc) * _dot(cgm, S.astype(mxu))
                y_ref[0, pl.ds(r0, c), h * SSD_P:(h + 1) * SSD_P] = y
                bl = bc[c - 1:c, :]
                upd = _dot((bg * jnp.exp(bl - bc)).astype(mxu), xh, _TN)
                s_scr[h] = jnp.exp(bl) * S + upd
        return carry

    lax.fori_loop(0, nch, chunk, 0)

    @pl.when(t == pl.num_programs(1) - 1)
    def _():
        sn_ref[0] = s_scr[...]


def _scan_ssd(cm, bm, xdt, a, s0, li, lv):
    B, L, _ = cm.shape
    c = _pick(L, (64, 32, 16, 8))
    tl = _pick(L, (512, 256, 128, 64, 32, 16, 8))
    wg = SSD_G * SSD_N
    if lv is None:
        lv = c
    return pl.pallas_call(
        functools.partial(_scan_ssd_kernel, c=c, nch=tl // c, lv=lv),
        out_shape=(jax.ShapeDtypeStruct((B, L, SSD_DI), F32),
                   jax.ShapeDtypeStruct((B, SSD_H, SSD_N, SSD_P), F32)),
        grid_spec=pltpu.PrefetchScalarGridSpec(
            num_scalar_prefetch=1,
            grid=(B, L // tl),
            in_specs=[pl.BlockSpec((1, tl, wg), lambda b, t, li: (b, t, 0)),
                      pl.BlockSpec((1, tl, wg), lambda b, t, li: (b, t, 0)),
                      pl.BlockSpec((1, tl, SSD_DI), lambda b, t, li: (b, t, 0)),
                      pl.BlockSpec((1, tl, SSD_H), lambda b, t, li: (b, t, 0)),
                      pl.BlockSpec((1, 1, SSD_H, SSD_N, SSD_P), lambda b, t, li: (li[0], b, 0, 0, 0))],
            out_specs=(pl.BlockSpec((1, tl, SSD_DI), lambda b, t, li: (b, t, 0)),
                       pl.BlockSpec((1, SSD_H, SSD_N, SSD_P), lambda b, t, li: (b, 0, 0, 0))),
            scratch_shapes=[pltpu.VMEM((SSD_H, SSD_N, SSD_P), F32)]),
        compiler_params=_cparams("parallel", "arbitrary"),
        name="scan_ssd",
    )(li, cm, bm, xdt, a, s0)


def _rope(x, pos, rot_dim, theta):
    half = rot_dim // 2
    inv = theta ** (-jnp.arange(half, dtype=F32) / half)
    ang = pos.astype(F32)[:, None] * inv[None, :]
    shape = (1, ang.shape[0]) + (1,) * (x.ndim - 3) + (half,)
    cos = jnp.cos(ang).reshape(shape)
    sin = jnp.sin(ang).reshape(shape)
    x1, x2, rest = x[..., :half], x[..., half:rot_dim], x[..., rot_dim:]
    return jnp.concatenate([x1 * cos - x2 * sin, x1 * sin + x2 * cos, rest], axis=-1)


def _prep_weights(w_in, gla_w_a2, w_up, w_o, w_cq, w_ck, w_cv, w_co, w_mlp1, w_mlp2):
    offs = {}
    o = 0
    names = ("da_q", "da_k", "da_v", "rt_q", "rt_k", "rt_v", "rt_g", "gl_q", "gl_k", "gl_v", "gl_r",
             "gl_lr", "sd_z", "sd_xbc", "sd_dt", "gate")
    d_model = w_in.shape[1]
    sizes = (512, 512, 512, 256, 256, 512, 512, 256, 256, 512, 512, GL_RANK, SSD_DI, CONV_CH, SSD_H,
             N_BRANCH * d_model)
    for n, s in zip(names, sizes):
        offs[n] = (o, o + s)
        o += s
    depth = w_in.shape[0]
    main = w_in[:, :, offs["da_q"][0]:offs["gl_r"][1]]
    small = jnp.concatenate(
        [w_in[:, :, offs["gl_lr"][0]:offs["gl_lr"][1]], w_in[:, :, offs["sd_dt"][0]:offs["sd_dt"][1]],
         jnp.zeros((depth, d_model, LANES - GL_RANK - SSD_H), w_in.dtype)], axis=-1)
    ssd = w_in[:, :, offs["sd_z"][0]:offs["sd_xbc"][1]]
    gate = w_in[:, :, offs["gate"][0]:offs["gate"][1]]
    a2 = jnp.concatenate(
        [gla_w_a2, jnp.zeros((depth, LANES - GL_RANK, gla_w_a2.shape[2]), gla_w_a2.dtype)], axis=1)
    c = lambda w: w.astype(BF16)
    return dict(main=c(main), small=c(small), ssd=c(ssd), gate=c(gate), a2=c(a2), w_up=c(w_up), w_o=c(w_o),
                w_cq=c(w_cq), w_ck=c(w_ck), w_cv=c(w_cv), w_co=c(w_co), w_mlp1=c(w_mlp1), w_mlp2=c(w_mlp2))


def _layer(l, x, xb, pos, W, P, mk, mv, mem_li, past, states, lv, alpha):
    B, L, D = x.shape
    M = B * L
    s_ret, s_gla, s_ssm, conv_buf, st_li = states
    xb2 = xb.reshape(M, D)
    main = _mm(xb2, W["main"][l]).reshape(B, L, -1)
    small = _mm(xb2, W["small"][l])
    ssd = _mm(xb2, W["ssd"][l]).reshape(B, L, -1)
    gate = _mm(xb2, W["gate"][l])
    (da_q, da_k, da_v, rt_q, rt_k, rt_v, rt_g, gl_q, gl_k, gl_v, gl_r) = jnp.split(
        main, [512, 1024, 1536, 1792, 2048, 2560, 3072, 3328, 3584, 4096], axis=-1)

    q = _rope(da_q.reshape(B, L, DA_H, 2, DA_DH), pos, DA_ROT, ROPE_THETA)
    k = _rope(da_k.reshape(B, L, DA_H, 2, DA_DH), pos, DA_ROT, ROPE_THETA)
    lam_init = 0.8 - 0.6 * math.exp(-0.3 * l)
    lam = (jnp.exp(jnp.sum((P["da_lq1"][l] * P["da_lk1"][l]).astype(F32)))
           - jnp.exp(jnp.sum((P["da_lq2"][l] * P["da_lk2"][l]).astype(F32))) + lam_init)
    sc = jnp.stack([lam, jnp.asarray(1.0 - lam_init, F32)]).astype(F32)
    k_rows = k.reshape(B, L, DA_H, 2 * DA_DH)
    v_rows = da_v.reshape(B, L, DA_H, DA_DV)
    qs = q.reshape(B, L, 512) * (DA_DH ** -0.5)
    if past is None:
        o_da = _flash_da(qs.astype(BF16), k.reshape(B, L, 512).astype(BF16), da_v.astype(BF16), sc)
    else:
        cache_k, cache_v, page_table, cache_li = past
        t_real = lv
        q5 = qs.reshape(B, L, DA_H, 2, DA_DH)
        eye_h = jnp.eye(DA_H, dtype=F32)
        eye_c = jnp.eye(2, dtype=F32)
        qblk = jnp.einsum("bthcd,hg,ce->bhcdegt", q5, eye_h, eye_c).reshape(B, 512, 2 * DA_H * L)
        qblk = jnp.pad(qblk, ((0, 0), (0, 0), (0, LANES - 2 * DA_H * L))).astype(BF16)
        del t_real
        o_da = _decode_da(qblk, k.reshape(B, L, 512), da_v, cache_k, cache_v, page_table, cache_li, sc)
    o_da = o_da.astype(BF16)

    rq = _rope(rt_q.reshape(B, L, RT_H, RT_DK), pos, RT_DK, RET_THETA).reshape(B, L, -1)
    rk = _rope(rt_k.reshape(B, L, RT_H, RT_DK), pos, RT_DK, RET_THETA).reshape(B, L, -1) * (RT_DK ** -0.5)
    log_gamma = jnp.log1p(-jnp.exp2(-5.0 - jnp.arange(RT_H, dtype=F32)))
    la_ret = jnp.broadcast_to(jnp.repeat(log_gamma, RT_DK)[None, None, :], (B, L, RT_H * RT_DK))
    o_rt, s_ret_new = _scan_qk(rq, rk, rt_v, la_ret, s_ret, st_li, lv)
    o_rt = (o_rt * jax.nn.silu(rt_g)).astype(BF16)

    logits = _mm(small.astype(BF16), W["a2"][l]).reshape(B, L, -1) + P["gla_b_a"][l]
    la = jax.nn.log_sigmoid(logits) / GLA_TEMP
    o_gl, s_gla_new = _scan_qk(gl_q, gl_k * (GL_DK ** -0.5), gl_v, la, s_gla, st_li, lv)
    o_gl = (o_gl * jax.nn.silu(gl_r)).astype(BF16)

    sd_z = ssd[..., :SSD_DI]
    sd_xbc = ssd[..., SSD_DI:]
    sd_dt = small[:, GL_RANK:GL_RANK + SSD_H].reshape(B, L, SSD_H)
    xpad = jnp.concatenate([conv_buf, sd_xbc], axis=1)
    cw = P["ssd_conv_w"][l]
    xbc = jax.nn.silu(sum(xpad[:, j:j + L] * cw[j] for j in range(SSD_CONV)) + P["ssd_conv_b"][l])
    xs, bmat, cmat = jnp.split(xbc, [SSD_DI, SSD_DI + SSD_G * SSD_N], axis=-1)
    dt = jax.nn.softplus(sd_dt + P["ssd_dt_bias"][l])
    a_log = -jnp.exp(P["ssd_a_log"][l].astype(F32))
    xh = xs.reshape(B, L, SSD_H, SSD_P)
    xdt = (xh * dt[..., None]).reshape(B, L, SSD_DI)
    y, s_ssm_new = _scan_ssd(cmat, bmat, xdt, dt * a_log, s_ssm, st_li, lv)
    y = y + (P["ssd_d"][l][:, None] * xh).reshape(B, L, SSD_DI)
    gsz = SSD_DI // SSD_G
    gn = (y * jax.nn.silu(sd_z)).reshape(B, L, SSD_G, gsz)
    gn = gn * lax.rsqrt(jnp.mean(gn * gn, axis=-1, keepdims=True) + EPS)
    o_sd = (gn.reshape(B, L, SSD_DI) * P["ssd_norm_w"][l]).astype(BF16)

    br = jnp.concatenate([o_da, o_rt, o_gl, o_sd], axis=-1).reshape(M, N_BRANCH * BR_W)
    x2d = x.reshape(M, D)
    x1, x1b = _merge_ln(br, gate, W["w_up"][l], W["w_o"][l], x2d, P["ln1_g"][l], P["ln1_b"][l], alpha)

    cq = _mm(x1b, W["w_cq"][l]).reshape(B, L, -1)
    oc = _xattn(cq, mk, mv, mem_li).reshape(M, -1)
    x2, x2b = _mm_res_ln(oc, W["w_co"][l], x1, P["ln2_g"][l], P["ln2_b"][l], alpha)

    hmid = _mm(x2b, W["w_mlp1"][l], act="relu2", out_dtype=BF16)
    x3, x3b = _mm_res_ln(hmid, W["w_mlp2"][l], x2, P["ln3_g"][l], P["ln3_b"][l], alpha)
    return (x3.reshape(B, L, D), x3b.reshape(B, L, D),
            (k_rows, v_rows, s_ret_new, s_gla_new, s_ssm_new, xpad))


def kernel(x_prompt, x_sample, mem_prompt, cache_diff_k, cache_diff_v, page_table, state_ret, state_gla, state_ssm, state_conv, cache_mem_k, cache_mem_v, w_in, da_lq1, da_lk1, da_lq2, da_lk2, gla_w_a2, gla_b_a, ssd_conv_w, ssd_conv_b, ssd_dt_bias, ssd_a_log, ssd_d, ssd_norm_w, w_up, w_o, ln1_g, ln1_b, w_cq, w_ck, w_cv, w_co, ln2_g, ln2_b, w_mlp1, w_mlp2, ln3_g, ln3_b):
    depth = w_in.shape[0]
    alpha = (2 * depth) ** 0.25
    P = dict(da_lq1=da_lq1, da_lk1=da_lk1, da_lq2=da_lq2, da_lk2=da_lk2, gla_b_a=gla_b_a,
             ssd_conv_w=ssd_conv_w, ssd_conv_b=ssd_conv_b, ssd_dt_bias=ssd_dt_bias, ssd_a_log=ssd_a_log,
             ssd_d=ssd_d, ssd_norm_w=ssd_norm_w, ln1_g=ln1_g, ln1_b=ln1_b, ln2_g=ln2_g, ln2_b=ln2_b,
             ln3_g=ln3_g, ln3_b=ln3_b)
    W = _prep_weights(w_in, gla_w_a2, w_up, w_o, w_cq, w_ck, w_cv, w_co, w_mlp1, w_mlp2)
    zero_li = jnp.zeros((1,), jnp.int32)

    B, L, D = x_prompt.shape
    pos_p = jnp.arange(L, dtype=jnp.int32)
    memb = mem_prompt.reshape(-1, D).astype(BF16)
    n_mem = mem_prompt.shape[1]
    z_ret = jnp.zeros((1, B, 2, 2 * RT_DK, RT_DV), F32)
    z_ssm = jnp.zeros((1, B, SSD_H, SSD_N, SSD_P), F32)
    z_conv = jnp.zeros((B, SSD_CONV - 1, CONV_CH), F32)
    x, xb = x_prompt, x_prompt.astype(BF16)
    outs_p = [[] for _ in range(8)]
    for l in range(depth):
        mk = _mm(memb, W["w_ck"][l]).reshape(1, B, n_mem, MA_H * MA_DH)
        mv = _mm(memb, W["w_cv"][l]).reshape(1, B, n_mem, MA_H * MA_DH)
        x, xb, (kr, vr, sr, sg, ss, xpad) = _layer(
            l, x, xb, pos_p, W, P, mk, mv, zero_li, None, (z_ret, z_ret, z_ssm, z_conv, zero_li), None, alpha)
        new = (kr, vr, sr.reshape(B, RT_H, RT_DK, RT_DV), sg.reshape(B, GL_H, GL_DK, GL_DV), ss, xpad[:, L:],
               mk.reshape(B, n_mem, MA_H, MA_DH), mv.reshape(B, n_mem, MA_H, MA_DH))
        for lst, val in zip(outs_p, new):
            lst.append(val)
    y_prompt = x

    Bd, Ld, _ = x_sample.shape
    t_pad = -(-Ld // SUBLANES) * SUBLANES
    n_pages, page = page_table.shape[1], cache_diff_k.shape[2]
    past_len = n_pages * page
    pos_s = past_len + jnp.arange(t_pad, dtype=jnp.int32)
    x = jnp.pad(x_sample, ((0, 0), (0, t_pad - Ld), (0, 0)))
    xb = x.astype(BF16)
    n_pool = cache_diff_k.shape[1]
    ck = cache_diff_k.reshape(depth, n_pool, page, DA_H * 2 * DA_DH)
    cv = cache_diff_v.reshape(depth, n_pool, page, DA_H * DA_DV)
    s_ret_all = state_ret.reshape(depth, Bd, 2, 2 * RT_DK, RT_DV)
    s_gla_all = state_gla.reshape(depth, Bd, 2, 2 * GL_DK, GL_DV)
    cmk = cache_mem_k.reshape(depth, Bd, cache_mem_k.shape[2], MA_H * MA_DH)
    cmv = cache_mem_v.reshape(depth, Bd, cache_mem_v.shape[2], MA_H * MA_DH)
    outs_s = [[] for _ in range(6)]
    for l in range(depth):
        li = jnp.full((1,), l, jnp.int32)
        x, xb, (kr, vr, sr, sg, ss, xpad) = _layer(
            l, x, xb, pos_s, W, P, cmk, cmv, li, (ck, cv, page_table, li),
            (s_ret_all, s_gla_all, state_ssm, state_conv[l], li), Ld, alpha)
        nc = SSD_CONV - 1
        new = (kr[:, :Ld], vr[:, :Ld], sr.reshape(Bd, RT_H, RT_DK, RT_DV), sg.reshape(Bd, GL_H, GL_DK, GL_DV),
               ss, xpad[:, Ld:Ld + nc])
        for lst, val in zip(outs_s, new):
            lst.append(val)
    y_sample = x[:, :Ld]

    st = lambda lst: jnp.stack(lst)
    return (y_prompt, y_sample) + tuple(st(o) for o in outs_p) + tuple(st(o) for o in outs_s)
```

```python
import functools
import math

import jax
import jax.numpy as jnp
from jax import lax
from jax.experimental import pallas as pl
from jax.experimental.pallas import tpu as pltpu

F32 = jnp.float32
BF16 = jnp.bfloat16

DA_H, DA_DH, DA_DV = 4, 64, 128
DA_ROT = DA_DH // 4
ROPE_THETA = 500000.0
RT_H, RT_DK, RT_DV = 4, 64, 128
RET_THETA = 10000.0
GL_H, GL_DK, GL_DV, GL_RANK = 4, 64, 128, 16
GLA_TEMP = 16.0
SSD_H, SSD_P, SSD_N, SSD_G, SSD_CONV = 8, 64, 128, 2, 4
SSD_DI = SSD_H * SSD_P
CONV_CH = SSD_DI + 2 * SSD_G * SSD_N
MA_H, MA_DH = 4, 128
N_BRANCH, BR_W = 4, 512
EPS = 1e-5
NEG = -1e30

VMEM_LIMIT_BYTES = 56 * 1024 * 1024
LANES = 128
SUBLANES = 8

_NT = (((1,), (1,)), ((), ()))
_TN = (((0,), (0,)), ((), ()))


def _cparams(*sem):
    return pltpu.CompilerParams(dimension_semantics=sem, vmem_limit_bytes=VMEM_LIMIT_BYTES)


def _pick(n, cands):
    for c in cands:
        if n % c == 0:
            return c
    return n


def _split3(x):
    h = x.astype(BF16)
    r = x - h.astype(F32)
    m = r.astype(BF16)
    lo = (r - m.astype(F32)).astype(BF16)
    return h, m, lo


def _dot(a, b, dims=None):
    if dims is None:
        return jnp.dot(a, b, preferred_element_type=F32)
    return lax.dot_general(a, b, dims, preferred_element_type=F32)


def _mm_kernel(x_ref, w_ref, o_ref, *, act):
    acc = _dot(x_ref[...], w_ref[...])
    if act == "relu2":
        acc = jnp.square(jnp.maximum(acc, 0.0))
    o_ref[...] = acc.astype(o_ref.dtype)


def _mm(x, w, act=None, out_dtype=F32):
    M, K = x.shape
    N = w.shape[1]
    tm = _pick(M, (1024, 512, 256, 128, 64, 32, 16, 8))
    tn = _pick(N, (1024, 768, 512, 384, 256, 128))
    return pl.pallas_call(
        functools.partial(_mm_kernel, act=act),
        out_shape=jax.ShapeDtypeStruct((M, N), out_dtype),
        grid=(N // tn, M // tm),
        in_specs=[pl.BlockSpec((tm, K), lambda j, i: (i, 0)),
                  pl.BlockSpec((K, tn), lambda j, i: (0, j))],
        out_specs=pl.BlockSpec((tm, tn), lambda j, i: (i, j)),
        compiler_params=_cparams("parallel", "parallel"),
        name="mm",
    )(x, w)


def _ln_rows(z, g, b):
    mu = jnp.mean(z, axis=-1, keepdims=True)
    zc = z - mu
    var = jnp.mean(zc * zc, axis=-1, keepdims=True)
    return zc * lax.rsqrt(var + EPS) * g + b


def _mm_res_ln_kernel(x_ref, w_ref, r_ref, g_ref, b_ref, o_ref, ob_ref, *, alpha):
    y = _dot(x_ref[...], w_ref[...])
    out = _ln_rows(alpha * r_ref[...] + y, g_ref[...], b_ref[...])
    o_ref[...] = out
    ob_ref[...] = out.astype(BF16)


def _mm_res_ln(x, w, res, g, b, alpha):
    M, K = x.shape
    N = w.shape[1]
    tm = _pick(M, (512, 256, 128, 64, 32, 16, 8))
    return pl.pallas_call(
        functools.partial(_mm_res_ln_kernel, alpha=alpha),
        out_shape=(jax.ShapeDtypeStruct((M, N), F32), jax.ShapeDtypeStruct((M, N), BF16)),
        grid=(M // tm,),
        in_specs=[pl.BlockSpec((tm, K), lambda i: (i, 0)),
                  pl.BlockSpec((K, N), lambda i: (0, 0)),
                  pl.BlockSpec((tm, N), lambda i: (i, 0)),
                  pl.BlockSpec((1, N), lambda i: (0, 0)),
                  pl.BlockSpec((1, N), lambda i: (0, 0))],
        out_specs=(pl.BlockSpec((tm, N), lambda i: (i, 0)), pl.BlockSpec((tm, N), lambda i: (i, 0))),
        compiler_params=_cparams("parallel"),
        name="mm_res_ln",
    )(x, w, res, g.reshape(1, N), b.reshape(1, N))


def _merge_ln_kernel(b0_ref, b1_ref, b2_ref, b3_ref, gate_ref, wup_ref, wo_ref, r_ref, g_ref, b_ref,
                     o_ref, ob_ref, *, alpha):
    d = wo_ref.shape[0]
    acc = None
    for n, br_ref in enumerate((b0_ref, b1_ref, b2_ref, b3_ref)):
        up = _dot(br_ref[...], wup_ref[n])
        term = jax.nn.sigmoid(gate_ref[:, n * d:(n + 1) * d]) * up
        acc = term if acc is None else acc + term
    y = _dot(acc.astype(BF16), wo_ref[...])
    out = _ln_rows(alpha * r_ref[...] + y, g_ref[...], b_ref[...])
    o_ref[...] = out
    ob_ref[...] = out.astype(BF16)


def _merge_ln(branches, gate, w_up, w_o, res, g, b, alpha):
    M = gate.shape[0]
    D = w_o.shape[0]
    tm = _pick(M, (256, 128, 64, 32, 16, 8))
    br_spec = pl.BlockSpec((tm, BR_W), lambda i: (i, 0))
    return pl.pallas_call(
        functools.partial(_merge_ln_kernel, alpha=alpha),
        out_shape=(jax.ShapeDtypeStruct((M, D), F32), jax.ShapeDtypeStruct((M, D), BF16)),
        grid=(M // tm,),
        in_specs=[br_spec, br_spec, br_spec, br_spec,
                  pl.BlockSpec((tm, N_BRANCH * D), lambda i: (i, 0)),
                  pl.BlockSpec((N_BRANCH, BR_W, D), lambda i: (0, 0, 0)),
                  pl.BlockSpec((D, D), lambda i: (0, 0)),
                  pl.BlockSpec((tm, D), lambda i: (i, 0)),
                  pl.BlockSpec((1, D), lambda i: (0, 0)),
                  pl.BlockSpec((1, D), lambda i: (0, 0))],
        out_specs=(pl.BlockSpec((tm, D), lambda i: (i, 0)), pl.BlockSpec((tm, D), lambda i: (i, 0))),
        compiler_params=_cparams("parallel"),
        name="merge_ln",
    )(*branches, gate, w_up, w_o, res, g.reshape(1, D), b.reshape(1, D))


def _xattn_kernel(li_ref, q_ref, k_ref, v_ref, o_ref):
    del li_ref
    scale = MA_DH ** -0.5
    for h in range(MA_H):
        sl = slice(h * MA_DH, (h + 1) * MA_DH)
        q = q_ref[0, :, sl].astype(BF16)
        k = k_ref[0, 0, :, h, :].astype(BF16)
        v = v_ref[0, 0, :, h, :].astype(BF16)
        s = _dot(q, k, _NT) * scale
        m = jnp.max(s, axis=-1, keepdims=True)
        p = jnp.exp(s - m)
        a = p / jnp.sum(p, axis=-1, keepdims=True)
        o_ref[0, :, sl] = _dot(a.astype(BF16), v).astype(o_ref.dtype)


def _xattn(q, mk, mv, li):
    B, L, W = q.shape
    nm = mk.shape[2]
    tq = _pick(L, (512, 256, 128, 64, 32, 16, 8))
    kv_spec = pl.BlockSpec((1, 1, nm, MA_H, MA_DH), lambda b, i, li: (li[0], b, 0, 0, 0))
    return pl.pallas_call(
        _xattn_kernel,
        out_shape=jax.ShapeDtypeStruct((B, L, W), BF16),
        grid_spec=pltpu.PrefetchScalarGridSpec(
            num_scalar_prefetch=1,
            grid=(B, L // tq),
            in_specs=[pl.BlockSpec((1, tq, W), lambda b, i, li: (b, i, 0)), kv_spec, kv_spec],
            out_specs=pl.BlockSpec((1, tq, W), lambda b, i, li: (b, i, 0))),
        compiler_params=_cparams("parallel", "parallel"),
        name="xattn",
    )(li, q, mk, mv)


def _head_rms(o):
    return o * lax.rsqrt(jnp.mean(o * o, axis=-1, keepdims=True) + EPS)


def _flash_da_kernel(sc_ref, q_ref, k_ref, v_ref, o_ref, *, tq, tk):
    i = pl.program_id(2)
    lam = sc_ref[0]
    out_scale = sc_ref[1]
    q = q_ref[0]
    lo = lax.broadcasted_iota(jnp.int32, (1, LANES), 1) < DA_DH
    zero = jnp.zeros_like(q)
    q12 = jnp.concatenate([jnp.where(lo, q, zero), jnp.where(lo, zero, q)], axis=0)
    row = lax.broadcasted_iota(jnp.int32, (2 * tq, 1), 0)
    qpos = i * tq + jnp.where(row >= tq, row - tq, row)
    last = (i * tq) // tk

    def qk(j):
        start = pl.multiple_of(j * tk, tk)
        return _dot(q12, k_ref[0, pl.ds(start, tk), :], _NT)

    def update(s, j, m, l, acc):
        start = pl.multiple_of(j * tk, tk)
        vb = v_ref[0, pl.ds(start, tk), :]
        m_new = jnp.maximum(m, jnp.max(s, axis=-1, keepdims=True))
        alpha = jnp.exp2(m - m_new)
        p = jnp.exp2(s - m_new)
        l = alpha * l + jnp.sum(p, axis=-1, keepdims=True)
        acc = alpha * acc + _dot(p.astype(BF16), vb)
        return m_new, l, acc

    def body(j, carry):
        s, m, l, acc = carry
        s_next = qk(j + 1)
        m, l, acc = update(s, j, m, l, acc)
        return s_next, m, l, acc

    init = (qk(0), jnp.full((2 * tq, 1), NEG, F32), jnp.zeros((2 * tq, 1), F32),
            jnp.zeros((2 * tq, LANES), F32))
    s, m, l, acc = lax.fori_loop(0, last, body, init)
    kpos = last * tk + lax.broadcasted_iota(jnp.int32, (1, tk), 1)
    s = jnp.where(kpos <= qpos, s, NEG)
    _, l, acc = update(s, last, m, l, acc)
    o12 = acc / l
    o = o12[:tq] - lam * o12[tq:]
    o_ref[0] = (_head_rms(o) * out_scale).astype(o_ref.dtype)


def _flash_da(q, k, v, sc):
    B, L, W = q.shape
    tq = _pick(L, (256, 128, 64, 32, 16, 8))
    tk = _pick(L, (512, 256, 128, 64, 32, 16, 8))
    return pl.pallas_call(
        functools.partial(_flash_da_kernel, tq=tq, tk=tk),
        out_shape=jax.ShapeDtypeStruct((B, L, W), BF16),
        grid=(B, DA_H, L // tq),
        in_specs=[pl.BlockSpec(memory_space=pltpu.SMEM),
                  pl.BlockSpec((1, tq, LANES), lambda b, h, i: (b, i, h)),
                  pl.BlockSpec((1, L, LANES), lambda b, h, i: (b, 0, h)),
                  pl.BlockSpec((1, L, LANES), lambda b, h, i: (b, 0, h))],
        out_specs=pl.BlockSpec((1, tq, LANES), lambda b, h, i: (b, i, h)),
        compiler_params=_cparams("parallel", "parallel", "arbitrary"),
        name="flash_da",
    )(sc, q, k, v)


def _decode_da_kernel(pt_ref, li_ref, sc_ref, qb_ref, kn_ref, vn_ref, *refs, n_pages, t_pad):
    del pt_ref, li_ref
    k_refs = refs[:n_pages]
    v_refs = refs[n_pages:2 * n_pages]
    o_ref = refs[2 * n_pages]
    lam = sc_ref[0]
    out_scale = sc_ref[1]
    qb = qb_ref[0]
    hd = 2 * DA_DH

    def page_scores(k_ref):
        s = None
        for h in range(DA_H):
            t = _dot(k_ref[0, 0, :, h, :].astype(BF16), qb[h * hd:(h + 1) * hd, :])
            s = t if s is None else s + t
        return s

    s_pages = [page_scores(k_refs[p]) for p in range(n_pages)]
    kn = kn_ref[0]
    vn = vn_ref[0]
    s_new = _dot(kn.astype(BF16), qb)
    key = lax.broadcasted_iota(jnp.int32, (t_pad, LANES), 0)
    col = lax.broadcasted_iota(jnp.int32, (t_pad, LANES), 1)
    s_new = jnp.where(key <= col % t_pad, s_new, NEG)

    m = jnp.max(s_new, axis=0, keepdims=True)
    for s in s_pages:
        m = jnp.maximum(m, jnp.max(s, axis=0, keepdims=True))

    ones = jnp.ones((k_refs[0].shape[2], LANES), BF16)
    p_new = jnp.exp(s_new - m)
    acc = [_dot(p_new, vn[:, h * DA_DV:(h + 1) * DA_DV], _TN) for h in range(DA_H)]
    den = _dot(p_new, jnp.ones((t_pad, LANES), F32), _TN)
    for p in range(n_pages):
        pp = jnp.exp(s_pages[p] - m).astype(BF16)
        for h in range(DA_H):
            acc[h] = acc[h] + _dot(pp, v_refs[p][0, 0, :, h, :].astype(BF16), _TN)
        den = den + _dot(pp, ones, _TN)
    half = DA_H * t_pad
    for h in range(DA_H):
        oh = acc[h] / den
        r0 = h * t_pad
        d = oh[r0:r0 + t_pad] - lam * oh[half + r0:half + r0 + t_pad]
        o_ref[0, :, h * DA_DV:(h + 1) * DA_DV] = _head_rms(d) * out_scale


def _decode_da(qblk, k_new, v_new, cache_k, cache_v, page_table, li, sc):
    bd, t_pad, w = k_new.shape
    n_pages = page_table.shape[1]
    page = cache_k.shape[2]

    def page_spec(p):
        return pl.BlockSpec((1, 1, page, DA_H, DA_DV), lambda b, pt, li: (li[0], pt[b, p], 0, 0, 0))

    in_specs = [pl.BlockSpec(memory_space=pltpu.SMEM),
                pl.BlockSpec((1, w, LANES), lambda b, pt, li: (b, 0, 0)),
                pl.BlockSpec((1, t_pad, w), lambda b, pt, li: (b, 0, 0)),
                pl.BlockSpec((1, t_pad, w), lambda b, pt, li: (b, 0, 0))]
    in_specs += [page_spec(p) for p in range(n_pages)] * 2
    return pl.pallas_call(
        functools.partial(_decode_da_kernel, n_pages=n_pages, t_pad=t_pad),
        out_shape=jax.ShapeDtypeStruct((bd, t_pad, w), F32),
        grid_spec=pltpu.PrefetchScalarGridSpec(
            num_scalar_prefetch=2,
            grid=(bd,),
            in_specs=in_specs,
            out_specs=pl.BlockSpec((1, t_pad, w), lambda b, pt, li: (b, 0, 0))),
        compiler_params=_cparams("parallel"),
        name="decode_da",
    )(page_table, li, sc, qblk, k_new, v_new, *([cache_k] * n_pages), *([cache_v] * n_pages))


def _cumsum_rows(x, tri):
    h, m, lo = _split3(x)
    return _dot(tri, h) + _dot(tri, m) + _dot(tri, lo)


def _scan_qk_kernel(li_ref, q_ref, k_ref, v_ref, g_ref, cos_ref, sin_ref, lr_ref, a2_ref, ba_ref, s0_ref,
                    o_ref, sn_ref, s_scr, *, c, nch, bb, lv, mode):
    del li_ref
    t = pl.program_id(1)

    @pl.when(t == 0)
    def _():
        s_scr[...] = s0_ref[0]

    mxu = BF16 if c >= 16 else F32
    row = lax.broadcasted_iota(jnp.int32, (c, 1), 0)
    colc = lax.broadcasted_iota(jnp.int32, (1, c), 1)
    causal = row >= colc
    tri = jnp.where(causal, 1.0, 0.0).astype(BF16)
    lo = lax.broadcasted_iota(jnp.int32, (1, LANES), 1) < RT_DK
    wqk = RT_H * RT_DK
    half = RT_DK // 2
    first_half = lax.broadcasted_iota(jnp.int32, (1, wqk), 1) % RT_DK < half
    mid = c // 2 - 1

    def rotary(x, cos, sin):
        partner = jnp.where(first_half, pltpu.roll(x, wqk - half, 1), pltpu.roll(x, half, 1))
        return x * cos + partner * sin

    def chunk(bi, ci):
        rows = slice(ci * c, (ci + 1) * c)
        q = q_ref[bi, rows, :]
        k = k_ref[bi, rows, :]
        v = v_ref[bi, rows, :]
        if mode == "ret":
            cos = cos_ref[rows, :]
            sin = sin_ref[rows, :]
            q = rotary(q, cos, sin)
            k = rotary(k, cos, sin)
            la = jnp.broadcast_to(ba_ref[...], (c, wqk))
        else:
            x = _dot(lr_ref[bi, rows, :].astype(BF16), a2_ref[...]) + ba_ref[...]
            la = jax.nn.log_sigmoid(x) * (1.0 / GLA_TEMP)
        k = k * (RT_DK ** -0.5)
        if lv < c:
            valid = row < lv
            la = jnp.where(valid, la, 0.0)
            k = jnp.where(valid, k, 0.0)
        b = _cumsum_rows(la, tri)
        bm = b[mid:mid + 1, :]
        bl = b[c - 1:c, :]
        qs = q * jnp.exp(b - bm)
        ks = k * jnp.exp(bm - b)
        kh = k * jnp.exp(bl - b)
        qe = q * jnp.exp(b)
        g = g_ref[bi, rows, :]
        for p in range(2):
            sl = slice(p * LANES, (p + 1) * LANES)
            S = s_scr[bi, p]
            Sm = S.astype(mxu)
            ksp = ks[:, sl].astype(mxu)
            qsp = qs[:, sl]
            qep = qe[:, sl]
            for hh in range(2):
                h = 2 * p + hh
                hs = slice(h * RT_DV, (h + 1) * RT_DV)
                msk = lo if hh == 0 else jnp.logical_not(lo)
                s = _dot(jnp.where(msk, qsp, 0.0).astype(mxu), ksp, _NT)
                s = jnp.where(causal, s, 0.0)
                o = _dot(s.astype(mxu), v[:, hs].astype(mxu)) + _dot(jnp.where(msk, qep, 0.0).astype(mxu), Sm)
                gh = g[:, hs]
                o_ref[bi, rows, hs] = (_head_rms(o) * (gh * jax.nn.sigmoid(gh))).astype(o_ref.dtype)
            upd = _dot(kh[:, sl].astype(mxu), v[:, 2 * p * RT_DV:(2 * p + 2) * RT_DV].astype(mxu), _TN)
            dcol = jnp.exp(b[:, sl].T[:, c - 1:c])
            s_scr[bi, p] = dcol * S + jnp.concatenate(
                [upd[0:RT_DK, 0:RT_DV], upd[RT_DK:2 * RT_DK, RT_DV:2 * RT_DV]], axis=0)

    for bi in range(bb):
        for ci in range(nch):
            chunk(bi, ci)

    @pl.when(t == pl.num_programs(1) - 1)
    def _():
        sn_ref[...] = s_scr[...]


def _scan_qk(main, cols, tables, lr, a2, ba, s0, li, lv, mode):
    B, L, _ = main.shape
    c = _pick(L, (128, 64, 32, 16, 8))
    tl = _pick(L, (512, 256, 128, 64, 32, 16, 8))
    bb = _pick(B, (2, 1)) if L > SUBLANES else _pick(B, (8, 4, 2, 1))
    wqk = RT_H * RT_DK
    wv = RT_H * RT_DV
    if lv is None:
        lv = c
    qi, ki, vi, gi = cols
    cos, sin = tables

    def col_spec(width, idx):
        return pl.BlockSpec((bb, tl, width), lambda b, t, li: (b, t, idx))

    tab_spec = pl.BlockSpec((tl, wqk), lambda b, t, li: (t, 0))
    return pl.pallas_call(
        functools.partial(_scan_qk_kernel, c=c, nch=tl // c, bb=bb, lv=lv, mode=mode),
        out_shape=(jax.ShapeDtypeStruct((B, L, wv), BF16),
                   jax.ShapeDtypeStruct((B, 2, 2 * RT_DK, RT_DV), F32)),
        grid_spec=pltpu.PrefetchScalarGridSpec(
            num_scalar_prefetch=1,
            grid=(B // bb, L // tl),
            in_specs=[col_spec(wqk, qi), col_spec(wqk, ki), col_spec(wv, vi), col_spec(wv, gi),
                      tab_spec, tab_spec,
                      pl.BlockSpec((bb, tl, LANES), lambda b, t, li: (b, t, 0)),
                      pl.BlockSpec((LANES, wqk), lambda b, t, li: (0, 0)),
                      pl.BlockSpec((1, wqk), lambda b, t, li: (0, 0)),
                      pl.BlockSpec((1, bb, 2, 2 * RT_DK, RT_DV), lambda b, t, li: (li[0], b, 0, 0, 0))],
            out_specs=(pl.BlockSpec((bb, tl, wv), lambda b, t, li: (b, t, 0)),
                       pl.BlockSpec((bb, 2, 2 * RT_DK, RT_DV), lambda b, t, li: (b, 0, 0, 0))),
            scratch_shapes=[pltpu.VMEM((bb, 2, 2 * RT_DK, RT_DV), F32)]),
        compiler_params=_cparams("parallel", "arbitrary"),
        name="scan_qk_" + mode,
    )(li, main, main, main, main, cos, sin, lr, a2, ba, s0)


def _scan_ssd_kernel(li_ref, c_ref, b_ref, x_ref, a_ref, s0_ref, y_ref, sn_ref, s_scr, *, c, nch, bb, lv):
    del li_ref
    t = pl.program_id(1)

    @pl.when(t == 0)
    def _():
        s_scr[...] = s0_ref[0]

    mxu = BF16 if c >= 16 else F32
    row = lax.broadcasted_iota(jnp.int32, (c, 1), 0)
    colc = lax.broadcasted_iota(jnp.int32, (1, c), 1)
    causal = row >= colc
    tri = jnp.where(causal, 1.0, 0.0).astype(BF16)
    tri_t = jnp.where(row <= colc, 1.0, 0.0).astype(BF16)
    hpg = SSD_H // SSD_G

    def chunk(bi, ci):
        rows = slice(ci * c, (ci + 1) * c)
        cm = c_ref[bi, rows, :]
        bm = b_ref[bi, rows, :]
        x = x_ref[bi, rows, :]
        a = a_ref[bi, rows, :]
        if lv < c:
            valid = row < lv
            a = jnp.where(valid, a, 0.0)
            bm = jnp.where(valid, bm, 0.0)
        ah, am, al = _split3(a)
        bcol = _dot(tri, ah) + _dot(tri, am) + _dot(tri, al)
        brow = _dot(ah, tri_t, _TN) + _dot(am, tri_t, _TN) + _dot(al, tri_t, _TN)
        for g in range(SSD_G):
            cg = cm[:, g * SSD_N:(g + 1) * SSD_N]
            bg = bm[:, g * SSD_N:(g + 1) * SSD_N]
            cgm = cg.astype(mxu)
            gmat = _dot(cgm, bg.astype(mxu), _NT)
            for hh in range(hpg):
                h = g * hpg + hh
                bc = bcol[:, h:h + 1]
                br = brow[h:h + 1, :]
                w = jnp.exp(jnp.where(causal, bc - br, NEG))
                xh = x[:, h * SSD_P:(h + 1) * SSD_P].astype(mxu)
                S = s_scr[bi, h]
                y = _dot((gmat * w).astype(mxu), xh) + jnp.exp(bc) * _dot(cgm, S.astype(mxu))
                y_ref[bi, rows, h * SSD_P:(h + 1) * SSD_P] = y
                bl = bc[c - 1:c, :]
                upd = _dot((bg * jnp.exp(bl - bc)).astype(mxu), xh, _TN)
                s_scr[bi, h] = jnp.exp(bl) * S + upd

    for bi in range(bb):
        for ci in range(nch):
            chunk(bi, ci)

    @pl.when(t == pl.num_programs(1) - 1)
    def _():
        sn_ref[...] = s_scr[...]


def _scan_ssd(cm, bm, xdt, a, s0, li, lv):
    B, L, _ = cm.shape
    c = _pick(L, (128, 64, 32, 16, 8))
    tl = _pick(L, (512, 256, 128, 64, 32, 16, 8))
    bb = _pick(B, (2, 1)) if L > SUBLANES else _pick(B, (8, 4, 2, 1))
    wg = SSD_G * SSD_N
    if lv is None:
        lv = c
    return pl.pallas_call(
        functools.partial(_scan_ssd_kernel, c=c, nch=tl // c, bb=bb, lv=lv),
        out_shape=(jax.ShapeDtypeStruct((B, L, SSD_DI), F32),
                   jax.ShapeDtypeStruct((B, SSD_H, SSD_N, SSD_P), F32)),
        grid_spec=pltpu.PrefetchScalarGridSpec(
            num_scalar_prefetch=1,
            grid=(B // bb, L // tl),
            in_specs=[pl.BlockSpec((bb, tl, wg), lambda b, t, li: (b, t, 0)),
                      pl.BlockSpec((bb, tl, wg), lambda b, t, li: (b, t, 0)),
                      pl.BlockSpec((bb, tl, SSD_DI), lambda b, t, li: (b, t, 0)),
                      pl.BlockSpec((bb, tl, SSD_H), lambda b, t, li: (b, t, 0)),
                      pl.BlockSpec((1, bb, SSD_H, SSD_N, SSD_P), lambda b, t, li: (li[0], b, 0, 0, 0))],
            out_specs=(pl.BlockSpec((bb, tl, SSD_DI), lambda b, t, li: (b, t, 0)),
                       pl.BlockSpec((bb, SSD_H, SSD_N, SSD_P), lambda b, t, li: (b, 0, 0, 0))),
            scratch_shapes=[pltpu.VMEM((bb, SSD_H, SSD_N, SSD_P), F32)]),
        compiler_params=_cparams("parallel", "arbitrary"),
        name="scan_ssd",
    )(li, cm, bm, xdt, a, s0)


def _rope(x, pos, rot_dim, theta):
    half = rot_dim // 2
    inv = theta ** (-jnp.arange(half, dtype=F32) / half)
    ang = pos.astype(F32)[:, None] * inv[None, :]
    shape = (1, ang.shape[0]) + (1,) * (x.ndim - 3) + (half,)
    cos = jnp.cos(ang).reshape(shape)
    sin = jnp.sin(ang).reshape(shape)
    x1, x2, rest = x[..., :half], x[..., half:rot_dim], x[..., rot_dim:]
    return jnp.concatenate([x1 * cos - x2 * sin, x1 * sin + x2 * cos, rest], axis=-1)


def _prep_weights(w_in, gla_w_a2, w_up, w_o, w_cq, w_ck, w_cv, w_co, w_mlp1, w_mlp2):
    offs = {}
    o = 0
    names = ("da_q", "da_k", "da_v", "rt_q", "rt_k", "rt_v", "rt_g", "gl_q", "gl_k", "gl_v", "gl_r",
             "gl_lr", "sd_z", "sd_xbc", "sd_dt", "gate")
    d_model = w_in.shape[1]
    sizes = (512, 512, 512, 256, 256, 512, 512, 256, 256, 512, 512, GL_RANK, SSD_DI, CONV_CH, SSD_H,
             N_BRANCH * d_model)
    for n, s in zip(names, sizes):
        offs[n] = (o, o + s)
        o += s
    depth = w_in.shape[0]
    main = w_in[:, :, offs["da_q"][0]:offs["gl_r"][1]]
    small = jnp.concatenate(
        [w_in[:, :, offs["gl_lr"][0]:offs["gl_lr"][1]], w_in[:, :, offs["sd_dt"][0]:offs["sd_dt"][1]],
         jnp.zeros((depth, d_model, LANES - GL_RANK - SSD_H), w_in.dtype)], axis=-1)
    ssd = w_in[:, :, offs["sd_z"][0]:offs["sd_xbc"][1]]
    gate = w_in[:, :, offs["gate"][0]:offs["gate"][1]]
    a2 = jnp.concatenate(
        [gla_w_a2, jnp.zeros((depth, LANES - GL_RANK, gla_w_a2.shape[2]), gla_w_a2.dtype)], axis=1)
    c = lambda w: w.astype(BF16)
    return dict(main=c(main), small=c(small), ssd=c(ssd), gate=c(gate), a2=c(a2), w_up=c(w_up), w_o=c(w_o),
                w_cq=c(w_cq), w_ck=c(w_ck), w_cv=c(w_cv), w_co=c(w_co), w_mlp1=c(w_mlp1), w_mlp2=c(w_mlp2))


MAIN_COLS_RET = (6, 7, 4, 5)
MAIN_COLS_GLA = (12, 13, 7, 8)


def _ret_tables(pos):
    half = RT_DK // 2
    inv = RET_THETA ** (-jnp.arange(half, dtype=F32) / half)
    ang = pos.astype(F32)[:, None] * inv[None, :]
    cos, sin = jnp.cos(ang), jnp.sin(ang)
    return (jnp.tile(jnp.concatenate([cos, cos], axis=-1), (1, RT_H)),
            jnp.tile(jnp.concatenate([-sin, sin], axis=-1), (1, RT_H)))


def _layer(l, x, xb, pos, tables, W, P, mk, mv, mem_li, past, states, lv, alpha):
    B, L, D = x.shape
    M = B * L
    s_ret, s_gla, s_ssm, conv_buf, st_li = states
    xb2 = xb.reshape(M, D)
    main = _mm(xb2, W["main"][l]).reshape(B, L, -1)
    small = _mm(xb2, W["small"][l])
    ssd = _mm(xb2, W["ssd"][l]).reshape(B, L, -1)
    gate = _mm(xb2, W["gate"][l])
    da_q, da_k, da_v = main[..., 0:512], main[..., 512:1024], main[..., 1024:1536]

    q = _rope(da_q.reshape(B, L, DA_H, 2, DA_DH), pos, DA_ROT, ROPE_THETA)
    k = _rope(da_k.reshape(B, L, DA_H, 2, DA_DH), pos, DA_ROT, ROPE_THETA)
    lam_init = 0.8 - 0.6 * math.exp(-0.3 * l)
    lam = (jnp.exp(jnp.sum((P["da_lq1"][l] * P["da_lk1"][l]).astype(F32)))
           - jnp.exp(jnp.sum((P["da_lq2"][l] * P["da_lk2"][l]).astype(F32))) + lam_init)
    sc = jnp.stack([lam, jnp.asarray(1.0 - lam_init, F32)]).astype(F32)
    k_rows = k.reshape(B, L, DA_H, 2 * DA_DH)
    v_rows = da_v.reshape(B, L, DA_H, DA_DV)
    qs = q.reshape(B, L, 512) * (DA_DH ** -0.5)
    if past is None:
        o_da = _flash_da((qs * math.log2(math.e)).astype(BF16), k.reshape(B, L, 512).astype(BF16),
                         da_v.astype(BF16), sc)
    else:
        cache_k, cache_v, page_table, cache_li = past
        t_real = lv
        q5 = qs.reshape(B, L, DA_H, 2, DA_DH)
        eye_h = jnp.eye(DA_H, dtype=F32)
        eye_c = jnp.eye(2, dtype=F32)
        qblk = jnp.einsum("bthcd,hg,ce->bhcdegt", q5, eye_h, eye_c).reshape(B, 512, 2 * DA_H * L)
        qblk = jnp.pad(qblk, ((0, 0), (0, 0), (0, LANES - 2 * DA_H * L))).astype(BF16)
        del t_real
        o_da = _decode_da(qblk, k.reshape(B, L, 512), da_v, cache_k, cache_v, page_table, cache_li, sc)
    o_da = o_da.astype(BF16)

    log_gamma = jnp.log1p(-jnp.exp2(-5.0 - jnp.arange(RT_H, dtype=F32)))
    la_ret = jnp.repeat(log_gamma, RT_DK).reshape(1, RT_H * RT_DK)
    small3 = small.reshape(B, L, LANES)
    o_rt, s_ret_new = _scan_qk(main, MAIN_COLS_RET, tables, small3, W["a2"][l], la_ret, s_ret, st_li, lv, "ret")

    o_gl, s_gla_new = _scan_qk(main, MAIN_COLS_GLA, tables, small3, W["a2"][l],
                               P["gla_b_a"][l].reshape(1, -1), s_gla, st_li, lv, "gla")

    sd_z = ssd[..., :SSD_DI]
    sd_xbc = ssd[..., SSD_DI:]
    sd_dt = small[:, GL_RANK:GL_RANK + SSD_H].reshape(B, L, SSD_H)
    xpad = jnp.concatenate([conv_buf, sd_xbc], axis=1)
    cw = P["ssd_conv_w"][l]
    xbc = jax.nn.silu(sum(xpad[:, j:j + L] * cw[j] for j in range(SSD_CONV)) + P["ssd_conv_b"][l])
    xs, bmat, cmat = jnp.split(xbc, [SSD_DI, SSD_DI + SSD_G * SSD_N], axis=-1)
    dt = jax.nn.softplus(sd_dt + P["ssd_dt_bias"][l])
    a_log = -jnp.exp(P["ssd_a_log"][l].astype(F32))
    xh = xs.reshape(B, L, SSD_H, SSD_P)
    xdt = (xh * dt[..., None]).reshape(B, L, SSD_DI)
    y, s_ssm_new = _scan_ssd(cmat, bmat, xdt, dt * a_log, s_ssm, st_li, lv)
    y = y + (P["ssd_d"][l][:, None] * xh).reshape(B, L, SSD_DI)
    gsz = SSD_DI // SSD_G
    gn = (y * jax.nn.silu(sd_z)).reshape(B, L, SSD_G, gsz)
    gn = gn * lax.rsqrt(jnp.mean(gn * gn, axis=-1, keepdims=True) + EPS)
    o_sd = (gn.reshape(B, L, SSD_DI) * P["ssd_norm_w"][l]).astype(BF16)

    branches = [o.reshape(M, BR_W) for o in (o_da, o_rt, o_gl, o_sd)]
    x2d = x.reshape(M, D)
    x1, x1b = _merge_ln(branches, gate, W["w_up"][l], W["w_o"][l], x2d, P["ln1_g"][l], P["ln1_b"][l], alpha)

    cq = _mm(x1b, W["w_cq"][l]).reshape(B, L, -1)
    oc = _xattn(cq, mk, mv, mem_li).reshape(M, -1)
    x2, x2b = _mm_res_ln(oc, W["w_co"][l], x1, P["ln2_g"][l], P["ln2_b"][l], alpha)

    hmid = _mm(x2b, W["w_mlp1"][l], act="relu2", out_dtype=BF16)
    x3, x3b = _mm_res_ln(hmid, W["w_mlp2"][l], x2, P["ln3_g"][l], P["ln3_b"][l], alpha)
    return (x3.reshape(B, L, D), x3b.reshape(B, L, D),
            (k_rows, v_rows, s_ret_new, s_gla_new, s_ssm_new, xpad))


def kernel(x_prompt, x_sample, mem_prompt, cache_diff_k, cache_diff_v, page_table, state_ret, state_gla, state_ssm, state_conv, cache_mem_k, cache_mem_v, w_in, da_lq1, da_lk1, da_lq2, da_lk2, gla_w_a2, gla_b_a, ssd_conv_w, ssd_conv_b, ssd_dt_bias, ssd_a_log, ssd_d, ssd_norm_w, w_up, w_o, ln1_g, ln1_b, w_cq, w_ck, w_cv, w_co, ln2_g, ln2_b, w_mlp1, w_mlp2, ln3_g, ln3_b):
    depth = w_in.shape[0]
    alpha = (2 * depth) ** 0.25
    P = dict(da_lq1=da_lq1, da_lk1=da_lk1, da_lq2=da_lq2, da_lk2=da_lk2, gla_b_a=gla_b_a,
             ssd_conv_w=ssd_conv_w, ssd_conv_b=ssd_conv_b, ssd_dt_bias=ssd_dt_bias, ssd_a_log=ssd_a_log,
             ssd_d=ssd_d, ssd_norm_w=ssd_norm_w, ln1_g=ln1_g, ln1_b=ln1_b, ln2_g=ln2_g, ln2_b=ln2_b,
             ln3_g=ln3_g, ln3_b=ln3_b)
    W = _prep_weights(w_in, gla_w_a2, w_up, w_o, w_cq, w_ck, w_cv, w_co, w_mlp1, w_mlp2)
    zero_li = jnp.zeros((1,), jnp.int32)

    B, L, D = x_prompt.shape
    pos_p = jnp.arange(L, dtype=jnp.int32)
    tab_p = _ret_tables(pos_p)
    memb =mem_prompt.reshape(-1, D).astype(BF16)
    n_mem = mem_prompt.shape[1]
    z_ret = jnp.zeros((1, B, 2, 2 * RT_DK, RT_DV), F32)
    z_ssm = jnp.zeros((1, B, SSD_H, SSD_N, SSD_P), F32)
    z_conv = jnp.zeros((B, SSD_CONV - 1, CONV_CH), F32)
    x, xb = x_prompt, x_prompt.astype(BF16)
    outs_p = [[] for _ in range(8)]
    for l in range(depth):
        mk = _mm(memb, W["w_ck"][l]).reshape(1, B, n_mem, MA_H, MA_DH)
        mv = _mm(memb, W["w_cv"][l]).reshape(1, B, n_mem, MA_H, MA_DH)
        x, xb, (kr, vr, sr, sg, ss, xpad) = _layer(
            l, x, xb, pos_p, tab_p, W, P, mk, mv, zero_li, None, (z_ret, z_ret, z_ssm, z_conv, zero_li), None,
            alpha)
        new = (kr, vr, sr.reshape(B, RT_H, RT_DK, RT_DV), sg.reshape(B, GL_H, GL_DK, GL_DV), ss, xpad[:, L:],
               mk[0], mv[0])
        for lst, val in zip(outs_p, new):
            lst.append(val)
    y_prompt = x

    Bd, Ld, _ = x_sample.shape
    t_pad = -(-Ld // SUBLANES) * SUBLANES
    n_pages, page = page_table.shape[1], cache_diff_k.shape[2]
    past_len = n_pages * page
    pos_s = past_len + jnp.arange(t_pad, dtype=jnp.int32)
    tab_s = _ret_tables(pos_s)
    x = jnp.pad(x_sample, ((0, 0), (0, t_pad - Ld), (0, 0)))
    xb = x.astype(BF16)
    ck, cv, cmk, cmv = cache_diff_k, cache_diff_v, cache_mem_k, cache_mem_v
    s_ret_all = state_ret.reshape(depth, Bd, 2, 2 * RT_DK, RT_DV)
    s_gla_all = state_gla.reshape(depth, Bd, 2, 2 * GL_DK, GL_DV)
    outs_s = [[] for _ in range(6)]
    for l in range(depth):
        li = jnp.full((1,), l, jnp.int32)
        x, xb, (kr, vr, sr, sg, ss, xpad) = _layer(
            l, x, xb, pos_s, tab_s, W, P, cmk, cmv, li, (ck, cv, page_table, li),
            (s_ret_all, s_gla_all, state_ssm, state_conv[l], li), Ld, alpha)
        nc = SSD_CONV - 1
        new = (kr[:, :Ld], vr[:, :Ld], sr.reshape(Bd, RT_H, RT_DK, RT_DV), sg.reshape(Bd, GL_H, GL_DK, GL_DV),
               ss, xpad[:, Ld:Ld + nc])
        for lst, val in zip(outs_s, new):
            lst.append(val)
    y_sample = x[:, :Ld]

    st = lambda lst: jnp.stack(lst)
    return (y_prompt, y_sample) + tuple(st(o) for o in outs_p) + tuple(st(o) for o in outs_s)
```

```python
import functools
import math

import jax
import jax.numpy as jnp
from jax import lax
from jax.experimental import pallas as pl
from jax.experimental.pallas import tpu as pltpu

F32 = jnp.float32
BF16 = jnp.bfloat16

DA_H, DA_DH, DA_DV = 4, 64, 128
DA_ROT = DA_DH // 4
ROPE_THETA = 500000.0
RT_H, RT_DK, RT_DV = 4, 64, 128
RET_THETA = 10000.0
GL_H, GL_DK, GL_DV, GL_RANK = 4, 64, 128, 16
GLA_TEMP = 16.0
SSD_H, SSD_P, SSD_N, SSD_G, SSD_CONV = 8, 64, 128, 2, 4
SSD_DI = SSD_H * SSD_P
CONV_CH = SSD_DI + 2 * SSD_G * SSD_N
MA_H, MA_DH = 4, 128
N_BRANCH, BR_W = 4, 512
EPS = 1e-5
NEG = -1e30

VMEM_LIMIT_BYTES = 56 * 1024 * 1024
LANES = 128
SUBLANES = 8

_NT = (((1,), (1,)), ((), ()))
_TN = (((0,), (0,)), ((), ()))


def _cparams(*sem):
    return pltpu.CompilerParams(dimension_semantics=sem, vmem_limit_bytes=VMEM_LIMIT_BYTES)


def _pick(n, cands):
    for c in cands:
        if n % c == 0:
            return c
    return n


def _split3(x):
    h = x.astype(BF16)
    r = x - h.astype(F32)
    m = r.astype(BF16)
    lo = (r - m.astype(F32)).astype(BF16)
    return h, m, lo


def _dot(a, b, dims=None):
    if dims is None:
        return jnp.dot(a, b, preferred_element_type=F32)
    return lax.dot_general(a, b, dims, preferred_element_type=F32)


def _mm_kernel(x_ref, w_ref, o_ref, *, act):
    acc = _dot(x_ref[...], w_ref[...])
    if act == "relu2":
        acc = jnp.square(jnp.maximum(acc, 0.0))
    o_ref[...] = acc.astype(o_ref.dtype)


def _mm(x, w, act=None, out_dtype=F32):
    M, K = x.shape
    N = w.shape[1]
    tm = _pick(M, (1024, 512, 256, 128, 64, 32, 16, 8))
    tn = _pick(N, (1024, 768, 512, 384, 256, 128))
    return pl.pallas_call(
        functools.partial(_mm_kernel, act=act),
        out_shape=jax.ShapeDtypeStruct((M, N), out_dtype),
        grid=(N // tn, M // tm),
        in_specs=[pl.BlockSpec((tm, K), lambda j, i: (i, 0)),
                  pl.BlockSpec((K, tn), lambda j, i: (0, j))],
        out_specs=pl.BlockSpec((tm, tn), lambda j, i: (i, j)),
        compiler_params=_cparams("parallel", "parallel"),
        name="mm",
    )(x, w)


def _ln_rows(z, g, b):
    mu = jnp.mean(z, axis=-1, keepdims=True)
    zc = z - mu
    var = jnp.mean(zc * zc, axis=-1, keepdims=True)
    return zc * lax.rsqrt(var + EPS) * g + b


def _mm_res_ln_kernel(x_ref, w_ref, r_ref, g_ref, b_ref, o_ref, ob_ref, *, alpha):
    y = _dot(x_ref[...], w_ref[...])
    out = _ln_rows(alpha * r_ref[...] + y, g_ref[...], b_ref[...])
    o_ref[...] = out
    ob_ref[...] = out.astype(BF16)


def _mm_res_ln(x, w, res, g, b, alpha):
    M, K = x.shape
    N = w.shape[1]
    tm = _pick(M, (512, 256, 128, 64, 32, 16, 8))
    return pl.pallas_call(
        functools.partial(_mm_res_ln_kernel, alpha=alpha),
        out_shape=(jax.ShapeDtypeStruct((M, N), F32), jax.ShapeDtypeStruct((M, N), BF16)),
        grid=(M // tm,),
        in_specs=[pl.BlockSpec((tm, K), lambda i: (i, 0)),
                  pl.BlockSpec((K, N), lambda i: (0, 0)),
                  pl.BlockSpec((tm, N), lambda i: (i, 0)),
                  pl.BlockSpec((1, N), lambda i: (0, 0)),
                  pl.BlockSpec((1, N), lambda i: (0, 0))],
        out_specs=(pl.BlockSpec((tm, N), lambda i: (i, 0)), pl.BlockSpec((tm, N), lambda i: (i, 0))),
        compiler_params=_cparams("parallel"),
        name="mm_res_ln",
    )(x, w, res, g.reshape(1, N), b.reshape(1, N))


def _merge_ln_kernel(b0_ref, b1_ref, b2_ref, b3_ref, gate_ref, wup_ref, wo_ref, r_ref, g_ref, b_ref,
                     o_ref, ob_ref, *, alpha):
    d = wo_ref.shape[0]
    acc = None
    for n, br_ref in enumerate((b0_ref, b1_ref, b2_ref, b3_ref)):
        up = _dot(br_ref[...], wup_ref[n])
        term = jax.nn.sigmoid(gate_ref[:, n * d:(n + 1) * d]) * up
        acc = term if acc is None else acc + term
    y = _dot(acc.astype(BF16), wo_ref[...])
    out = _ln_rows(alpha * r_ref[...] + y, g_ref[...], b_ref[...])
    o_ref[...] = out
    ob_ref[...] = out.astype(BF16)


def _merge_ln(branches, gate, w_up, w_o, res, g, b, alpha):
    M = gate.shape[0]
    D = w_o.shape[0]
    tm = _pick(M, (256, 128, 64, 32, 16, 8))
    br_spec = pl.BlockSpec((tm, BR_W), lambda i: (i, 0))
    return pl.pallas_call(
        functools.partial(_merge_ln_kernel, alpha=alpha),
        out_shape=(jax.ShapeDtypeStruct((M, D), F32), jax.ShapeDtypeStruct((M, D), BF16)),
        grid=(M // tm,),
        in_specs=[br_spec, br_spec, br_spec, br_spec,
                  pl.BlockSpec((tm, N_BRANCH * D), lambda i: (i, 0)),
                  pl.BlockSpec((N_BRANCH, BR_W, D), lambda i: (0, 0, 0)),
                  pl.BlockSpec((D, D), lambda i: (0, 0)),
                  pl.BlockSpec((tm, D), lambda i: (i, 0)),
                  pl.BlockSpec((1, D), lambda i: (0, 0)),
                  pl.BlockSpec((1, D), lambda i: (0, 0))],
        out_specs=(pl.BlockSpec((tm, D), lambda i: (i, 0)), pl.BlockSpec((tm, D), lambda i: (i, 0))),
        compiler_params=_cparams("parallel"),
        name="merge_ln",
    )(*branches, gate, w_up, w_o, res, g.reshape(1, D), b.reshape(1, D))


def _xattn_kernel(li_ref, q_ref, k_ref, v_ref, o_ref):
    del li_ref
    scale = MA_DH ** -0.5
    for h in range(MA_H):
        sl = slice(h * MA_DH, (h + 1) * MA_DH)
        q = q_ref[0, :, sl].astype(BF16)
        k = k_ref[0, 0, :, h, :].astype(BF16)
        v = v_ref[0, 0, :, h, :].astype(BF16)
        s = _dot(q, k, _NT) * scale
        m = jnp.max(s, axis=-1, keepdims=True)
        p = jnp.exp(s - m)
        a = p / jnp.sum(p, axis=-1, keepdims=True)
        o_ref[0, :, sl] = _dot(a.astype(BF16), v).astype(o_ref.dtype)


def _xattn(q, mk, mv, li):
    B, L, W = q.shape
    nm = mk.shape[2]
    tq = _pick(L, (512, 256, 128, 64, 32, 16, 8))
    kv_spec = pl.BlockSpec((1, 1, nm, MA_H, MA_DH), lambda b, i, li: (li[0], b, 0, 0, 0))
    return pl.pallas_call(
        _xattn_kernel,
        out_shape=jax.ShapeDtypeStruct((B, L, W), BF16),
        grid_spec=pltpu.PrefetchScalarGridSpec(
            num_scalar_prefetch=1,
            grid=(B, L // tq),
            in_specs=[pl.BlockSpec((1, tq, W), lambda b, i, li: (b, i, 0)), kv_spec, kv_spec],
            out_specs=pl.BlockSpec((1, tq, W), lambda b, i, li: (b, i, 0))),
        compiler_params=_cparams("parallel", "parallel"),
        name="xattn",
    )(li, q, mk, mv)


def _xattn_dec_kernel(li_ref, q_ref, k_ref, v_ref, o_ref, *, t_pad):
    del li_ref
    qt = q_ref[0]
    row_h = lax.broadcasted_iota(jnp.int32, (qt.shape[0], 1), 0) // t_pad
    lane_h = lax.broadcasted_iota(jnp.int32, (1, k_ref.shape[2]), 1) % MA_H
    s = _dot(qt, k_ref[0, 0].astype(BF16), _NT) * (MA_DH ** -0.5)
    s = jnp.where(lane_h == row_h, s, NEG)
    m = jnp.max(s, axis=-1, keepdims=True)
    p = jnp.exp(s - m)
    a = p / jnp.sum(p, axis=-1, keepdims=True)
    o = _dot(a.astype(BF16), v_ref[0, 0].astype(BF16))
    for h in range(MA_H):
        o_ref[0, :, h * MA_DH:(h + 1) * MA_DH] = o[h * t_pad:(h + 1) * t_pad]


def _xattn_dec(qt, mk, mv, li):
    B, nrow, dh = qt.shape
    t_pad = nrow // MA_H
    nk = mk.shape[2]
    kv_spec = pl.BlockSpec((1, 1, nk, dh), lambda b, li: (li[0], b, 0, 0))
    return pl.pallas_call(
        functools.partial(_xattn_dec_kernel, t_pad=t_pad),
        out_shape=jax.ShapeDtypeStruct((B, t_pad, MA_H * dh), F32),
        grid_spec=pltpu.PrefetchScalarGridSpec(
            num_scalar_prefetch=1,
            grid=(B,),
            in_specs=[pl.BlockSpec((1, nrow, dh), lambda b, li: (b, 0, 0)), kv_spec, kv_spec],
            out_specs=pl.BlockSpec((1, t_pad, MA_H * dh), lambda b, li: (b, 0, 0))),
        compiler_params=_cparams("parallel"),
        name="xattn_dec",
    )(li, qt, mk, mv)


def _head_rms(o):
    return o * lax.rsqrt(jnp.mean(o * o, axis=-1, keepdims=True) + EPS)


def _flash_da_kernel(sc_ref, q_ref, k_ref, vt_ref, o_ref, *, tq, tk):
    i = pl.program_id(2)
    lam = sc_ref[0]
    out_scale = sc_ref[1]
    q = q_ref[0]
    lo = lax.broadcasted_iota(jnp.int32, (1, LANES), 1) < DA_DH
    zero = jnp.zeros_like(q)
    q12 = jnp.concatenate([jnp.where(lo, q, zero), jnp.where(lo, zero, q)], axis=0)
    col = lax.broadcasted_iota(jnp.int32, (1, 2 * tq), 1)
    qpos = i * tq + jnp.where(col >= tq, col - tq, col)
    last = (i * tq) // tk

    def qk(j):
        start = pl.multiple_of(j * tk, tk)
        return _dot(k_ref[0, pl.ds(start, tk), :], q12, _NT)

    def update(st, j, m, l, acc):
        start = pl.multiple_of(j * tk, tk)
        vt = vt_ref[0, :, pl.ds(start, tk)]
        m_new = jnp.maximum(m, jnp.max(st, axis=0, keepdims=True))
        alpha = jnp.exp2(m - m_new)
        p = jnp.exp2(st - m_new)
        l = alpha * l + jnp.sum(p, axis=0, keepdims=True)
        acc = alpha * acc + _dot(vt, p.astype(BF16))
        return m_new, l, acc

    def body(j, carry):
        st, m, l, acc = carry
        st_next = qk(j + 1)
        m, l, acc = update(st, j, m, l, acc)
        return st_next, m, l, acc

    init = (qk(0), jnp.full((1, 2 * tq), NEG, F32), jnp.zeros((1, 2 * tq), F32),
            jnp.zeros((DA_DV, 2 * tq), F32))
    st, m, l, acc = lax.fori_loop(0, last, body, init)
    kpos = last * tk + lax.broadcasted_iota(jnp.int32, (tk, 1), 0)
    st = jnp.where(kpos <= qpos, st, NEG)
    _, l, acc = update(st, last, m, l, acc)
    o12 = acc / l
    o = (o12[:, :tq] - lam * o12[:, tq:]).T
    o_ref[0] = (_head_rms(o) * out_scale).astype(o_ref.dtype)


def _flash_da(q, k, vt, sc):
    B, L, W = q.shape
    tq = _pick(L, (256, 128, 64, 32, 16, 8))
    tk = _pick(L, (512, 256, 128, 64, 32, 16, 8))
    return pl.pallas_call(
        functools.partial(_flash_da_kernel, tq=tq, tk=tk),
        out_shape=jax.ShapeDtypeStruct((B, L, W), BF16),
        grid=(B, DA_H, L // tq),
        in_specs=[pl.BlockSpec(memory_space=pltpu.SMEM),
                  pl.BlockSpec((1, tq, LANES), lambda b, h, i: (b, i, h)),
                  pl.BlockSpec((1, L, LANES), lambda b, h, i: (b, 0, h)),
                  pl.BlockSpec((1, DA_DV, L), lambda b, h, i: (b, h, 0))],
        out_specs=pl.BlockSpec((1, tq, LANES), lambda b, h, i: (b, i, h)),
        compiler_params=_cparams("parallel", "parallel", "arbitrary"),
        name="flash_da",
    )(sc, q, k, vt)


def _decode_da_kernel(pt_ref, li_ref, sc_ref, qb_ref, kn_ref, vn_ref, *refs, n_pages, t_pad):
    del pt_ref, li_ref
    k_refs = refs[:n_pages]
    v_refs = refs[n_pages:2 * n_pages]
    o_ref = refs[2 * n_pages]
    lam = sc_ref[0]
    out_scale = sc_ref[1]
    qt = qb_ref[0]
    nrow = qt.shape[0]
    row = lax.broadcasted_iota(jnp.int32, (nrow, 1), 0)
    row_h = (row // t_pad) % DA_H
    row_t = row % t_pad

    def lane_ids(n):
        lane = lax.broadcasted_iota(jnp.int32, (1, n), 1)
        return lane // DA_H, lane % DA_H

    _, lane_h = lane_ids(k_refs[0].shape[2])
    valid = lane_h == row_h
    s_pages = [jnp.where(valid, _dot(qt, k_refs[p][0, 0].astype(BF16), _NT), NEG) for p in range(n_pages)]
    new_t, new_h = lane_ids(kn_ref.shape[1])
    valid_new = jnp.logical_and(new_h == row_h, new_t <= row_t)
    s_new = jnp.where(valid_new, _dot(qt, kn_ref[0].astype(BF16), _NT), NEG)

    m = jnp.max(s_new, axis=-1, keepdims=True)
    for s in s_pages:
        m = jnp.maximum(m, jnp.max(s, axis=-1, keepdims=True))

    p_new = jnp.exp(s_new - m)
    den = jnp.sum(p_new, axis=-1, keepdims=True)
    acc = _dot(p_new, vn_ref[0])
    for p in range(n_pages):
        pp = jnp.exp(s_pages[p] - m)
        den = den + jnp.sum(pp, axis=-1, keepdims=True)
        acc = acc + _dot(pp.astype(BF16), v_refs[p][0, 0].astype(BF16))
    o12 = acc / den
    half = DA_H * t_pad
    d = o12[:half] - lam * o12[half:]
    for h in range(DA_H):
        o_ref[0, :, h * DA_DV:(h + 1) * DA_DV] = _head_rms(d[h * t_pad:(h + 1) * t_pad]) * out_scale


def _decode_da(qt, k_new, v_new, cache_k, cache_v, page_table, li, sc):
    bd, nrow, _ = qt.shape
    t_pad = k_new.shape[1] // DA_H
    w = DA_H * DA_DV
    n_pages = page_table.shape[1]
    prow = cache_k.shape[2]

    def page_spec(p):
        return pl.BlockSpec((1, 1, prow, DA_DV), lambda b, pt, li: (li[0], pt[b, p], 0, 0))

    in_specs = [pl.BlockSpec(memory_space=pltpu.SMEM),
                pl.BlockSpec((1, nrow, LANES), lambda b, pt, li: (b, 0, 0)),
                pl.BlockSpec((1, t_pad * DA_H, DA_DV), lambda b, pt, li: (b, 0, 0)),
                pl.BlockSpec((1, t_pad * DA_H, DA_DV), lambda b, pt, li: (b, 0, 0))]
    in_specs += [page_spec(p) for p in range(n_pages)] * 2
    return pl.pallas_call(
        functools.partial(_decode_da_kernel, n_pages=n_pages, t_pad=t_pad),
        out_shape=jax.ShapeDtypeStruct((bd, t_pad, w), F32),
        grid_spec=pltpu.PrefetchScalarGridSpec(
            num_scalar_prefetch=2,
            grid=(bd,),
            in_specs=in_specs,
            out_specs=pl.BlockSpec((1, t_pad, w), lambda b, pt, li: (b, 0, 0))),
        compiler_params=_cparams("parallel"),
        name="decode_da",
    )(page_table, li, sc, qt, k_new, v_new, *([cache_k] * n_pages), *([cache_v] * n_pages))


def _cumsum_rows(x, tri):
    h, m, lo = _split3(x)
    return _dot(tri, h) + _dot(tri, m) + _dot(tri, lo)


def _scan_qk_kernel(li_ref, q_ref, k_ref, v_ref, g_ref, cos_ref, sin_ref, lr_ref, a2_ref, ba_ref, s0_ref,
                    o_ref, sn_ref, s_scr, *, c, nch, bb, lv, mode):
    del li_ref
    t = pl.program_id(1)

    @pl.when(t == 0)
    def _():
        s_scr[...] = s0_ref[0]

    mxu = BF16 if c >= 16 else F32
    row = lax.broadcasted_iota(jnp.int32, (c, 1), 0)
    colc = lax.broadcasted_iota(jnp.int32, (1, c), 1)
    causal = row >= colc
    tri = jnp.where(causal, 1.0, 0.0).astype(BF16)
    lo = lax.broadcasted_iota(jnp.int32, (1, LANES), 1) < RT_DK
    wqk = RT_H * RT_DK
    half = RT_DK // 2
    first_half = lax.broadcasted_iota(jnp.int32, (1, wqk), 1) % RT_DK < half
    mid = c // 2 - 1

    def rotary(x, cos, sin):
        partner = jnp.where(first_half, pltpu.roll(x, wqk - half, 1), pltpu.roll(x, half, 1))
        return x * cos + partner * sin

    def chunk(bi, ci):
        rows = slice(ci * c, (ci + 1) * c)
        q = q_ref[bi, rows, :]
        k = k_ref[bi, rows, :]
        v = v_ref[bi, rows, :]
        if mode == "ret":
            cos = cos_ref[rows, :]
            sin = sin_ref[rows, :]
            q = rotary(q, cos, sin)
            k = rotary(k, cos, sin)
            la = jnp.broadcast_to(ba_ref[...], (c, wqk))
        else:
            x = _dot(lr_ref[bi, rows, :].astype(BF16), a2_ref[...]) + ba_ref[...]
            la = jax.nn.log_sigmoid(x) * (1.0 / GLA_TEMP)
        k = k * (RT_DK ** -0.5)
        if lv < c:
            valid = row < lv
            la = jnp.where(valid, la, 0.0)
            k = jnp.where(valid, k, 0.0)
        b = _cumsum_rows(la, tri)
        bm = b[mid:mid + 1, :]
        bl = b[c - 1:c, :]
        qs = q * jnp.exp(b - bm)
        ks = k * jnp.exp(bm - b)
        kh = k * jnp.exp(bl - b)
        qe = q * jnp.exp(b)
        g = g_ref[bi, rows, :]
        for p in range(2):
            sl = slice(p * LANES, (p + 1) * LANES)
            S = s_scr[bi, p]
            Sm = S.astype(mxu)
            ksp = ks[:, sl].astype(mxu)
            qsp = qs[:, sl]
            qep = qe[:, sl]
            for hh in range(2):
                h = 2 * p + hh
                hs = slice(h * RT_DV, (h + 1) * RT_DV)
                msk = lo if hh == 0 else jnp.logical_not(lo)
                s = _dot(jnp.where(msk, qsp, 0.0).astype(mxu), ksp, _NT)
                s = jnp.where(causal, s, 0.0)
                o = _dot(s.astype(mxu), v[:, hs].astype(mxu)) + _dot(jnp.where(msk, qep, 0.0).astype(mxu), Sm)
                gh = g[:, hs]
                o_ref[bi, rows, hs] = (_head_rms(o) * (gh * jax.nn.sigmoid(gh))).astype(o_ref.dtype)
            upd = _dot(kh[:, sl].astype(mxu), v[:, 2 * p * RT_DV:(2 * p + 2) * RT_DV].astype(mxu), _TN)
            dcol = jnp.exp(b[:, sl].T[:, c - 1:c])
            s_scr[bi, p] = dcol * S + jnp.concatenate(
                [upd[0:RT_DK, 0:RT_DV], upd[RT_DK:2 * RT_DK, RT_DV:2 * RT_DV]], axis=0)

    for bi in range(bb):
        for ci in range(nch):
            chunk(bi, ci)

    @pl.when(t == pl.num_programs(1) - 1)
    def _():
        sn_ref[...] = s_scr[...]


def _scan_qk(main, cols, tables, lr, a2, ba, s0, li, lv, mode):
    B, L, _ = main.shape
    c = _pick(L, (128, 64, 32, 16, 8))
    tl = _pick(L, (512, 256, 128, 64, 32, 16, 8))
    bb = _pick(B, (2, 1)) if L > SUBLANES else _pick(B, (8, 4, 2, 1))
    wqk = RT_H * RT_DK
    wv = RT_H * RT_DV
    if lv is None:
        lv = c
    qi, ki, vi, gi = cols
    cos, sin = tables

    def col_spec(width, idx):
        return pl.BlockSpec((bb, tl, width), lambda b, t, li: (b, t, idx))

    tab_spec = pl.BlockSpec((tl, wqk), lambda b, t, li: (t, 0))
    return pl.pallas_call(
        functools.partial(_scan_qk_kernel, c=c, nch=tl // c, bb=bb, lv=lv, mode=mode),
        out_shape=(jax.ShapeDtypeStruct((B, L, wv), BF16),
                   jax.ShapeDtypeStruct((B, 2, 2 * RT_DK, RT_DV), F32)),
        grid_spec=pltpu.PrefetchScalarGridSpec(
            num_scalar_prefetch=1,
            grid=(B // bb, L // tl),
            in_specs=[col_spec(wqk, qi), col_spec(wqk, ki), col_spec(wv, vi), col_spec(wv, gi),
                      tab_spec, tab_spec,
                      pl.BlockSpec((bb, tl, LANES), lambda b, t, li: (b, t, 0)),
                      pl.BlockSpec((LANES, wqk), lambda b, t, li: (0, 0)),
                      pl.BlockSpec((1, wqk), lambda b, t, li: (0, 0)),
                      pl.BlockSpec((1, bb, 2, 2 * RT_DK, RT_DV), lambda b, t, li: (li[0], b, 0, 0, 0))],
            out_specs=(pl.BlockSpec((bb, tl, wv), lambda b, t, li: (b, t, 0)),
                       pl.BlockSpec((bb, 2, 2 * RT_DK, RT_DV), lambda b, t, li: (b, 0, 0, 0))),
            scratch_shapes=[pltpu.VMEM((bb, 2, 2 * RT_DK, RT_DV), F32)]),
        compiler_params=_cparams("parallel", "arbitrary"),
        name="scan_qk_" + mode,
    )(li, main, main, main, main, cos, sin, lr, a2, ba, s0)


def _scan_ssd_kernel(li_ref, c_ref, b_ref, x_ref, a_ref, s0_ref, y_ref, sn_ref, s_scr, *, c, nch, bb, lv):
    del li_ref
    t = pl.program_id(1)

    @pl.when(t == 0)
    def _():
        s_scr[...] = s0_ref[0]

    mxu = BF16 if c >= 16 else F32
    row = lax.broadcasted_iota(jnp.int32, (c, 1), 0)
    colc = lax.broadcasted_iota(jnp.int32, (1, c), 1)
    causal = row >= colc
    tri = jnp.where(causal, 1.0, 0.0).astype(BF16)
    tri_t = jnp.where(row <= colc, 1.0, 0.0).astype(BF16)
    hpg = SSD_H // SSD_G

    def chunk(bi, ci):
        rows = slice(ci * c, (ci + 1) * c)
        cm = c_ref[bi, rows, :]
        bm = b_ref[bi, rows, :]
        x = x_ref[bi, rows, :]
        a = a_ref[bi, rows, :]
        if lv < c:
            valid = row < lv
            a = jnp.where(valid, a, 0.0)
            bm = jnp.where(valid, bm, 0.0)
        ah, am, al = _split3(a)
        bcol = _dot(tri, ah) + _dot(tri, am) + _dot(tri, al)
        brow = _dot(ah, tri_t, _TN) + _dot(am, tri_t, _TN) + _dot(al, tri_t, _TN)
        for g in range(SSD_G):
            cg = cm[:, g * SSD_N:(g + 1) * SSD_N]
            bg = bm[:, g * SSD_N:(g + 1) * SSD_N]
            cgm = cg.astype(mxu)
            gmat = _dot(cgm, bg.astype(mxu), _NT)
            for hh in range(hpg):
                h = g * hpg + hh
                bc = bcol[:, h:h + 1]
                br = brow[h:h + 1, :]
                w = jnp.exp(jnp.where(causal, bc - br, NEG))
                xh = x[:, h * SSD_P:(h + 1) * SSD_P].astype(mxu)
                S = s_scr[bi, h]
                y = _dot((gmat * w).astype(mxu), xh) + jnp.exp(bc) * _dot(cgm, S.astype(mxu))
                y_ref[bi, rows, h * SSD_P:(h + 1) * SSD_P] = y
                bl = bc[c - 1:c, :]
                upd = _dot((bg * jnp.exp(bl - bc)).astype(mxu), xh, _TN)
                s_scr[bi, h] = jnp.exp(bl) * S + upd

    for bi in range(bb):
        for ci in range(nch):
            chunk(bi, ci)

    @pl.when(t == pl.num_programs(1) - 1)
    def _():
        sn_ref[...] = s_scr[...]


def _scan_ssd(cm, bm, xdt, a, s0, li, lv):
    B, L, _ = cm.shape
    c = _pick(L, (128, 64, 32, 16, 8))
    tl = _pick(L, (512, 256, 128, 64, 32, 16, 8))
    bb = _pick(B, (2, 1)) if L > SUBLANES else _pick(B, (8, 4, 2, 1))
    wg = SSD_G * SSD_N
    if lv is None:
        lv = c
    return pl.pallas_call(
        functools.partial(_scan_ssd_kernel, c=c, nch=tl // c, bb=bb, lv=lv),
        out_shape=(jax.ShapeDtypeStruct((B, L, SSD_DI), F32),
                   jax.ShapeDtypeStruct((B, SSD_H, SSD_N, SSD_P), F32)),
        grid_spec=pltpu.PrefetchScalarGridSpec(
            num_scalar_prefetch=1,
            grid=(B // bb, L // tl),
            in_specs=[pl.BlockSpec((bb, tl, wg), lambda b, t, li: (b, t, 0)),
                      pl.BlockSpec((bb, tl, wg), lambda b, t, li: (b, t, 0)),
                      pl.BlockSpec((bb, tl, SSD_DI), lambda b, t, li: (b, t, 0)),
                      pl.BlockSpec((bb, tl, SSD_H), lambda b, t, li: (b, t, 0)),
                      pl.BlockSpec((1, bb, SSD_H, SSD_N, SSD_P), lambda b, t, li: (li[0], b, 0, 0, 0))],
            out_specs=(pl.BlockSpec((bb, tl, SSD_DI), lambda b, t, li: (b, t, 0)),
                       pl.BlockSpec((bb, SSD_H, SSD_N, SSD_P), lambda b, t, li: (b, 0, 0, 0))),
            scratch_shapes=[pltpu.VMEM((bb, SSD_H, SSD_N, SSD_P), F32)]),
        compiler_params=_cparams("parallel", "arbitrary"),
        name="scan_ssd",
    )(li, cm, bm, xdt, a, s0)


def _rope(x, pos, rot_dim, theta):
    half = rot_dim // 2
    inv = theta ** (-jnp.arange(half, dtype=F32) / half)
    ang = pos.astype(F32)[:, None] * inv[None, :]
    shape = (1, ang.shape[0]) + (1,) * (x.ndim - 3) + (half,)
    cos = jnp.cos(ang).reshape(shape)
    sin = jnp.sin(ang).reshape(shape)
    x1, x2, rest = x[..., :half], x[..., half:rot_dim], x[..., rot_dim:]
    return jnp.concatenate([x1 * cos - x2 * sin, x1 * sin + x2 * cos, rest], axis=-1)


def _prep_weights(w_in, gla_w_a2, w_up, w_o, w_cq, w_ck, w_cv, w_co, w_mlp1, w_mlp2):
    offs = {}
    o = 0
    names = ("da_q", "da_k", "da_v", "rt_q", "rt_k", "rt_v", "rt_g", "gl_q", "gl_k", "gl_v", "gl_r",
             "gl_lr", "sd_z", "sd_xbc", "sd_dt", "gate")
    d_model = w_in.shape[1]
    sizes = (512, 512, 512, 256, 256, 512, 512, 256, 256, 512, 512, GL_RANK, SSD_DI, CONV_CH, SSD_H,
             N_BRANCH * d_model)
    for n, s in zip(names, sizes):
        offs[n] = (o, o + s)
        o += s
    depth = w_in.shape[0]
    main = w_in[:, :, offs["da_q"][0]:offs["gl_r"][1]]
    small = jnp.concatenate(
        [w_in[:, :, offs["gl_lr"][0]:offs["gl_lr"][1]], w_in[:, :, offs["sd_dt"][0]:offs["sd_dt"][1]],
         jnp.zeros((depth, d_model, LANES - GL_RANK - SSD_H), w_in.dtype)], axis=-1)
    ssd = w_in[:, :, offs["sd_z"][0]:offs["sd_xbc"][1]]
    gate = w_in[:, :, offs["gate"][0]:offs["gate"][1]]
    a2 = jnp.concatenate(
        [gla_w_a2, jnp.zeros((depth, LANES - GL_RANK, gla_w_a2.shape[2]), gla_w_a2.dtype)], axis=1)
    c = lambda w: w.astype(BF16)
    return dict(main=c(main), small=c(small), ssd=c(ssd), gate=c(gate), a2=c(a2), w_up=c(w_up), w_o=c(w_o),
                w_cq=c(w_cq), w_ck=c(w_ck), w_cv=c(w_cv), w_co=c(w_co), w_mlp1=c(w_mlp1), w_mlp2=c(w_mlp2))


MAIN_COLS_RET = (6, 7, 4, 5)
MAIN_COLS_GLA = (12, 13, 7, 8)


def _ret_tables(pos):
    half = RT_DK // 2
    inv = RET_THETA ** (-jnp.arange(half, dtype=F32) / half)
    ang = pos.astype(F32)[:, None] * inv[None, :]
    cos, sin = jnp.cos(ang), jnp.sin(ang)
    return (jnp.tile(jnp.concatenate([cos, cos], axis=-1), (1, RT_H)),
            jnp.tile(jnp.concatenate([-sin, sin], axis=-1), (1, RT_H)))


def _layer(l, x, xb, pos, tables, W, P, mk, mv, mem_li, past, states, lv, alpha):
    B, L, D = x.shape
    M = B * L
    s_ret, s_gla, s_ssm, conv_buf, st_li = states
    xb2 = xb.reshape(M, D)
    main = _mm(xb2, W["main"][l]).reshape(B, L, -1)
    small = _mm(xb2, W["small"][l])
    ssd = _mm(xb2, W["ssd"][l]).reshape(B, L, -1)
    gate = _mm(xb2, W["gate"][l])
    da_q, da_k, da_v = main[..., 0:512], main[..., 512:1024], main[..., 1024:1536]

    q = _rope(da_q.reshape(B, L, DA_H, 2, DA_DH), pos, DA_ROT, ROPE_THETA)
    k = _rope(da_k.reshape(B, L, DA_H, 2, DA_DH), pos, DA_ROT, ROPE_THETA)
    lam_init = 0.8 - 0.6 * math.exp(-0.3 * l)
    lam = (jnp.exp(jnp.sum((P["da_lq1"][l] * P["da_lk1"][l]).astype(F32)))
           - jnp.exp(jnp.sum((P["da_lq2"][l] * P["da_lk2"][l]).astype(F32))) + lam_init)
    sc = jnp.stack([lam, jnp.asarray(1.0 - lam_init, F32)]).astype(F32)
    k_rows = k.reshape(B, L, DA_H, 2 * DA_DH)
    v_rows = da_v.reshape(B, L, DA_H, DA_DV)
    qs = q.reshape(B, L, 512) * (DA_DH ** -0.5)
    if past is None:
        o_da = _flash_da((qs * math.log2(math.e)).astype(BF16), k.reshape(B, L, 512).astype(BF16),
                         jnp.swapaxes(da_v, 1, 2).astype(BF16), sc)
    else:
        cache_k, cache_v, page_table, cache_li = past
        q5 = qs.reshape(B, L, DA_H, 2, DA_DH)
        qt = jnp.einsum("bthcd,ce->bchted", q5, jnp.eye(2, dtype=F32)).reshape(B, 2 * DA_H * L, 2 * DA_DH)
        o_da = _decode_da(qt.astype(BF16), k.reshape(B, L * DA_H, 2 * DA_DH), da_v.reshape(B, L * DA_H, DA_DV),
                          cache_k, cache_v, page_table, cache_li, sc)
    o_da = o_da.astype(BF16)

    log_gamma = jnp.log1p(-jnp.exp2(-5.0 - jnp.arange(RT_H, dtype=F32)))
    la_ret = jnp.repeat(log_gamma, RT_DK).reshape(1, RT_H * RT_DK)
    small3 = small.reshape(B, L, LANES)
    o_rt, s_ret_new = _scan_qk(main, MAIN_COLS_RET, tables, small3, W["a2"][l], la_ret, s_ret, st_li, lv, "ret")

    o_gl, s_gla_new = _scan_qk(main, MAIN_COLS_GLA, tables, small3, W["a2"][l],
                               P["gla_b_a"][l].reshape(1, -1), s_gla, st_li, lv, "gla")

    sd_z = ssd[..., :SSD_DI]
    sd_xbc = ssd[..., SSD_DI:]
    sd_dt = small[:, GL_RANK:GL_RANK + SSD_H].reshape(B, L, SSD_H)
    xpad = jnp.concatenate([conv_buf, sd_xbc], axis=1)
    cw = P["ssd_conv_w"][l]
    xbc = jax.nn.silu(sum(xpad[:, j:j + L] * cw[j] for j in range(SSD_CONV)) + P["ssd_conv_b"][l])
    xs, bmat, cmat = jnp.split(xbc, [SSD_DI, SSD_DI + SSD_G * SSD_N], axis=-1)
    dt = jax.nn.softplus(sd_dt + P["ssd_dt_bias"][l])
    a_log = -jnp.exp(P["ssd_a_log"][l].astype(F32))
    xh = xs.reshape(B, L, SSD_H, SSD_P)
    xdt = (xh * dt[..., None]).reshape(B, L, SSD_DI)
    y, s_ssm_new = _scan_ssd(cmat, bmat, xdt, dt * a_log, s_ssm, st_li, lv)
    y = y + (P["ssd_d"][l][:, None] * xh).reshape(B, L, SSD_DI)
    gsz = SSD_DI // SSD_G
    gn = (y * jax.nn.silu(sd_z)).reshape(B, L, SSD_G, gsz)
    gn = gn * lax.rsqrt(jnp.mean(gn * gn, axis=-1, keepdims=True) + EPS)
    o_sd = (gn.reshape(B, L, SSD_DI) * P["ssd_norm_w"][l]).astype(BF16)

    branches = [o.reshape(M, BR_W) for o in (o_da, o_rt, o_gl, o_sd)]
    x2d = x.reshape(M, D)
    x1, x1b = _merge_ln(branches, gate, W["w_up"][l], W["w_o"][l], x2d, P["ln1_g"][l], P["ln1_b"][l], alpha)

    cq = _mm(x1b, W["w_cq"][l]).reshape(B, L, -1)
    if past is None:
        oc = _xattn(cq, mk, mv, mem_li).reshape(M, -1)
    else:
        qt = cq.reshape(B, L, MA_H, MA_DH).transpose(0, 2, 1, 3).reshape(B, MA_H * L, MA_DH)
        oc = _xattn_dec(qt.astype(BF16), mk, mv, mem_li).reshape(M, -1).astype(BF16)
    x2, x2b = _mm_res_ln(oc, W["w_co"][l], x1, P["ln2_g"][l], P["ln2_b"][l], alpha)

    hmid = _mm(x2b, W["w_mlp1"][l], act="relu2", out_dtype=BF16)
    x3, x3b = _mm_res_ln(hmid, W["w_mlp2"][l], x2, P["ln3_g"][l], P["ln3_b"][l], alpha)
    return (x3.reshape(B, L, D), x3b.reshape(B, L, D),
            (k_rows, v_rows, s_ret_new, s_gla_new, s_ssm_new, xpad))


def kernel(x_prompt, x_sample, mem_prompt, cache_diff_k, cache_diff_v, page_table, state_ret, state_gla, state_ssm, state_conv, cache_mem_k, cache_mem_v, w_in, da_lq1, da_lk1, da_lq2, da_lk2, gla_w_a2, gla_b_a, ssd_conv_w, ssd_conv_b, ssd_dt_bias, ssd_a_log, ssd_d, ssd_norm_w, w_up, w_o, ln1_g, ln1_b, w_cq, w_ck, w_cv, w_co, ln2_g, ln2_b, w_mlp1, w_mlp2, ln3_g, ln3_b):
    depth = w_in.shape[0]
    alpha = (2 * depth) ** 0.25
    P = dict(da_lq1=da_lq1, da_lk1=da_lk1, da_lq2=da_lq2, da_lk2=da_lk2, gla_b_a=gla_b_a,
             ssd_conv_w=ssd_conv_w, ssd_conv_b=ssd_conv_b, ssd_dt_bias=ssd_dt_bias, ssd_a_log=ssd_a_log,
             ssd_d=ssd_d, ssd_norm_w=ssd_norm_w, ln1_g=ln1_g, ln1_b=ln1_b, ln2_g=ln2_g, ln2_b=ln2_b,
             ln3_g=ln3_g, ln3_b=ln3_b)
    W = _prep_weights(w_in, gla_w_a2, w_up, w_o, w_cq, w_ck, w_cv, w_co, w_mlp1, w_mlp2)
    zero_li = jnp.zeros((1,), jnp.int32)

    B, L, D = x_prompt.shape
    pos_p = jnp.arange(L, dtype=jnp.int32)
    tab_p = _ret_tables(pos_p)
    memb =mem_prompt.reshape(-1, D).astype(BF16)
    n_mem = mem_prompt.shape[1]
    z_ret = jnp.zeros((1, B, 2, 2 * RT_DK, RT_DV), F32)
    z_ssm = jnp.zeros((1, B, SSD_H, SSD_N, SSD_P), F32)
    z_conv = jnp.zeros((B, SSD_CONV - 1, CONV_CH), F32)
    x, xb = x_prompt, x_prompt.astype(BF16)
    outs_p = [[] for _ in range(8)]
    for l in range(depth):
        mk = _mm(memb, W["w_ck"][l]).reshape(1, B, n_mem, MA_H, MA_DH)
        mv = _mm(memb, W["w_cv"][l]).reshape(1, B, n_mem, MA_H, MA_DH)
        x, xb, (kr, vr, sr, sg, ss, xpad) = _layer(
            l, x, xb, pos_p, tab_p, W, P, mk, mv, zero_li, None, (z_ret, z_ret, z_ssm, z_conv, zero_li), None,
            alpha)
        new = (kr, vr, sr.reshape(B, RT_H, RT_DK, RT_DV), sg.reshape(B, GL_H, GL_DK, GL_DV), ss, xpad[:, L:],
               mk[0], mv[0])
        for lst, val in zip(outs_p, new):
            lst.append(val)
    y_prompt = x

    Bd, Ld, _ = x_sample.shape
    t_pad = -(-Ld // SUBLANES) * SUBLANES
    n_pages, page = page_table.shape[1], cache_diff_k.shape[2]
    past_len = n_pages * page
    pos_s = past_len + jnp.arange(t_pad, dtype=jnp.int32)
    tab_s = _ret_tables(pos_s)
    x = jnp.pad(x_sample, ((0, 0), (0, t_pad - Ld), (0, 0)))
    xb = x.astype(BF16)
    n_pool = cache_diff_k.shape[1]
    ck = cache_diff_k.reshape(depth, n_pool, page * DA_H, 2 * DA_DH)
    cv = cache_diff_v.reshape(depth, n_pool, page * DA_H, DA_DV)
    cmk = cache_mem_k.reshape(depth, Bd, -1, MA_DH)
    cmv = cache_mem_v.reshape(depth, Bd, -1, MA_DH)
    s_ret_all = state_ret.reshape(depth, Bd, 2, 2 * RT_DK, RT_DV)
    s_gla_all = state_gla.reshape(depth, Bd, 2, 2 * GL_DK, GL_DV)
    outs_s = [[] for _ in range(6)]
    for l in range(depth):
        li = jnp.full((1,), l, jnp.int32)
        x, xb, (kr, vr, sr, sg, ss, xpad) = _layer(
            l, x, xb, pos_s, tab_s, W, P, cmk, cmv, li, (ck, cv, page_table, li),
            (s_ret_all, s_gla_all, state_ssm, state_conv[l], li), Ld, alpha)
        nc = SSD_CONV - 1
        new = (kr[:, :Ld], vr[:, :Ld], sr.reshape(Bd, RT_H, RT_DK, RT_DV), sg.reshape(Bd, GL_H, GL_DK, GL_DV),
               ss, xpad[:, Ld:Ld + nc])
        for lst, val in zip(outs_s, new):
            lst.append(val)
    y_sample = x[:, :Ld]

    st = lambda lst: jnp.stack(lst)
    return (y_prompt, y_sample) + tuple(st(o) for o in outs_p) + tuple(st(o) for o in outs_s)
```

```python
import functools
import math

import jax
import jax.numpy as jnp
from jax import lax
from jax.experimental import pallas as pl
from jax.experimental.pallas import tpu as pltpu

F32 = jnp.float32
BF16 = jnp.bfloat16

DA_H, DA_DH, DA_DV = 4, 64, 128
DA_ROT = DA_DH // 4
ROPE_THETA = 500000.0
RT_H, RT_DK, RT_DV = 4, 64, 128
RET_THETA = 10000.0
GL_H, GL_DK, GL_DV, GL_RANK = 4, 64, 128, 16
GLA_TEMP = 16.0
SSD_H, SSD_P, SSD_N, SSD_G, SSD_CONV = 8, 64, 128, 2, 4
SSD_DI = SSD_H * SSD_P
CONV_CH = SSD_DI + 2 * SSD_G * SSD_N
MA_H, MA_DH = 4, 128
N_BRANCH, BR_W = 4, 512
EPS = 1e-5
NEG = -1e30

VMEM_LIMIT_BYTES = 56 * 1024 * 1024
LANES = 128
SUBLANES = 8

_NT = (((1,), (1,)), ((), ()))
_TN = (((0,), (0,)), ((), ()))


def _cparams(*sem):
    return pltpu.CompilerParams(dimension_semantics=sem, vmem_limit_bytes=VMEM_LIMIT_BYTES)


def _pick(n, cands):
    for c in cands:
        if n % c == 0:
            return c
    return n


def _split3(x):
    h = x.astype(BF16)
    r = x - h.astype(F32)
    m = r.astype(BF16)
    lo = (r - m.astype(F32)).astype(BF16)
    return h, m, lo


def _dot(a, b, dims=None):
    if dims is None:
        return jnp.dot(a, b, preferred_element_type=F32)
    return lax.dot_general(a, b, dims, preferred_element_type=F32)


def _mm_kernel(x_ref, w_ref, o_ref, *, act):
    acc = _dot(x_ref[...], w_ref[...])
    if act == "relu2":
        acc = jnp.square(jnp.maximum(acc, 0.0))
    o_ref[...] = acc.astype(o_ref.dtype)


def _mm(x, w, act=None, out_dtype=F32):
    M, K = x.shape
    N = w.shape[1]
    tm = _pick(M, (1024, 512, 256, 128, 64, 32, 16, 8))
    tn = _pick(N, (1024, 768, 512, 384, 256, 128))
    return pl.pallas_call(
        functools.partial(_mm_kernel, act=act),
        out_shape=jax.ShapeDtypeStruct((M, N), out_dtype),
        grid=(N // tn, M // tm),
        in_specs=[pl.BlockSpec((tm, K), lambda j, i: (i, 0)),
                  pl.BlockSpec((K, tn), lambda j, i: (0, j))],
        out_specs=pl.BlockSpec((tm, tn), lambda j, i: (i, j)),
        compiler_params=_cparams("parallel", "parallel"),
        name="mm",
    )(x, w)


def _ln_rows(z, g, b):
    mu = jnp.mean(z, axis=-1, keepdims=True)
    zc = z - mu
    var = jnp.mean(zc * zc, axis=-1, keepdims=True)
    return zc * lax.rsqrt(var + EPS) * g + b


def _mm_res_ln_kernel(x_ref, w_ref, r_ref, g_ref, b_ref, o_ref, ob_ref, *, alpha):
    y = _dot(x_ref[...], w_ref[...])
    out = _ln_rows(alpha * r_ref[...] + y, g_ref[...], b_ref[...])
    o_ref[...] = out
    ob_ref[...] = out.astype(BF16)


def _mm_res_ln(x, w, res, g, b, alpha):
    M, K = x.shape
    N = w.shape[1]
    tm = _pick(M, (512, 256, 128, 64, 32, 16, 8))
    return pl.pallas_call(
        functools.partial(_mm_res_ln_kernel, alpha=alpha),
        out_shape=(jax.ShapeDtypeStruct((M, N), F32), jax.ShapeDtypeStruct((M, N), BF16)),
        grid=(M // tm,),
        in_specs=[pl.BlockSpec((tm, K), lambda i: (i, 0)),
                  pl.BlockSpec((K, N), lambda i: (0, 0)),
                  pl.BlockSpec((tm, N), lambda i: (i, 0)),
                  pl.BlockSpec((1, N), lambda i: (0, 0)),
                  pl.BlockSpec((1, N), lambda i: (0, 0))],
        out_specs=(pl.BlockSpec((tm, N), lambda i: (i, 0)), pl.BlockSpec((tm, N), lambda i: (i, 0))),
        compiler_params=_cparams("parallel"),
        name="mm_res_ln",
    )(x, w, res, g.reshape(1, N), b.reshape(1, N))


def _merge_ln_kernel(b0_ref, b1_ref, b2_ref, b3_ref, xb_ref, wg_ref, wup_ref, wo_ref, r_ref, g_ref, b_ref,
                     o_ref, ob_ref, *, alpha):
    d = wo_ref.shape[0]
    xb = xb_ref[...]
    acc = None
    for n, br_ref in enumerate((b0_ref, b1_ref, b2_ref, b3_ref)):
        up = _dot(br_ref[...], wup_ref[n])
        term = jax.nn.sigmoid(_dot(xb, wg_ref[:, n * d:(n + 1) * d])) * up
        acc = term if acc is None else acc + term
    y = _dot(acc.astype(BF16), wo_ref[...])
    out = _ln_rows(alpha * r_ref[...] + y, g_ref[...], b_ref[...])
    o_ref[...] = out
    ob_ref[...] = out.astype(BF16)


def _merge_ln(branches, xb, w_gate, w_up, w_o, res, g, b, alpha):
    M = xb.shape[0]
    D = w_o.shape[0]
    tm = _pick(M, (256, 128, 64, 32, 16, 8))
    br_spec = pl.BlockSpec((tm, BR_W), lambda i: (i, 0))
    return pl.pallas_call(
        functools.partial(_merge_ln_kernel, alpha=alpha),
        out_shape=(jax.ShapeDtypeStruct((M, D), F32), jax.ShapeDtypeStruct((M, D), BF16)),
        grid=(M // tm,),
        in_specs=[br_spec, br_spec, br_spec, br_spec,
                  pl.BlockSpec((tm, D), lambda i: (i, 0)),
                  pl.BlockSpec((D, N_BRANCH * D), lambda i: (0, 0)),
                  pl.BlockSpec((N_BRANCH, BR_W, D), lambda i: (0, 0, 0)),
                  pl.BlockSpec((D, D), lambda i: (0, 0)),
                  pl.BlockSpec((tm, D), lambda i: (i, 0)),
                  pl.BlockSpec((1, D), lambda i: (0, 0)),
                  pl.BlockSpec((1, D), lambda i: (0, 0))],
        out_specs=(pl.BlockSpec((tm, D), lambda i: (i, 0)), pl.BlockSpec((tm, D), lambda i: (i, 0))),
        compiler_params=_cparams("parallel"),
        name="merge_ln",
    )(*branches, xb, w_gate, w_up, w_o, res, g.reshape(1, D), b.reshape(1, D))


def _xattn_kernel(li_ref, q_ref, k_ref, v_ref, o_ref):
    del li_ref
    scale = MA_DH ** -0.5
    for h in range(MA_H):
        sl = slice(h * MA_DH, (h + 1) * MA_DH)
        q = q_ref[0, :, sl].astype(BF16)
        k = k_ref[0, 0, :, h, :].astype(BF16)
        v = v_ref[0, 0, :, h, :].astype(BF16)
        s = _dot(q, k, _NT) * scale
        m = jnp.max(s, axis=-1, keepdims=True)
        p = jnp.exp(s - m)
        a = p / jnp.sum(p, axis=-1, keepdims=True)
        o_ref[0, :, sl] = _dot(a.astype(BF16), v).astype(o_ref.dtype)


def _xattn(q, mk, mv, li):
    B, L, W = q.shape
    nm = mk.shape[2]
    tq = _pick(L, (512, 256, 128, 64, 32, 16, 8))
    kv_spec = pl.BlockSpec((1, 1, nm, MA_H, MA_DH), lambda b, i, li: (li[0], b, 0, 0, 0))
    return pl.pallas_call(
        _xattn_kernel,
        out_shape=jax.ShapeDtypeStruct((B, L, W), BF16),
        grid_spec=pltpu.PrefetchScalarGridSpec(
            num_scalar_prefetch=1,
            grid=(B, L // tq),
            in_specs=[pl.BlockSpec((1, tq, W), lambda b, i, li: (b, i, 0)), kv_spec, kv_spec],
            out_specs=pl.BlockSpec((1, tq, W), lambda b, i, li: (b, i, 0))),
        compiler_params=_cparams("parallel", "parallel"),
        name="xattn",
    )(li, q, mk, mv)


def _xattn_dec_kernel(li_ref, q_ref, k_ref, v_ref, o_ref, *, t_pad):
    del li_ref
    qt = q_ref[0]
    row_h = lax.broadcasted_iota(jnp.int32, (qt.shape[0], 1), 0) // t_pad
    lane_h = lax.broadcasted_iota(jnp.int32, (1, k_ref.shape[2]), 1) % MA_H
    s = _dot(qt, k_ref[0, 0].astype(BF16), _NT) * (MA_DH ** -0.5)
    s = jnp.where(lane_h == row_h, s, NEG)
    m = jnp.max(s, axis=-1, keepdims=True)
    p = jnp.exp(s - m)
    a = p / jnp.sum(p, axis=-1, keepdims=True)
    o = _dot(a.astype(BF16), v_ref[0, 0].astype(BF16))
    for h in range(MA_H):
        o_ref[0, :, h * MA_DH:(h + 1) * MA_DH] = o[h * t_pad:(h + 1) * t_pad]


def _xattn_dec(qt, mk, mv, li):
    B, nrow, dh = qt.shape
    t_pad = nrow // MA_H
    nk = mk.shape[2]
    kv_spec = pl.BlockSpec((1, 1, nk, dh), lambda b, li: (li[0], b, 0, 0))
    return pl.pallas_call(
        functools.partial(_xattn_dec_kernel, t_pad=t_pad),
        out_shape=jax.ShapeDtypeStruct((B, t_pad, MA_H * dh), F32),
        grid_spec=pltpu.PrefetchScalarGridSpec(
            num_scalar_prefetch=1,
            grid=(B,),
            in_specs=[pl.BlockSpec((1, nrow, dh), lambda b, li: (b, 0, 0)), kv_spec, kv_spec],
            out_specs=pl.BlockSpec((1, t_pad, MA_H * dh), lambda b, li: (b, 0, 0))),
        compiler_params=_cparams("parallel"),
        name="xattn_dec",
    )(li, qt, mk, mv)


def _head_rms(o):
    return o * lax.rsqrt(jnp.mean(o * o, axis=-1, keepdims=True) + EPS)


def _flash_da_kernel(sc_ref, q_ref, k_ref, vt_ref, o_ref, st0_scr, st1_scr, p0_scr, p1_scr, acc_scr, m_scr, l_scr,
                     *, tq, tk):
    i = pl.program_id(2)
    lam = sc_ref[0]
    out_scale = sc_ref[1]
    q = q_ref[0]
    lo = lax.broadcasted_iota(jnp.int32, (1, LANES), 1) < DA_DH
    zero = jnp.zeros_like(q)
    q12 = jnp.concatenate([jnp.where(lo, q, zero), jnp.where(lo, zero, q)], axis=0)
    col = lax.broadcasted_iota(jnp.int32, (1, 2 * tq), 1)
    qpos = i * tq + jnp.where(col >= tq, col - tq, col)
    last = (i * tq) // tk

    def qk(j):
        start = pl.multiple_of(j * tk, tk)
        return _dot(k_ref[0, pl.ds(start, tk), :], q12, _NT)

    def softmax_step(st_ref, p_ref, m, l, mask=None):
        def scores():
            st = st_ref[...]
            return st if mask is None else jnp.where(mask, st, NEG)
        m_new = jnp.maximum(m, jnp.max(scores(), axis=0, keepdims=True))
        alpha = jnp.exp2(m - m_new)
        p = jnp.exp2(scores() - m_new)
        p_ref[...] = p.astype(BF16)
        return m_new, alpha * l + jnp.sum(p, axis=0, keepdims=True), alpha

    def accumulate(j, p_ref, alpha):
        start = pl.multiple_of(j * tk, tk)
        acc_scr[...] = alpha * acc_scr[...] + _dot(vt_ref[0, :, pl.ds(start, tk)], p_ref[...])

    odd = last % 2
    m_scr[...] = jnp.full((1, 2 * tq), NEG, F32)
    l_scr[...] = jnp.zeros((1, 2 * tq), F32)
    acc_scr[...] = jnp.zeros((DA_DV, 2 * tq), F32)
    p1_scr[...] = jnp.zeros((tk, 2 * tq), BF16)

    @pl.when(odd == 1)
    def _():
        st0_scr[...] = qk(0)
        m, l, alpha = softmax_step(st0_scr, p0_scr, m_scr[...], l_scr[...])
        accumulate(0, p0_scr, alpha)
        m_scr[...] = m
        l_scr[...] = l

    st0_scr[...] = qk(odd)

    def pair(u, carry):
        a_prev, m, l = carry
        b0 = odd + 2 * u
        st1_scr[...] = qk(b0 + 1)
        accumulate(jnp.maximum(b0 - 1, 0), p1_scr, a_prev)
        m, l, a0 = softmax_step(st0_scr, p0_scr, m, l)
        st0_scr[...] = qk(b0 + 2)
        accumulate(b0, p0_scr, a0)
        m, l, a1 = softmax_step(st1_scr, p1_scr, m, l)
        return a1, m, l

    a_prev, m, l = lax.fori_loop(0, last // 2, pair, (jnp.ones((1, 2 * tq), F32), m_scr[...], l_scr[...]))
    accumulate(jnp.maximum(last - 1, 0), p1_scr, a_prev)
    kpos = last * tk + lax.broadcasted_iota(jnp.int32, (tk, 1), 0)
    _, l, alpha = softmax_step(st0_scr, p0_scr, m, l, mask=kpos <= qpos)
    accumulate(last, p0_scr, alpha)
    o12 = acc_scr[...] / l
    o = (o12[:, :tq] - lam * o12[:, tq:]).T
    o_ref[0] = (_head_rms(o) * out_scale).astype(o_ref.dtype)


def _flash_da(q, k, vt, sc):
    B, L, W = q.shape
    tq = _pick(L, (256, 128, 64, 32, 16, 8))
    tk = _pick(L, (512, 256, 128, 64, 32, 16, 8))
    return pl.pallas_call(
        functools.partial(_flash_da_kernel, tq=tq, tk=tk),
        out_shape=jax.ShapeDtypeStruct((B, L, W), BF16),
        grid=(B, DA_H, L // tq),
        in_specs=[pl.BlockSpec(memory_space=pltpu.SMEM),
                  pl.BlockSpec((1, tq, LANES), lambda b, h, i: (b, i, h)),
                  pl.BlockSpec((1, L, LANES), lambda b, h, i: (b, 0, h)),
                  pl.BlockSpec((1, DA_DV, L), lambda b, h, i: (b, h, 0))],
        out_specs=pl.BlockSpec((1, tq, LANES), lambda b, h, i: (b, i, h)),
        scratch_shapes=[pltpu.VMEM((tk, 2 * tq), F32), pltpu.VMEM((tk, 2 * tq), F32),
                        pltpu.VMEM((tk, 2 * tq), BF16), pltpu.VMEM((tk, 2 * tq), BF16),
                        pltpu.VMEM((DA_DV, 2 * tq), F32), pltpu.VMEM((1, 2 * tq), F32),
                        pltpu.VMEM((1, 2 * tq), F32)],
        compiler_params=_cparams("parallel", "parallel", "arbitrary"),
        name="flash_da",
    )(sc, q, k, vt)


def _decode_da_kernel(pt_ref, li_ref, sc_ref, qb_ref, kn_ref, vn_ref, *refs, n_pages, t_pad):
    del pt_ref, li_ref
    k_refs = refs[:n_pages]
    v_refs = refs[n_pages:2 * n_pages]
    o_ref = refs[2 * n_pages]
    lam = sc_ref[0]
    out_scale = sc_ref[1]
    qt = qb_ref[0]
    nrow = qt.shape[0]
    row = lax.broadcasted_iota(jnp.int32, (nrow, 1), 0)
    row_h = (row // t_pad) % DA_H
    row_t = row % t_pad

    def lane_ids(n):
        lane = lax.broadcasted_iota(jnp.int32, (1, n), 1)
        return lane // DA_H, lane % DA_H

    _, lane_h = lane_ids(k_refs[0].shape[2])
    valid = lane_h == row_h
    s_pages = [jnp.where(valid, _dot(qt, k_refs[p][0, 0].astype(BF16), _NT), NEG) for p in range(n_pages)]
    new_t, new_h = lane_ids(kn_ref.shape[1])
    valid_new = jnp.logical_and(new_h == row_h, new_t <= row_t)
    s_new = jnp.where(valid_new, _dot(qt, kn_ref[0].astype(BF16), _NT), NEG)

    m = jnp.max(s_new, axis=-1, keepdims=True)
    for s in s_pages:
        m = jnp.maximum(m, jnp.max(s, axis=-1, keepdims=True))

    p_new = jnp.exp(s_new - m)
    den = jnp.sum(p_new, axis=-1, keepdims=True)
    acc = _dot(p_new, vn_ref[0])
    for p in range(n_pages):
        pp = jnp.exp(s_pages[p] - m)
        den = den + jnp.sum(pp, axis=-1, keepdims=True)
        acc = acc + _dot(pp.astype(BF16), v_refs[p][0, 0].astype(BF16))
    o12 = acc / den
    half = DA_H * t_pad
    d = o12[:half] - lam * o12[half:]
    for h in range(DA_H):
        o_ref[0, :, h * DA_DV:(h + 1) * DA_DV] = _head_rms(d[h * t_pad:(h + 1) * t_pad]) * out_scale


def _decode_da(qt, k_new, v_new, cache_k, cache_v, page_table, li, sc):
    bd, nrow, _ = qt.shape
    t_pad = k_new.shape[1] // DA_H
    w = DA_H * DA_DV
    n_pages = page_table.shape[1]
    prow = cache_k.shape[2]

    def page_spec(p):
        return pl.BlockSpec((1, 1, prow, DA_DV), lambda b, pt, li: (li[0], pt[b, p], 0, 0))

    in_specs = [pl.BlockSpec(memory_space=pltpu.SMEM),
                pl.BlockSpec((1, nrow, LANES), lambda b, pt, li: (b, 0, 0)),
                pl.BlockSpec((1, t_pad * DA_H, DA_DV), lambda b, pt, li: (b, 0, 0)),
                pl.BlockSpec((1, t_pad * DA_H, DA_DV), lambda b, pt, li: (b, 0, 0))]
    in_specs += [page_spec(p) for p in range(n_pages)] * 2
    return pl.pallas_call(
        functools.partial(_decode_da_kernel, n_pages=n_pages, t_pad=t_pad),
        out_shape=jax.ShapeDtypeStruct((bd, t_pad, w), F32),
        grid_spec=pltpu.PrefetchScalarGridSpec(
            num_scalar_prefetch=2,
            grid=(bd,),
            in_specs=in_specs,
            out_specs=pl.BlockSpec((1, t_pad, w), lambda b, pt, li: (b, 0, 0))),
        compiler_params=_cparams("parallel"),
        name="decode_da",
    )(page_table, li, sc, qt, k_new, v_new, *([cache_k] * n_pages), *([cache_v] * n_pages))


def _cumsum_rows(x, tri):
    h, m, lo = _split3(x)
    return _dot(tri, h) + _dot(tri, m) + _dot(tri, lo)


def _scan_qk_kernel(li_ref, q_ref, k_ref, v_ref, g_ref, cos_ref, sin_ref, lr_ref, a2_ref, ba_ref, s0_ref,
                    o_ref, sn_ref, s_scr, *, c, nch, bb, lv, mode):
    del li_ref
    t = pl.program_id(1)

    @pl.when(t == 0)
    def _():
        s_scr[...] = s0_ref[0]

    mxu = BF16 if c >= 16 else F32
    row = lax.broadcasted_iota(jnp.int32, (c, 1), 0)
    colc = lax.broadcasted_iota(jnp.int32, (1, c), 1)
    causal = row >= colc
    tri = jnp.where(causal, 1.0, 0.0).astype(BF16)
    lo = lax.broadcasted_iota(jnp.int32, (1, LANES), 1) < RT_DK
    wqk = RT_H * RT_DK
    half = RT_DK // 2
    first_half = lax.broadcasted_iota(jnp.int32, (1, wqk), 1) % RT_DK < half
    mid = c // 2 - 1

    def rotary(x, cos, sin):
        partner = jnp.where(first_half, pltpu.roll(x, wqk - half, 1), pltpu.roll(x, half, 1))
        return x * cos + partner * sin

    def chunk(bi, ci):
        rows = slice(ci * c, (ci + 1) * c)
        q = q_ref[bi, rows, :]
        k = k_ref[bi, rows, :]
        v = v_ref[bi, rows, :]
        if mode == "ret":
            cos = cos_ref[rows, :]
            sin = sin_ref[rows, :]
            q = rotary(q, cos, sin)
            k = rotary(k, cos, sin)
            la = jnp.broadcast_to(ba_ref[...], (c, wqk))
        else:
            x = _dot(lr_ref[bi, rows, :].astype(BF16), a2_ref[...]) + ba_ref[...]
            la = jax.nn.log_sigmoid(x) * (1.0 / GLA_TEMP)
        k = k * (RT_DK ** -0.5)
        if lv < c:
            valid = row < lv
            la = jnp.where(valid, la, 0.0)
            k = jnp.where(valid, k, 0.0)
        b = _cumsum_rows(la, tri)
        bm = b[mid:mid + 1, :]
        bl = b[c - 1:c, :]
        qs = q * jnp.exp(b - bm)
        ks = k * jnp.exp(bm - b)
        kh = k * jnp.exp(bl - b)
        qe = q * jnp.exp(b)
        g = g_ref[bi, rows, :]
        for p in range(2):
            sl = slice(p * LANES, (p + 1) * LANES)
            S = s_scr[bi, p]
            Sm = S.astype(mxu)
            ksp = ks[:, sl].astype(mxu)
            qsp = qs[:, sl]
            qep = qe[:, sl]
            for hh in range(2):
                h = 2 * p + hh
                hs = slice(h * RT_DV, (h + 1) * RT_DV)
                msk = lo if hh == 0 else jnp.logical_not(lo)
                s = _dot(jnp.where(msk, qsp, 0.0).astype(mxu), ksp, _NT)
                s = jnp.where(causal, s, 0.0)
                o = _dot(s.astype(mxu), v[:, hs].astype(mxu)) + _dot(jnp.where(msk, qep, 0.0).astype(mxu), Sm)
                gh = g[:, hs]
                o_ref[bi, rows, hs] = (_head_rms(o) * (gh * jax.nn.sigmoid(gh))).astype(o_ref.dtype)
            upd = _dot(kh[:, sl].astype(mxu), v[:, 2 * p * RT_DV:(2 * p + 2) * RT_DV].astype(mxu), _TN)
            dcol = jnp.exp(b[:, sl].T[:, c - 1:c])
            s_scr[bi, p] = dcol * S + jnp.concatenate(
                [upd[0:RT_DK, 0:RT_DV], upd[RT_DK:2 * RT_DK, RT_DV:2 * RT_DV]], axis=0)

    for bi in range(bb):
        for ci in range(nch):
            chunk(bi, ci)

    @pl.when(t == pl.num_programs(1) - 1)
    def _():
        sn_ref[...] = s_scr[...]


def _scan_qk(main, cols, tables, lr, a2, ba, s0, li, lv, mode):
    B, L, _ = main.shape
    c = _pick(L, (128, 64, 32, 16, 8))
    tl = _pick(L, (512, 256, 128, 64, 32, 16, 8))
    bb = _pick(B, (2, 1)) if L > SUBLANES else _pick(B, (8, 4, 2, 1))
    wqk = RT_H * RT_DK
    wv = RT_H * RT_DV
    if lv is None:
        lv = c
    qi, ki, vi, gi = cols
    cos, sin = tables

    def col_spec(width, idx):
        return pl.BlockSpec((bb, tl, width), lambda b, t, li: (b, t, idx))

    tab_spec = pl.BlockSpec((tl, wqk), lambda b, t, li: (t, 0))
    return pl.pallas_call(
        functools.partial(_scan_qk_kernel, c=c, nch=tl // c, bb=bb, lv=lv, mode=mode),
        out_shape=(jax.ShapeDtypeStruct((B, L, wv), BF16),
                   jax.ShapeDtypeStruct((B, 2, 2 * RT_DK, RT_DV), F32)),
        grid_spec=pltpu.PrefetchScalarGridSpec(
            num_scalar_prefetch=1,
            grid=(B // bb, L // tl),
            in_specs=[col_spec(wqk, qi), col_spec(wqk, ki), col_spec(wv, vi), col_spec(wv, gi),
                      tab_spec, tab_spec,
                      pl.BlockSpec((bb, tl, LANES), lambda b, t, li: (b, t, 0)),
                      pl.BlockSpec((LANES, wqk), lambda b, t, li: (0, 0)),
                      pl.BlockSpec((1, wqk), lambda b, t, li: (0, 0)),
                      pl.BlockSpec((1, bb, 2, 2 * RT_DK, RT_DV), lambda b, t, li: (li[0], b, 0, 0, 0))],
            out_specs=(pl.BlockSpec((bb, tl, wv), lambda b, t, li: (b, t, 0)),
                       pl.BlockSpec((bb, 2, 2 * RT_DK, RT_DV), lambda b, t, li: (b, 0, 0, 0))),
            scratch_shapes=[pltpu.VMEM((bb, 2, 2 * RT_DK, RT_DV), F32)]),
        compiler_params=_cparams("parallel", "arbitrary"),
        name="scan_qk_" + mode,
    )(li, main, main, main, main, cos, sin, lr, a2, ba, s0)


def _scan_ssd_kernel(li_ref, z_ref, x_ref, bc_ref, lr_ref, cw_ref, cb_ref, hp_ref, nw_ref, cbuf_ref, s0_ref,
                     o_ref, sn_ref, s_scr, prev_scr, y_scr, *, c, nch, bb, lv):
    del li_ref
    t = pl.program_id(1)
    nprev = prev_scr.shape[1]
    nconv = cbuf_ref.shape[1]

    @pl.when(t == 0)
    def _():
        s_scr[...] = s0_ref[0]
        prev_scr[:, :, :] = jnp.zeros(prev_scr.shape, F32)
        prev_scr[:, nprev - nconv:, :] = cbuf_ref[...]

    mxu = BF16 if c >= 16 else F32
    row = lax.broadcasted_iota(jnp.int32, (c, 1), 0)
    colc = lax.broadcasted_iota(jnp.int32, (1, c), 1)
    causal = row >= colc
    tri = jnp.where(causal, 1.0, 0.0).astype(BF16)
    tri_t = jnp.where(row <= colc, 1.0, 0.0).astype(BF16)
    hpg = SSD_H // SSD_G
    gw = SSD_DI // SSD_G
    dt_bias = hp_ref[0:1, :]
    a_head = -jnp.exp(hp_ref[1:2, :])
    d_head = hp_ref[2:3, :]

    def conv_silu(ref, bi, ci, lo_ch):
        w = ref.shape[2]
        if ci == 0:
            ext = jnp.concatenate([prev_scr[bi, :, lo_ch:lo_ch + w], ref[bi, 0:c, :]], axis=0)
            taps = [ext[nprev - nconv + j:nprev - nconv + j + c] for j in range(nconv + 1)]
        else:
            r0 = ci * c
            taps = [ref[bi, r0 - nconv + j:r0 - nconv + j + c, :] for j in range(nconv + 1)]
        acc = cb_ref[0:1, lo_ch:lo_ch + w]
        for j, tap in enumerate(taps):
            acc = acc + tap * cw_ref[j:j + 1, lo_ch:lo_ch + w]
        return acc * jax.nn.sigmoid(acc)

    def chunk(bi, ci):
        rows = slice(ci * c, (ci + 1) * c)
        xs = conv_silu(x_ref, bi, ci, 0)
        bcm = conv_silu(bc_ref, bi, ci, SSD_DI)
        bm = bcm[:, :SSD_G * SSD_N]
        cm = bcm[:, SSD_G * SSD_N:]
        dt = jax.nn.softplus(lr_ref[bi, rows, GL_RANK:GL_RANK + SSD_H] + dt_bias)
        a = dt * a_head
        if lv < c:
            valid = row < lv
            a = jnp.where(valid, a, 0.0)
            bm = jnp.where(valid, bm, 0.0)
        ah, am, al = _split3(a)
        bcol = _dot(tri, ah) + _dot(tri, am) + _dot(tri, al)
        brow = _dot(ah, tri_t, _TN) + _dot(am, tri_t, _TN) + _dot(al, tri_t, _TN)
        for g in range(SSD_G):
            cg = cm[:, g * SSD_N:(g + 1) * SSD_N]
            bg = bm[:, g * SSD_N:(g + 1) * SSD_N]
            cgm = cg.astype(mxu)
            gmat = _dot(cgm, bg.astype(mxu), _NT)
            for hh in range(hpg):
                h = g * hpg + hh
                bc = bcol[:, h:h + 1]
                br = brow[h:h + 1, :]
                w = jnp.exp(jnp.where(causal, bc - br, NEG))
                hs = slice(h * SSD_P, (h + 1) * SSD_P)
                xraw = xs[:, hs]
                xh = (xraw * dt[:, h:h + 1]).astype(mxu)
                S = s_scr[bi, h]
                y = _dot((gmat * w).astype(mxu), xh) + jnp.exp(bc) * _dot(cgm, S.astype(mxu))
                y_scr[bi * nch + ci, :, hs] = y + d_head[:, h:h + 1] * xraw
                bl = bc[c - 1:c, :]
                upd = _dot((bg * jnp.exp(bl - bc)).astype(mxu), xh, _TN)
                s_scr[bi, h] = jnp.exp(bl) * S + upd
        z = z_ref[bi, rows, :]
        gated = y_scr[bi * nch + ci] * (z * jax.nn.sigmoid(z))
        for g in range(SSD_G):
            gs = slice(g * gw, (g + 1) * gw)
            gg = gated[:, gs]
            o_ref[bi, rows, gs] = (_head_rms(gg) * nw_ref[0:1, gs]).astype(o_ref.dtype)

    for bi in range(bb):
        for ci in range(nch):
            chunk(bi, ci)
        prev_scr[bi, :, 0:SSD_DI] = x_ref[bi, x_ref.shape[1] - nprev:, :]
        prev_scr[bi, :, SSD_DI:] = bc_ref[bi, bc_ref.shape[1] - nprev:, :]

    @pl.when(t == pl.num_programs(1) - 1)
    def _():
        sn_ref[...] = s_scr[...]


def _scan_ssd(ssd, lr, conv_w, conv_b, head_params, norm_w, conv_buf, s0, li, lv):
    B, L, _ = ssd.shape
    c = _pick(L, (128, 64, 32, 16, 8))
    tl = _pick(L, (512, 256, 128, 64, 32, 16, 8))
    bb = _pick(B, (2, 1)) if L > SUBLANES else _pick(B, (8, 4, 2, 1))
    nch = tl // c
    if lv is None:
        lv = c

    def col_spec(idx):
        return pl.BlockSpec((bb, tl, SSD_DI), lambda b, t, li: (b, t, idx))

    def full(a):
        return pl.BlockSpec(a.shape, lambda b, t, li: (0,) * a.ndim)

    return pl.pallas_call(
        functools.partial(_scan_ssd_kernel, c=c, nch=nch, bb=bb, lv=lv),
        out_shape=(jax.ShapeDtypeStruct((B, L, SSD_DI), BF16),
                   jax.ShapeDtypeStruct((B, SSD_H, SSD_N, SSD_P), F32)),
        grid_spec=pltpu.PrefetchScalarGridSpec(
            num_scalar_prefetch=1,
            grid=(B // bb, L // tl),
            in_specs=[col_spec(0), col_spec(1), col_spec(2),
                      pl.BlockSpec((bb, tl, LANES), lambda b, t, li: (b, t, 0)),
                      full(conv_w), full(conv_b), full(head_params), full(norm_w),
                      pl.BlockSpec((bb, SSD_CONV - 1, CONV_CH), lambda b, t, li: (b, 0, 0)),
                      pl.BlockSpec((1, bb, SSD_H, SSD_N, SSD_P), lambda b, t, li: (li[0], b, 0, 0, 0))],
            out_specs=(pl.BlockSpec((bb, tl, SSD_DI), lambda b, t, li: (b, t, 0)),
                       pl.BlockSpec((bb, SSD_H, SSD_N, SSD_P), lambda b, t, li: (b, 0, 0, 0))),
            scratch_shapes=[pltpu.VMEM((bb, SSD_H, SSD_N, SSD_P), F32),
                            pltpu.VMEM((bb, SUBLANES, CONV_CH), F32),
                            pltpu.VMEM((bb * nch, c, SSD_DI), F32)]),
        compiler_params=_cparams("parallel", "arbitrary"),
        name="scan_ssd",
    )(li, ssd, ssd, ssd, lr, conv_w, conv_b, head_params, norm_w, conv_buf, s0)


def _rope(x, pos, rot_dim, theta):
    half = rot_dim // 2
    inv = theta ** (-jnp.arange(half, dtype=F32) / half)
    ang = pos.astype(F32)[:, None] * inv[None, :]
    shape = (1, ang.shape[0]) + (1,) * (x.ndim - 3) + (half,)
    cos = jnp.cos(ang).reshape(shape)
    sin = jnp.sin(ang).reshape(shape)
    x1, x2, rest = x[..., :half], x[..., half:rot_dim], x[..., rot_dim:]
    return jnp.concatenate([x1 * cos - x2 * sin, x1 * sin + x2 * cos, rest], axis=-1)


def _prep_weights(w_in, gla_w_a2, w_up, w_o, w_cq, w_ck, w_cv, w_co, w_mlp1, w_mlp2):
    offs = {}
    o = 0
    names = ("da_q", "da_k", "da_v", "rt_q", "rt_k", "rt_v", "rt_g", "gl_q", "gl_k", "gl_v", "gl_r",
             "gl_lr", "sd_z", "sd_xbc", "sd_dt", "gate")
    d_model = w_in.shape[1]
    sizes = (512, 512, 512, 256, 256, 512, 512, 256, 256, 512, 512, GL_RANK, SSD_DI, CONV_CH, SSD_H,
             N_BRANCH * d_model)
    for n, s in zip(names, sizes):
        offs[n] = (o, o + s)
        o += s
    depth = w_in.shape[0]
    main = w_in[:, :, offs["da_q"][0]:offs["gl_r"][1]]
    small = jnp.concatenate(
        [w_in[:, :, offs["gl_lr"][0]:offs["gl_lr"][1]], w_in[:, :, offs["sd_dt"][0]:offs["sd_dt"][1]],
         jnp.zeros((depth, d_model, LANES - GL_RANK - SSD_H), w_in.dtype)], axis=-1)
    ssd = w_in[:, :, offs["sd_z"][0]:offs["sd_xbc"][1]]
    gate = w_in[:, :, offs["gate"][0]:offs["gate"][1]]
    a2 = jnp.concatenate(
        [gla_w_a2, jnp.zeros((depth, LANES - GL_RANK, gla_w_a2.shape[2]), gla_w_a2.dtype)], axis=1)
    c = lambda w: w.astype(BF16)
    return dict(main=c(main), small=c(small), ssd=c(ssd), gate=c(gate), a2=c(a2), w_up=c(w_up), w_o=c(w_o),
                w_cq=c(w_cq), w_ck=c(w_ck), w_cv=c(w_cv), w_co=c(w_co), w_mlp1=c(w_mlp1), w_mlp2=c(w_mlp2))


MAIN_COLS_RET = (6, 7, 4, 5)
MAIN_COLS_GLA = (12, 13, 7, 8)


def _ret_tables(pos):
    half = RT_DK // 2
    inv = RET_THETA ** (-jnp.arange(half, dtype=F32) / half)
    ang = pos.astype(F32)[:, None] * inv[None, :]
    cos, sin = jnp.cos(ang), jnp.sin(ang)
    return (jnp.tile(jnp.concatenate([cos, cos], axis=-1), (1, RT_H)),
            jnp.tile(jnp.concatenate([-sin, sin], axis=-1), (1, RT_H)))


def _layer(l, x, xb, pos, tables, W, P, mk, mv, mem_li, past, states, lv, alpha):
    B, L, D = x.shape
    M = B * L
    s_ret, s_gla, s_ssm, conv_buf, st_li = states
    xb2 = xb.reshape(M, D)
    main = _mm(xb2, W["main"][l]).reshape(B, L, -1)
    small = _mm(xb2, W["small"][l])
    ssd = _mm(xb2, W["ssd"][l]).reshape(B, L, -1)
    da_q, da_k, da_v = main[..., 0:512], main[..., 512:1024], main[..., 1024:1536]

    q = _rope(da_q.reshape(B, L, DA_H, 2, DA_DH), pos, DA_ROT, ROPE_THETA)
    k = _rope(da_k.reshape(B, L, DA_H, 2, DA_DH), pos, DA_ROT, ROPE_THETA)
    lam_init = 0.8 - 0.6 * math.exp(-0.3 * l)
    lam = (jnp.exp(jnp.sum((P["da_lq1"][l] * P["da_lk1"][l]).astype(F32)))
           - jnp.exp(jnp.sum((P["da_lq2"][l] * P["da_lk2"][l]).astype(F32))) + lam_init)
    sc = jnp.stack([lam, jnp.asarray(1.0 - lam_init, F32)]).astype(F32)
    k_rows = k.reshape(B, L, DA_H, 2 * DA_DH)
    v_rows = da_v.reshape(B, L, DA_H, DA_DV)
    qs = q.reshape(B, L, 512) * (DA_DH ** -0.5)
    if past is None:
        o_da = _flash_da((qs * math.log2(math.e)).astype(BF16), k.reshape(B, L, 512).astype(BF16),
                         jnp.swapaxes(da_v, 1, 2).astype(BF16), sc)
    else:
        cache_k, cache_v, page_table, cache_li = past
        q5 = qs.reshape(B, L, DA_H, 2, DA_DH)
        qt = jnp.einsum("bthcd,ce->bchted", q5, jnp.eye(2, dtype=F32)).reshape(B, 2 * DA_H * L, 2 * DA_DH)
        o_da = _decode_da(qt.astype(BF16), k.reshape(B, L * DA_H, 2 * DA_DH), da_v.reshape(B, L * DA_H, DA_DV),
                          cache_k, cache_v, page_table, cache_li, sc)
    o_da = o_da.astype(BF16)

    log_gamma = jnp.log1p(-jnp.exp2(-5.0 - jnp.arange(RT_H, dtype=F32)))
    la_ret = jnp.repeat(log_gamma, RT_DK).reshape(1, RT_H * RT_DK)
    small3 = small.reshape(B, L, LANES)
    o_rt, s_ret_new = _scan_qk(main, MAIN_COLS_RET, tables, small3, W["a2"][l], la_ret, s_ret, st_li, lv, "ret")

    o_gl, s_gla_new = _scan_qk(main, MAIN_COLS_GLA, tables, small3, W["a2"][l],
                               P["gla_b_a"][l].reshape(1, -1), s_gla, st_li, lv, "gla")

    head_params = jnp.stack([P["ssd_dt_bias"][l], P["ssd_a_log"][l], P["ssd_d"][l]]).astype(F32)
    o_sd, s_ssm_new = _scan_ssd(ssd, small3, P["ssd_conv_w"][l], P["ssd_conv_b"][l].reshape(1, -1), head_params,
                                P["ssd_norm_w"][l].reshape(1, -1), conv_buf, s_ssm, st_li, lv)
    n_real = L if lv is None else lv
    xpad = jnp.concatenate([conv_buf, ssd[:, max(n_real - SSD_CONV + 1, 0):n_real, SSD_DI:]], axis=1)
    conv_new = xpad[:, xpad.shape[1] - (SSD_CONV - 1):]

    branches = [o.reshape(M, BR_W) for o in (o_da, o_rt, o_gl, o_sd)]
    x2d = x.reshape(M, D)
    x1, x1b = _merge_ln(branches, xb2, W["gate"][l], W["w_up"][l], W["w_o"][l], x2d, P["ln1_g"][l],
                        P["ln1_b"][l], alpha)

    cq = _mm(x1b, W["w_cq"][l]).reshape(B, L, -1)
    if past is None:
        oc = _xattn(cq, mk, mv, mem_li).reshape(M, -1)
    else:
        qt = cq.reshape(B, L, MA_H, MA_DH).transpose(0, 2, 1, 3).reshape(B, MA_H * L, MA_DH)
        oc = _xattn_dec(qt.astype(BF16), mk, mv, mem_li).reshape(M, -1).astype(BF16)
    x2, x2b = _mm_res_ln(oc, W["w_co"][l], x1, P["ln2_g"][l], P["ln2_b"][l], alpha)

    hmid = _mm(x2b, W["w_mlp1"][l], act="relu2", out_dtype=BF16)
    x3, x3b = _mm_res_ln(hmid, W["w_mlp2"][l], x2, P["ln3_g"][l], P["ln3_b"][l], alpha)
    return (x3.reshape(B, L, D), x3b.reshape(B, L, D),
            (k_rows, v_rows, s_ret_new, s_gla_new, s_ssm_new, conv_new))


def kernel(x_prompt, x_sample, mem_prompt, cache_diff_k, cache_diff_v, page_table, state_ret, state_gla, state_ssm, state_conv, cache_mem_k, cache_mem_v, w_in, da_lq1, da_lk1, da_lq2, da_lk2, gla_w_a2, gla_b_a, ssd_conv_w, ssd_conv_b, ssd_dt_bias, ssd_a_log, ssd_d, ssd_norm_w, w_up, w_o, ln1_g, ln1_b, w_cq, w_ck, w_cv, w_co, ln2_g, ln2_b, w_mlp1, w_mlp2, ln3_g, ln3_b):
    depth = w_in.shape[0]
    alpha = (2 * depth) ** 0.25
    P = dict(da_lq1=da_lq1, da_lk1=da_lk1, da_lq2=da_lq2, da_lk2=da_lk2, gla_b_a=gla_b_a,
             ssd_conv_w=ssd_conv_w, ssd_conv_b=ssd_conv_b, ssd_dt_bias=ssd_dt_bias, ssd_a_log=ssd_a_log,
             ssd_d=ssd_d, ssd_norm_w=ssd_norm_w, ln1_g=ln1_g, ln1_b=ln1_b, ln2_g=ln2_g, ln2_b=ln2_b,
             ln3_g=ln3_g, ln3_b=ln3_b)
    W = _prep_weights(w_in, gla_w_a2, w_up, w_o, w_cq, w_ck, w_cv, w_co, w_mlp1, w_mlp2)
    zero_li = jnp.zeros((1,), jnp.int32)

    B, L, D = x_prompt.shape
    pos_p = jnp.arange(L, dtype=jnp.int32)
    tab_p = _ret_tables(pos_p)
    memb =mem_prompt.reshape(-1, D).astype(BF16)
    n_mem = mem_prompt.shape[1]
    z_ret = jnp.zeros((1, B, 2, 2 * RT_DK, RT_DV), F32)
    z_ssm = jnp.zeros((1, B, SSD_H, SSD_N, SSD_P), F32)
    z_conv = jnp.zeros((B, SSD_CONV - 1, CONV_CH), F32)
    x, xb = x_prompt, x_prompt.astype(BF16)
    outs_p = [[] for _ in range(8)]
    for l in range(depth):
        mk = _mm(memb, W["w_ck"][l]).reshape(1, B, n_mem, MA_H, MA_DH)
        mv = _mm(memb, W["w_cv"][l]).reshape(1, B, n_mem, MA_H, MA_DH)
        x, xb, (kr, vr, sr, sg, ss, cn) = _layer(
            l, x, xb, pos_p, tab_p, W, P, mk, mv, zero_li, None, (z_ret, z_ret, z_ssm, z_conv, zero_li), None,
            alpha)
        new = (kr, vr, sr.reshape(B, RT_H, RT_DK, RT_DV), sg.reshape(B, GL_H, GL_DK, GL_DV), ss, cn,
               mk[0], mv[0])
        for lst, val in zip(outs_p, new):
            lst.append(val)
    y_prompt = x

    Bd, Ld, _ = x_sample.shape
    t_pad = -(-Ld // SUBLANES) * SUBLANES
    n_pages, page = page_table.shape[1], cache_diff_k.shape[2]
    past_len = n_pages * page
    pos_s = past_len + jnp.arange(t_pad, dtype=jnp.int32)
    tab_s = _ret_tables(pos_s)
    x = jnp.pad(x_sample, ((0, 0), (0, t_pad - Ld), (0, 0)))
    xb = x.astype(BF16)
    n_pool = cache_diff_k.shape[1]
    ck = cache_diff_k.reshape(depth, n_pool, page * DA_H, 2 * DA_DH)
    cv = cache_diff_v.reshape(depth, n_pool, page * DA_H, DA_DV)
    cmk = cache_mem_k.reshape(depth, Bd, -1, MA_DH)
    cmv = cache_mem_v.reshape(depth, Bd, -1, MA_DH)
    s_ret_all = state_ret.reshape(depth, Bd, 2, 2 * RT_DK, RT_DV)
    s_gla_all = state_gla.reshape(depth, Bd, 2, 2 * GL_DK, GL_DV)
    outs_s = [[] for _ in range(6)]
    for l in range(depth):
        li = jnp.full((1,), l, jnp.int32)
        x, xb, (kr, vr, sr, sg, ss, cn) = _layer(
            l, x, xb, pos_s, tab_s, W, P, cmk, cmv, li, (ck, cv, page_table, li),
            (s_ret_all, s_gla_all, state_ssm, state_conv[l], li), Ld, alpha)
        new = (kr[:, :Ld], vr[:, :Ld], sr.reshape(Bd, RT_H, RT_DK, RT_DV), sg.reshape(Bd, GL_H, GL_DK, GL_DV),
               ss, cn)
        for lst, val in zip(outs_s, new):
            lst.append(val)
    y_sample = x[:, :Ld]

    st = lambda lst: jnp.stack(lst)
    return (y_prompt, y_sample) + tuple(st(o) for o in outs_p) + tuple(st(o) for o in outs_s)
```

```python
import functools
import math

import jax
import jax.numpy as jnp
from jax import lax
from jax.experimental import pallas as pl
from jax.experimental.pallas import tpu as pltpu

F32 = jnp.float32
BF16 = jnp.bfloat16

DA_H, DA_DH, DA_DV = 4, 64, 128
DA_ROT = DA_DH // 4
ROPE_THETA = 500000.0
RT_H, RT_DK, RT_DV = 4, 64, 128
RET_THETA = 10000.0
GL_H, GL_DK, GL_DV, GL_RANK = 4, 64, 128, 16
GLA_TEMP = 16.0
SSD_H, SSD_P, SSD_N, SSD_G, SSD_CONV = 8, 64, 128, 2, 4
SSD_DI = SSD_H * SSD_P
CONV_CH = SSD_DI + 2 * SSD_G * SSD_N
MA_H, MA_DH = 4, 128
N_BRANCH, BR_W = 4, 512
EPS = 1e-5
NEG = -1e30

VMEM_LIMIT_BYTES = 56 * 1024 * 1024
LANES = 128
SUBLANES = 8

_NT = (((1,), (1,)), ((), ()))
_TN = (((0,), (0,)), ((), ()))


def _cparams(*sem):
    return pltpu.CompilerParams(dimension_semantics=sem, vmem_limit_bytes=VMEM_LIMIT_BYTES)


def _pick(n, cands):
    for c in cands:
        if n % c == 0:
            return c
    return n


def _split3(x):
    h = x.astype(BF16)
    r = x - h.astype(F32)
    m = r.astype(BF16)
    lo = (r - m.astype(F32)).astype(BF16)
    return h, m, lo


def _dot(a, b, dims=None):
    if dims is None:
        return jnp.dot(a, b, preferred_element_type=F32)
    return lax.dot_general(a, b, dims, preferred_element_type=F32)


def _mm_kernel(x_ref, w_ref, o_ref, *, act):
    acc = _dot(x_ref[...], w_ref[...])
    if act == "relu2":
        acc = jnp.square(jnp.maximum(acc, 0.0))
    o_ref[...] = acc.astype(o_ref.dtype)


def _mm(x, w, act=None, out_dtype=F32):
    M, K = x.shape
    N = w.shape[1]
    tm = _pick(M, (1024, 512, 256, 128, 64, 32, 16, 8))
    tn = _pick(N, (1024, 768, 512, 384, 256, 128))
    return pl.pallas_call(
        functools.partial(_mm_kernel, act=act),
        out_shape=jax.ShapeDtypeStruct((M, N), out_dtype),
        grid=(N // tn, M // tm),
        in_specs=[pl.BlockSpec((tm, K), lambda j, i: (i, 0)),
                  pl.BlockSpec((K, tn), lambda j, i: (0, j))],
        out_specs=pl.BlockSpec((tm, tn), lambda j, i: (i, j)),
        compiler_params=_cparams("parallel", "parallel"),
        name="mm",
    )(x, w)


def _ln_rows(z, g, b):
    mu = jnp.mean(z, axis=-1, keepdims=True)
    zc = z - mu
    var = jnp.mean(zc * zc, axis=-1, keepdims=True)
    return zc * lax.rsqrt(var + EPS) * g + b


def _mm_res_ln_kernel(x_ref, w_ref, r_ref, g_ref, b_ref, o_ref, ob_ref, *, alpha):
    y = _dot(x_ref[...], w_ref[...])
    out = _ln_rows(alpha * r_ref[...] + y, g_ref[...], b_ref[...])
    o_ref[...] = out
    ob_ref[...] = out.astype(BF16)


def _mm_res_ln(x, w, res, g, b, alpha):
    M, K = x.shape
    N = w.shape[1]
    tm = _pick(M, (512, 256, 128, 64, 32, 16, 8))
    return pl.pallas_call(
        functools.partial(_mm_res_ln_kernel, alpha=alpha),
        out_shape=(jax.ShapeDtypeStruct((M, N), F32), jax.ShapeDtypeStruct((M, N), BF16)),
        grid=(M // tm,),
        in_specs=[pl.BlockSpec((tm, K), lambda i: (i, 0)),
                  pl.BlockSpec((K, N), lambda i: (0, 0)),
                  pl.BlockSpec((tm, N), lambda i: (i, 0)),
                  pl.BlockSpec((1, N), lambda i: (0, 0)),
                  pl.BlockSpec((1, N), lambda i: (0, 0))],
        out_specs=(pl.BlockSpec((tm, N), lambda i: (i, 0)), pl.BlockSpec((tm, N), lambda i: (i, 0))),
        compiler_params=_cparams("parallel"),
        name="mm_res_ln",
    )(x, w, res, g.reshape(1, N), b.reshape(1, N))


def _merge_ln_kernel(b0_ref, b1_ref, b2_ref, b3_ref, xb_ref, wg_ref, wup_ref, wo_ref, r_ref, g_ref, b_ref,
                     o_ref, ob_ref, *, alpha):
    d = wo_ref.shape[0]
    xb = xb_ref[...]
    acc = None
    for n, br_ref in enumerate((b0_ref, b1_ref, b2_ref, b3_ref)):
        up = _dot(br_ref[...], wup_ref[n])
        term = jax.nn.sigmoid(_dot(xb, wg_ref[:, n * d:(n + 1) * d])) * up
        acc = term if acc is None else acc + term
    y = _dot(acc.astype(BF16), wo_ref[...])
    out = _ln_rows(alpha * r_ref[...] + y, g_ref[...], b_ref[...])
    o_ref[...] = out
    ob_ref[...] = out.astype(BF16)


def _merge_ln(branches, xb, w_gate, w_up, w_o, res, g, b, alpha):
    M = xb.shape[0]
    D = w_o.shape[0]
    tm = _pick(M, (256, 128, 64, 32, 16, 8))
    br_spec = pl.BlockSpec((tm, BR_W), lambda i: (i, 0))
    return pl.pallas_call(
        functools.partial(_merge_ln_kernel, alpha=alpha),
        out_shape=(jax.ShapeDtypeStruct((M, D), F32), jax.ShapeDtypeStruct((M, D), BF16)),
        grid=(M // tm,),
        in_specs=[br_spec, br_spec, br_spec, br_spec,
                  pl.BlockSpec((tm, D), lambda i: (i, 0)),
                  pl.BlockSpec((D, N_BRANCH * D), lambda i: (0, 0)),
                  pl.BlockSpec((N_BRANCH, BR_W, D), lambda i: (0, 0, 0)),
                  pl.BlockSpec((D, D), lambda i: (0, 0)),
                  pl.BlockSpec((tm, D), lambda i: (i, 0)),
                  pl.BlockSpec((1, D), lambda i: (0, 0)),
                  pl.BlockSpec((1, D), lambda i: (0, 0))],
        out_specs=(pl.BlockSpec((tm, D), lambda i: (i, 0)), pl.BlockSpec((tm, D), lambda i: (i, 0))),
        compiler_params=_cparams("parallel"),
        name="merge_ln",
    )(*branches, xb, w_gate, w_up, w_o, res, g.reshape(1, D), b.reshape(1, D))


def _xattn_kernel(li_ref, q_ref, k_ref, v_ref, o_ref):
    del li_ref
    scale = MA_DH ** -0.5
    for h in range(MA_H):
        sl = slice(h * MA_DH, (h + 1) * MA_DH)
        q = q_ref[0, :, sl].astype(BF16)
        k = k_ref[0, 0, :, h, :].astype(BF16)
        v = v_ref[0, 0, :, h, :].astype(BF16)
        s = _dot(q, k, _NT) * scale
        m = jnp.max(s, axis=-1, keepdims=True)
        p = jnp.exp(s - m)
        a = p / jnp.sum(p, axis=-1, keepdims=True)
        o_ref[0, :, sl] = _dot(a.astype(BF16), v).astype(o_ref.dtype)


def _xattn(q, mk, mv, li):
    B, L, W = q.shape
    nm = mk.shape[2]
    tq = _pick(L, (512, 256, 128, 64, 32, 16, 8))
    kv_spec = pl.BlockSpec((1, 1, nm, MA_H, MA_DH), lambda b, i, li: (li[0], b, 0, 0, 0))
    return pl.pallas_call(
        _xattn_kernel,
        out_shape=jax.ShapeDtypeStruct((B, L, W), BF16),
        grid_spec=pltpu.PrefetchScalarGridSpec(
            num_scalar_prefetch=1,
            grid=(B, L // tq),
            in_specs=[pl.BlockSpec((1, tq, W), lambda b, i, li: (b, i, 0)), kv_spec, kv_spec],
            out_specs=pl.BlockSpec((1, tq, W), lambda b, i, li: (b, i, 0))),
        compiler_params=_cparams("parallel", "parallel"),
        name="xattn",
    )(li, q, mk, mv)


def _xattn_dec_kernel(li_ref, q_ref, k_ref, v_ref, o_ref, *, t_pad):
    del li_ref
    qt = q_ref[0]
    row_h = lax.broadcasted_iota(jnp.int32, (qt.shape[0], 1), 0) // t_pad
    lane_h = lax.broadcasted_iota(jnp.int32, (1, k_ref.shape[2]), 1) % MA_H
    s = _dot(qt, k_ref[0, 0].astype(BF16), _NT) * (MA_DH ** -0.5)
    s = jnp.where(lane_h == row_h, s, NEG)
    m = jnp.max(s, axis=-1, keepdims=True)
    p = jnp.exp(s - m)
    a = p / jnp.sum(p, axis=-1, keepdims=True)
    o = _dot(a.astype(BF16), v_ref[0, 0].astype(BF16))
    for h in range(MA_H):
        o_ref[0, :, h * MA_DH:(h + 1) * MA_DH] = o[h * t_pad:(h + 1) * t_pad]


def _xattn_dec(qt, mk, mv, li):
    B, nrow, dh = qt.shape
    t_pad = nrow // MA_H
    nk = mk.shape[2]
    kv_spec = pl.BlockSpec((1, 1, nk, dh), lambda b, li: (li[0], b, 0, 0))
    return pl.pallas_call(
        functools.partial(_xattn_dec_kernel, t_pad=t_pad),
        out_shape=jax.ShapeDtypeStruct((B, t_pad, MA_H * dh), F32),
        grid_spec=pltpu.PrefetchScalarGridSpec(
            num_scalar_prefetch=1,
            grid=(B,),
            in_specs=[pl.BlockSpec((1, nrow, dh), lambda b, li: (b, 0, 0)), kv_spec, kv_spec],
            out_specs=pl.BlockSpec((1, t_pad, MA_H * dh), lambda b, li: (b, 0, 0))),
        compiler_params=_cparams("parallel"),
        name="xattn_dec",
    )(li, qt, mk, mv)


def _head_rms(o):
    return o * lax.rsqrt(jnp.mean(o * o, axis=-1, keepdims=True) + EPS)


def _da_prep_kernel(q_ref, k_ref, v_ref, cos_ref, sin_ref, qo_ref, ko_ref, vt_ref, kr_ref, *, q_scale):
    lane = lax.broadcasted_iota(jnp.int32, (1, LANES), 1) % DA_DH
    first = lane < DA_ROT // 2
    cos = cos_ref[...]
    sin = sin_ref[...]

    def rotary(x):
        partner = jnp.where(first, pltpu.roll(x, LANES - DA_ROT // 2, 1), pltpu.roll(x, DA_ROT // 2, 1))
        return x * cos + partner * sin

    for h in range(DA_H):
        sl = slice(h * LANES, (h + 1) * LANES)
        kh = rotary(k_ref[0, :, sl])
        kr_ref[0, :, sl] = kh
        ko_ref[0, :, sl] = kh.astype(BF16)
        qo_ref[0, :, sl] = (rotary(q_ref[0, :, sl]) * q_scale).astype(BF16)
        vt_ref[0, sl, :] = v_ref[0, :, sl].T.astype(BF16)


def _da_prep(main, tables, q_scale):
    B, L, _ = main.shape
    w = DA_H * LANES
    tl = _pick(L, (512, 256, 128))
    cos, sin = tables

    def col_spec(idx):
        return pl.BlockSpec((1, tl, w), lambda b, t: (b, t, idx))

    tab_spec = pl.BlockSpec((tl, LANES), lambda b, t: (t, 0))
    row_spec = pl.BlockSpec((1, tl, w), lambda b, t: (b, t, 0))
    return pl.pallas_call(
        functools.partial(_da_prep_kernel, q_scale=q_scale),
        out_shape=(jax.ShapeDtypeStruct((B, L, w), BF16), jax.ShapeDtypeStruct((B, L, w), BF16),
                   jax.ShapeDtypeStruct((B, w, L), BF16), jax.ShapeDtypeStruct((B, L, w), F32)),
        grid=(B, L // tl),
        in_specs=[col_spec(0), col_spec(1), col_spec(2), tab_spec, tab_spec],
        out_specs=(row_spec, row_spec, pl.BlockSpec((1, w, tl), lambda b, t: (b, 0, t)), row_spec),
        compiler_params=_cparams("parallel", "parallel"),
        name="da_prep",
    )(main, main, main, cos, sin)


def _flash_da_kernel(sc_ref, q_ref, k_ref, vt_ref, o_ref, st0_scr, st1_scr, p0_scr, p1_scr, acc_scr, m_scr, l_scr,
                     *, tq, tk):
    i = pl.program_id(2)
    lam = sc_ref[0]
    out_scale = sc_ref[1]
    q = q_ref[0]
    lo = lax.broadcasted_iota(jnp.int32, (1, LANES), 1) < DA_DH
    zero = jnp.zeros_like(q)
    q12 = jnp.concatenate([jnp.where(lo, q, zero), jnp.where(lo, zero, q)], axis=0)
    col = lax.broadcasted_iota(jnp.int32, (1, 2 * tq), 1)
    qpos = i * tq + jnp.where(col >= tq, col - tq, col)
    last = (i * tq) // tk

    def qk(j):
        start = pl.multiple_of(j * tk, tk)
        return _dot(k_ref[0, pl.ds(start, tk), :], q12, _NT)

    def softmax_step(st_ref, p_ref, m, l, mask=None):
        def scores():
            st = st_ref[...]
            return st if mask is None else jnp.where(mask, st, NEG)
        m_new = jnp.maximum(m, jnp.max(scores(), axis=0, keepdims=True))
        alpha = jnp.exp2(m - m_new)
        p = jnp.exp2(scores() - m_new)
        p_ref[...] = p.astype(BF16)
        return m_new, alpha * l + jnp.sum(p, axis=0, keepdims=True), alpha

    def accumulate(j, p_ref, alpha):
        start = pl.multiple_of(j * tk, tk)
        acc_scr[...] = alpha * acc_scr[...] + _dot(vt_ref[0, :, pl.ds(start, tk)], p_ref[...])

    odd = last % 2
    m_scr[...] = jnp.full((1, 2 * tq), NEG, F32)
    l_scr[...] = jnp.zeros((1, 2 * tq), F32)
    acc_scr[...] = jnp.zeros((DA_DV, 2 * tq), F32)
    p1_scr[...] = jnp.zeros((tk, 2 * tq), BF16)

    @pl.when(odd == 1)
    def _():
        st0_scr[...] = qk(0)
        m, l, alpha = softmax_step(st0_scr, p0_scr, m_scr[...], l_scr[...])
        accumulate(0, p0_scr, alpha)
        m_scr[...] = m
        l_scr[...] = l

    st0_scr[...] = qk(odd)

    def pair(u, carry):
        a_prev, m, l = carry
        b0 = odd + 2 * u
        st1_scr[...] = qk(b0 + 1)
        accumulate(jnp.maximum(b0 - 1, 0), p1_scr, a_prev)
        m, l, a0 = softmax_step(st0_scr, p0_scr, m, l)
        st0_scr[...] = qk(b0 + 2)
        accumulate(b0, p0_scr, a0)
        m, l, a1 = softmax_step(st1_scr, p1_scr, m, l)
        return a1, m, l

    a_prev, m, l = lax.fori_loop(0, last // 2, pair, (jnp.ones((1, 2 * tq), F32), m_scr[...], l_scr[...]))
    accumulate(jnp.maximum(last - 1, 0), p1_scr, a_prev)
    kpos = last * tk + lax.broadcasted_iota(jnp.int32, (tk, 1), 0)
    _, l, alpha = softmax_step(st0_scr, p0_scr, m, l, mask=kpos <= qpos)
    accumulate(last, p0_scr, alpha)
    o12 = acc_scr[...] / l
    o = (o12[:, :tq] - lam * o12[:, tq:]).T
    o_ref[0] = (_head_rms(o) * out_scale).astype(o_ref.dtype)


def _flash_da(q, k, vt, sc):
    B, L, W = q.shape
    tq = _pick(L, (256, 128, 64, 32, 16, 8))
    tk = _pick(L, (512, 256, 128, 64, 32, 16, 8))
    return pl.pallas_call(
        functools.partial(_flash_da_kernel, tq=tq, tk=tk),
        out_shape=jax.ShapeDtypeStruct((B, L, W), BF16),
        grid=(B, DA_H, L // tq),
        in_specs=[pl.BlockSpec(memory_space=pltpu.SMEM),
                  pl.BlockSpec((1, tq, LANES), lambda b, h, i: (b, i, h)),
                  pl.BlockSpec((1, L, LANES), lambda b, h, i: (b, 0, h)),
                  pl.BlockSpec((1, DA_DV, L), lambda b, h, i: (b, h, 0))],
        out_specs=pl.BlockSpec((1, tq, LANES), lambda b, h, i: (b, i, h)),
        scratch_shapes=[pltpu.VMEM((tk, 2 * tq), F32), pltpu.VMEM((tk, 2 * tq), F32),
                        pltpu.VMEM((tk, 2 * tq), BF16), pltpu.VMEM((tk, 2 * tq), BF16),
                        pltpu.VMEM((DA_DV, 2 * tq), F32), pltpu.VMEM((1, 2 * tq), F32),
                        pltpu.VMEM((1, 2 * tq), F32)],
        compiler_params=_cparams("parallel", "parallel", "arbitrary"),
        name="flash_da",
    )(sc, q, k, vt)


def _decode_da_kernel(pt_ref, li_ref, sc_ref, qb_ref, kn_ref, vn_ref, *refs, n_pages, t_pad):
    del pt_ref, li_ref
    k_refs = refs[:n_pages]
    v_refs = refs[n_pages:2 * n_pages]
    o_ref = refs[2 * n_pages]
    lam = sc_ref[0]
    out_scale = sc_ref[1]
    qt = qb_ref[0]
    nrow = qt.shape[0]
    row = lax.broadcasted_iota(jnp.int32, (nrow, 1), 0)
    row_h = (row // t_pad) % DA_H
    row_t = row % t_pad

    def lane_ids(n):
        lane = lax.broadcasted_iota(jnp.int32, (1, n), 1)
        return lane // DA_H, lane % DA_H

    _, lane_h = lane_ids(k_refs[0].shape[2])
    valid = lane_h == row_h
    s_pages = [jnp.where(valid, _dot(qt, k_refs[p][0, 0].astype(BF16), _NT), NEG) for p in range(n_pages)]
    new_t, new_h = lane_ids(kn_ref.shape[1])
    valid_new = jnp.logical_and(new_h == row_h, new_t <= row_t)
    s_new = jnp.where(valid_new, _dot(qt, kn_ref[0].astype(BF16), _NT), NEG)

    m = jnp.max(s_new, axis=-1, keepdims=True)
    for s in s_pages:
        m = jnp.maximum(m, jnp.max(s, axis=-1, keepdims=True))

    p_new = jnp.exp(s_new - m)
    den = jnp.sum(p_new, axis=-1, keepdims=True)
    acc = _dot(p_new, vn_ref[0])
    for p in range(n_pages):
        pp = jnp.exp(s_pages[p] - m)
        den = den + jnp.sum(pp, axis=-1, keepdims=True)
        acc = acc + _dot(pp.astype(BF16), v_refs[p][0, 0].astype(BF16))
    o12 = acc / den
    half = DA_H * t_pad
    d = o12[:half] - lam * o12[half:]
    for h in range(DA_H):
        o_ref[0, :, h * DA_DV:(h + 1) * DA_DV] = _head_rms(d[h * t_pad:(h + 1) * t_pad]) * out_scale


def _decode_da(qt, k_new, v_new, cache_k, cache_v, page_table, li, sc):
    bd, nrow, _ = qt.shape
    t_pad = k_new.shape[1] // DA_H
    w = DA_H * DA_DV
    n_pages = page_table.shape[1]
    prow = cache_k.shape[2]

    def page_spec(p):
        return pl.BlockSpec((1, 1, prow, DA_DV), lambda b, pt, li: (li[0], pt[b, p], 0, 0))

    in_specs = [pl.BlockSpec(memory_space=pltpu.SMEM),
                pl.BlockSpec((1, nrow, LANES), lambda b, pt, li: (b, 0, 0)),
                pl.BlockSpec((1, t_pad * DA_H, DA_DV), lambda b, pt, li: (b, 0, 0)),
                pl.BlockSpec((1, t_pad * DA_H, DA_DV), lambda b, pt, li: (b, 0, 0))]
    in_specs += [page_spec(p) for p in range(n_pages)] * 2
    return pl.pallas_call(
        functools.partial(_decode_da_kernel, n_pages=n_pages, t_pad=t_pad),
        out_shape=jax.ShapeDtypeStruct((bd, t_pad, w), F32),
        grid_spec=pltpu.PrefetchScalarGridSpec(
            num_scalar_prefetch=2,
            grid=(bd,),
            in_specs=in_specs,
            out_specs=pl.BlockSpec((1, t_pad, w), lambda b, pt, li: (b, 0, 0))),
        compiler_params=_cparams("parallel"),
        name="decode_da",
    )(page_table, li, sc, qt, k_new, v_new, *([cache_k] * n_pages), *([cache_v] * n_pages))


def _cumsum_rows(x, tri):
    h, m, lo = _split3(x)
    return _dot(tri, h) + _dot(tri, m) + _dot(tri, lo)


def _scan_qk_kernel(li_ref, q_ref, k_ref, v_ref, g_ref, cos_ref, sin_ref, lr_ref, a2_ref, ba_ref, s0_ref,
                    o_ref, sn_ref, s_scr, *, c, nch, bb, lv, mode):
    del li_ref
    t = pl.program_id(1)

    @pl.when(t == 0)
    def _():
        s_scr[...] = s0_ref[0]

    mxu = BF16 if c >= 16 else F32
    row = lax.broadcasted_iota(jnp.int32, (c, 1), 0)
    colc = lax.broadcasted_iota(jnp.int32, (1, c), 1)
    causal = row >= colc
    tri = jnp.where(causal, 1.0, 0.0).astype(BF16)
    lo = lax.broadcasted_iota(jnp.int32, (1, LANES), 1) < RT_DK
    wqk = RT_H * RT_DK
    half = RT_DK // 2
    first_half = lax.broadcasted_iota(jnp.int32, (1, wqk), 1) % RT_DK < half
    mid = c // 2 - 1

    def rotary(x, cos, sin):
        partner = jnp.where(first_half, pltpu.roll(x, wqk - half, 1), pltpu.roll(x, half, 1))
        return x * cos + partner * sin

    bodies = [(bi, ci) for ci in range(nch) for bi in range(bb)]
    head_masks = (lo, jnp.logical_not(lo))

    def load(bi, ci):
        rows = slice(ci * c, (ci + 1) * c)
        q = q_ref[bi, rows, :]
        k = k_ref[bi, rows, :]
        if mode == "ret":
            cos = cos_ref[rows, :]
            sin = sin_ref[rows, :]
            q = rotary(q, cos, sin)
            k = rotary(k, cos, sin)
            la = jnp.broadcast_to(ba_ref[...], (c, wqk))
        else:
            x = _dot(lr_ref[bi, rows, :].astype(BF16), a2_ref[...]) + ba_ref[...]
            la = jax.nn.log_sigmoid(x) * (1.0 / GLA_TEMP)
        k = k * (RT_DK ** -0.5)
        if lv < c:
            valid = row < lv
            la = jnp.where(valid, la, 0.0)
            k = jnp.where(valid, k, 0.0)
        return dict(q=q, k=k, la=la, v=v_ref[bi, rows, :].astype(mxu))

    st = {bd: load(*bd) for bd in bodies}
    for bd in bodies:
        st[bd]["b"] = _cumsum_rows(st[bd]["la"], tri)
    for bd in bodies:
        d = st[bd]
        b, q, k = d["b"], d["q"], d["k"]
        bm = b[mid:mid + 1, :]
        bl = b[c - 1:c, :]
        d["qs"] = q * jnp.exp(b - bm)
        d["ks"] = (k * jnp.exp(bm - b)).astype(mxu)
        d["kh"] = (k * jnp.exp(bl - b)).astype(mxu)
        d["qe"] = q * jnp.exp(b)
    for bd in bodies:
        d = st[bd]
        d["s"] = [jnp.where(causal, _dot(jnp.where(head_masks[h % 2], d["qs"][:, (h // 2) * LANES:(h // 2 + 1) * LANES],
                                                     0.0).astype(mxu),
                                         d["ks"][:, (h // 2) * LANES:(h // 2 + 1) * LANES], _NT), 0.0).astype(mxu)
                  for h in range(RT_H)]
        d["upd"] = [_dot(d["kh"][:, p * LANES:(p + 1) * LANES], d["v"][:, 2 * p * RT_DV:(2 * p + 2) * RT_DV], _TN)
                    for p in range(2)]
        d["dcol"] = [jnp.exp(d["b"][:, p * LANES:(p + 1) * LANES].T[:, c - 1:c]) for p in range(2)]
    for bd in bodies:
        d = st[bd]
        d["o"] = [_dot(d["s"][h], d["v"][:, h * RT_DV:(h + 1) * RT_DV]) for h in range(RT_H)]
    for bi, ci in bodies:
        d = st[(bi, ci)]
        rows = slice(ci * c, (ci + 1) * c)
        g = g_ref[bi, rows, :]
        for p in range(2):
            S = s_scr[bi, p]
            Sm = S.astype(mxu)
            qep = d["qe"][:, p * LANES:(p + 1) * LANES]
            for hh in range(2):
                h = 2 * p + hh
                hs = slice(h * RT_DV, (h + 1) * RT_DV)
                o = d["o"][h] + _dot(jnp.where(head_masks[hh], qep, 0.0).astype(mxu), Sm)
                gh = g[:, hs]
                o_ref[bi, rows, hs] = (_head_rms(o) * (gh * jax.nn.sigmoid(gh))).astype(o_ref.dtype)
            upd = d["upd"][p]
            s_scr[bi, p] = d["dcol"][p] * S + jnp.concatenate(
                [upd[0:RT_DK, 0:RT_DV], upd[RT_DK:2 * RT_DK, RT_DV:2 * RT_DV]], axis=0)

    @pl.when(t == pl.num_programs(1) - 1)
    def _():
        sn_ref[...] = s_scr[...]


def _scan_qk(main, cols, tables, lr, a2, ba, s0, li, lv, mode):
    B, L, _ = main.shape
    c = _pick(L, (128, 64, 32, 16, 8))
    tl = _pick(L, (512, 256, 128, 64, 32, 16, 8))
    bb = _pick(B, (2, 1)) if L > SUBLANES else _pick(B, (8, 4, 2, 1))
    wqk = RT_H * RT_DK
    wv = RT_H * RT_DV
    if lv is None:
        lv = c
    qi, ki, vi, gi = cols
    cos, sin = tables

    def col_spec(width, idx):
        return pl.BlockSpec((bb, tl, width), lambda b, t, li: (b, t, idx))

    tab_spec = pl.BlockSpec((tl, wqk), lambda b, t, li: (t, 0))
    return pl.pallas_call(
        functools.partial(_scan_qk_kernel, c=c, nch=tl // c, bb=bb, lv=lv, mode=mode),
        out_shape=(jax.ShapeDtypeStruct((B, L, wv), BF16),
                   jax.ShapeDtypeStruct((B, 2, 2 * RT_DK, RT_DV), F32)),
        grid_spec=pltpu.PrefetchScalarGridSpec(
            num_scalar_prefetch=1,
            grid=(B // bb, L // tl),
            in_specs=[col_spec(wqk, qi), col_spec(wqk, ki), col_spec(wv, vi), col_spec(wv, gi),
                      tab_spec, tab_spec,
                      pl.BlockSpec((bb, tl, LANES), lambda b, t, li: (b, t, 0)),
                      pl.BlockSpec((LANES, wqk), lambda b, t, li: (0, 0)),
                      pl.BlockSpec((1, wqk), lambda b, t, li: (0, 0)),
                      pl.BlockSpec((1, bb, 2, 2 * RT_DK, RT_DV), lambda b, t, li: (li[0], b, 0, 0, 0))],
            out_specs=(pl.BlockSpec((bb, tl, wv), lambda b, t, li: (b, t, 0)),
                       pl.BlockSpec((bb, 2, 2 * RT_DK, RT_DV), lambda b, t, li: (b, 0, 0, 0))),
            scratch_shapes=[pltpu.VMEM((bb, 2, 2 * RT_DK, RT_DV), F32)]),
        compiler_params=_cparams("parallel", "arbitrary"),
        name="scan_qk_" + mode,
    )(li, main, main, main, main, cos, sin, lr, a2, ba, s0)


def _scan_ssd_kernel(li_ref, z_ref, x_ref, bc_ref, lr_ref, cw_ref, cb_ref, hp_ref, nw_ref, cbuf_ref, s0_ref,
                     o_ref, sn_ref, s_scr, prev_scr, y_scr, *, c, nch, bb, lv):
    del li_ref
    t = pl.program_id(1)
    nprev = prev_scr.shape[1]
    nconv = cbuf_ref.shape[1]

    @pl.when(t == 0)
    def _():
        s_scr[...] = s0_ref[0]
        prev_scr[:, :, :] = jnp.zeros(prev_scr.shape, F32)
        prev_scr[:, nprev - nconv:, :] = cbuf_ref[...]

    mxu = BF16 if c >= 16 else F32
    row = lax.broadcasted_iota(jnp.int32, (c, 1), 0)
    colc = lax.broadcasted_iota(jnp.int32, (1, c), 1)
    causal = row >= colc
    tri = jnp.where(causal, 1.0, 0.0).astype(BF16)
    tri_t = jnp.where(row <= colc, 1.0, 0.0).astype(BF16)
    hpg = SSD_H // SSD_G
    gw = SSD_DI // SSD_G
    dt_bias = hp_ref[0:1, :]
    a_head = -jnp.exp(hp_ref[1:2, :])
    d_head = hp_ref[2:3, :]

    def conv_silu(ref, bi, ci, lo_ch):
        w = ref.shape[2]
        if ci == 0:
            ext = jnp.concatenate([prev_scr[bi, :, lo_ch:lo_ch + w], ref[bi, 0:c, :]], axis=0)
            taps = [ext[nprev - nconv + j:nprev - nconv + j + c] for j in range(nconv + 1)]
        else:
            r0 = ci * c
            taps = [ref[bi, r0 - nconv + j:r0 - nconv + j + c, :] for j in range(nconv + 1)]
        acc = cb_ref[0:1, lo_ch:lo_ch + w]
        for j, tap in enumerate(taps):
            acc = acc + tap * cw_ref[j:j + 1, lo_ch:lo_ch + w]
        return acc * jax.nn.sigmoid(acc)

    bodies = [(bi, ci) for ci in range(nch) for bi in range(bb)]

    def load(bi, ci):
        rows = slice(ci * c, (ci + 1) * c)
        xs = conv_silu(x_ref, bi, ci, 0)
        bcm = conv_silu(bc_ref, bi, ci, SSD_DI)
        bm = bcm[:, :SSD_G * SSD_N]
        dt = jax.nn.softplus(lr_ref[bi, rows, GL_RANK:GL_RANK + SSD_H] + dt_bias)
        a = dt * a_head
        if lv < c:
            valid = row < lv
            a = jnp.where(valid, a, 0.0)
            bm = jnp.where(valid, bm, 0.0)
        return dict(xs=xs, bm=bm, cm=bcm[:, SSD_G * SSD_N:].astype(mxu), dt=dt, a=a)

    st = {bd: load(*bd) for bd in bodies}
    for bd in bodies:
        d = st[bd]
        ah, am, al = _split3(d["a"])
        d["bcol"] = _dot(tri, ah) + _dot(tri, am) + _dot(tri, al)
        d["brow"] = _dot(ah, tri_t, _TN) + _dot(am, tri_t, _TN) + _dot(al, tri_t, _TN)
        d["gmat"] = [_dot(d["cm"][:, g * SSD_N:(g + 1) * SSD_N],
                          d["bm"][:, g * SSD_N:(g + 1) * SSD_N].astype(mxu), _NT) for g in range(SSD_G)]
    for bd in bodies:
        d = st[bd]
        d["xh"], d["y"], d["upd"] = [], [], []
        for h in range(SSD_H):
            g = h // hpg
            bc = d["bcol"][:, h:h + 1]
            w = jnp.exp(jnp.where(causal, bc - d["brow"][h:h + 1, :], NEG))
            xh = (d["xs"][:, h * SSD_P:(h + 1) * SSD_P] * d["dt"][:, h:h + 1]).astype(mxu)
            d["y"].append(_dot((d["gmat"][g] * w).astype(mxu), xh))
            bg = d["bm"][:, g * SSD_N:(g + 1) * SSD_N]
            d["upd"].append(_dot((bg * jnp.exp(bc[c - 1:c, :] - bc)).astype(mxu), xh, _TN))
    for bi, ci in bodies:
        d = st[(bi, ci)]
        slot = bi * nch + ci
        for h in range(SSD_H):
            g = h // hpg
            hs = slice(h * SSD_P, (h + 1) * SSD_P)
            bc = d["bcol"][:, h:h + 1]
            S = s_scr[bi, h]
            y = d["y"][h] + jnp.exp(bc) * _dot(d["cm"][:, g * SSD_N:(g + 1) * SSD_N], S.astype(mxu))
            y_scr[slot, :, hs] = y + d_head[:, h:h + 1] * d["xs"][:, hs]
            s_scr[bi, h] = jnp.exp(bc[c - 1:c, :]) * S + d["upd"][h]
    for bi, ci in bodies:
        rows = slice(ci * c, (ci + 1) * c)
        z = z_ref[bi, rows, :]
        gated = y_scr[bi * nch + ci] * (z * jax.nn.sigmoid(z))
        for g in range(SSD_G):
            gs = slice(g * gw, (g + 1) * gw)
            o_ref[bi, rows, gs] = (_head_rms(gated[:, gs]) * nw_ref[0:1, gs]).astype(o_ref.dtype)

    for bi in range(bb):
        prev_scr[bi, :, 0:SSD_DI] = x_ref[bi, x_ref.shape[1] - nprev:, :]
        prev_scr[bi, :, SSD_DI:] = bc_ref[bi, bc_ref.shape[1] - nprev:, :]

    @pl.when(t == pl.num_programs(1) - 1)
    def _():
        sn_ref[...] = s_scr[...]


def _scan_ssd(ssd, lr, conv_w, conv_b, head_params, norm_w, conv_buf, s0, li, lv):
    B, L, _ = ssd.shape
    c = _pick(L, (128, 64, 32, 16, 8))
    tl = _pick(L, (512, 256, 128, 64, 32, 16, 8))
    bb = _pick(B, (2, 1)) if L > SUBLANES else _pick(B, (8, 4, 2, 1))
    nch = tl // c
    if lv is None:
        lv = c

    def col_spec(idx):
        return pl.BlockSpec((bb, tl, SSD_DI), lambda b, t, li: (b, t, idx))

    def full(a):
        return pl.BlockSpec(a.shape, lambda b, t, li: (0,) * a.ndim)

    return pl.pallas_call(
        functools.partial(_scan_ssd_kernel, c=c, nch=nch, bb=bb, lv=lv),
        out_shape=(jax.ShapeDtypeStruct((B, L, SSD_DI), BF16),
                   jax.ShapeDtypeStruct((B, SSD_H, SSD_N, SSD_P), F32)),
        grid_spec=pltpu.PrefetchScalarGridSpec(
            num_scalar_prefetch=1,
            grid=(B // bb, L // tl),
            in_specs=[col_spec(0), col_spec(1), col_spec(2),
                      pl.BlockSpec((bb, tl, LANES), lambda b, t, li: (b, t, 0)),
                      full(conv_w), full(conv_b), full(head_params), full(norm_w),
                      pl.BlockSpec((bb, SSD_CONV - 1, CONV_CH), lambda b, t, li: (b, 0, 0)),
                      pl.BlockSpec((1, bb, SSD_H, SSD_N, SSD_P), lambda b, t, li: (li[0], b, 0, 0, 0))],
            out_specs=(pl.BlockSpec((bb, tl, SSD_DI), lambda b, t, li: (b, t, 0)),
                       pl.BlockSpec((bb, SSD_H, SSD_N, SSD_P), lambda b, t, li: (b, 0, 0, 0))),
            scratch_shapes=[pltpu.VMEM((bb, SSD_H, SSD_N, SSD_P), F32),
                            pltpu.VMEM((bb, SUBLANES, CONV_CH), F32),
                            pltpu.VMEM((bb * nch, c, SSD_DI), F32)]),
        compiler_params=_cparams("parallel", "arbitrary"),
        name="scan_ssd",
    )(li, ssd, ssd, ssd, lr, conv_w, conv_b, head_params, norm_w, conv_buf, s0)


def _rope(x, pos, rot_dim, theta):
    half = rot_dim // 2
    inv = theta ** (-jnp.arange(half, dtype=F32) / half)
    ang = pos.astype(F32)[:, None] * inv[None, :]
    shape = (1, ang.shape[0]) + (1,) * (x.ndim - 3) + (half,)
    cos = jnp.cos(ang).reshape(shape)
    sin = jnp.sin(ang).reshape(shape)
    x1, x2, rest = x[..., :half], x[..., half:rot_dim], x[..., rot_dim:]
    return jnp.concatenate([x1 * cos - x2 * sin, x1 * sin + x2 * cos, rest], axis=-1)


def _prep_weights(w_in, gla_w_a2, w_up, w_o, w_cq, w_ck, w_cv, w_co, w_mlp1, w_mlp2):
    offs = {}
    o = 0
    names = ("da_q", "da_k", "da_v", "rt_q", "rt_k", "rt_v", "rt_g", "gl_q", "gl_k", "gl_v", "gl_r",
             "gl_lr", "sd_z", "sd_xbc", "sd_dt", "gate")
    d_model = w_in.shape[1]
    sizes = (512, 512, 512, 256, 256, 512, 512, 256, 256, 512, 512, GL_RANK, SSD_DI, CONV_CH, SSD_H,
             N_BRANCH * d_model)
    for n, s in zip(names, sizes):
        offs[n] = (o, o + s)
        o += s
    depth = w_in.shape[0]
    main = w_in[:, :, offs["da_q"][0]:offs["gl_r"][1]]
    small = jnp.concatenate(
        [w_in[:, :, offs["gl_lr"][0]:offs["gl_lr"][1]], w_in[:, :, offs["sd_dt"][0]:offs["sd_dt"][1]],
         jnp.zeros((depth, d_model, LANES - GL_RANK - SSD_H), w_in.dtype)], axis=-1)
    ssd = w_in[:, :, offs["sd_z"][0]:offs["sd_xbc"][1]]
    gate = w_in[:, :, offs["gate"][0]:offs["gate"][1]]
    a2 = jnp.concatenate(
        [gla_w_a2, jnp.zeros((depth, LANES - GL_RANK, gla_w_a2.shape[2]), gla_w_a2.dtype)], axis=1)
    c = lambda w: w.astype(BF16)
    return dict(main=c(main), small=c(small), ssd=c(ssd), gate=c(gate), a2=c(a2), w_up=c(w_up), w_o=c(w_o),
                w_cq=c(w_cq), w_ck=c(w_ck), w_cv=c(w_cv), w_co=c(w_co), w_mlp1=c(w_mlp1), w_mlp2=c(w_mlp2))


MAIN_COLS_RET = (6, 7, 4, 5)
MAIN_COLS_GLA = (12, 13, 7, 8)


def _ret_tables(pos):
    half = RT_DK // 2
    inv = RET_THETA ** (-jnp.arange(half, dtype=F32) / half)
    ang = pos.astype(F32)[:, None] * inv[None, :]
    cos, sin = jnp.cos(ang), jnp.sin(ang)
    return (jnp.tile(jnp.concatenate([cos, cos], axis=-1), (1, RT_H)),
            jnp.tile(jnp.concatenate([-sin, sin], axis=-1), (1, RT_H)))


def _da_tables(pos):
    half = DA_ROT // 2
    inv = ROPE_THETA ** (-jnp.arange(half, dtype=F32) / half)
    ang = pos.astype(F32)[:, None] * inv[None, :]
    cos, sin = jnp.cos(ang), jnp.sin(ang)
    rest = DA_DH - DA_ROT
    n = pos.shape[0]
    cos64 = jnp.concatenate([cos, cos, jnp.ones((n, rest), F32)], axis=-1)
    sin64 = jnp.concatenate([-sin, sin, jnp.zeros((n, rest), F32)], axis=-1)
    return jnp.tile(cos64, (1, LANES // DA_DH)), jnp.tile(sin64, (1, LANES // DA_DH))


def _layer(l, x, xb, pos, tables, W, P, mk, mv, mem_li, past, states, lv, alpha):
    B, L, D = x.shape
    M = B * L
    s_ret, s_gla, s_ssm, conv_buf, st_li = states
    xb2 = xb.reshape(M, D)
    main = _mm(xb2, W["main"][l]).reshape(B, L, -1)
    small = _mm(xb2, W["small"][l])
    ssd = _mm(xb2, W["ssd"][l]).reshape(B, L, -1)
    da_q, da_k, da_v = main[..., 0:512], main[..., 512:1024], main[..., 1024:1536]

    lam_init = 0.8 - 0.6 * math.exp(-0.3 * l)
    lam = (jnp.exp(jnp.sum((P["da_lq1"][l] * P["da_lk1"][l]).astype(F32)))
           - jnp.exp(jnp.sum((P["da_lq2"][l] * P["da_lk2"][l]).astype(F32))) + lam_init)
    sc = jnp.stack([lam, jnp.asarray(1.0 - lam_init, F32)]).astype(F32)
    v_rows = da_v.reshape(B, L, DA_H, DA_DV)
    if past is None:
        qb, kb, vt, k_rot = _da_prep(main, tables[1], DA_DH ** -0.5 * math.log2(math.e))
        k_rows = k_rot.reshape(B, L, DA_H, 2 * DA_DH)
        o_da = _flash_da(qb, kb, vt, sc)
    else:
        q = _rope(da_q.reshape(B, L, DA_H, 2, DA_DH), pos, DA_ROT, ROPE_THETA)
        k = _rope(da_k.reshape(B, L, DA_H, 2, DA_DH), pos, DA_ROT, ROPE_THETA)
        k_rows = k.reshape(B, L, DA_H, 2 * DA_DH)
        qs = q.reshape(B, L, 512) * (DA_DH ** -0.5)
        cache_k, cache_v, page_table, cache_li = past
        q5 = qs.reshape(B, L, DA_H, 2, DA_DH)
        qt = jnp.einsum("bthcd,ce->bchted", q5, jnp.eye(2, dtype=F32)).reshape(B, 2 * DA_H * L, 2 * DA_DH)
        o_da = _decode_da(qt.astype(BF16), k.reshape(B, L * DA_H, 2 * DA_DH), da_v.reshape(B, L * DA_H, DA_DV),
                          cache_k, cache_v, page_table, cache_li, sc)
    o_da = o_da.astype(BF16)

    log_gamma = jnp.log1p(-jnp.exp2(-5.0 - jnp.arange(RT_H, dtype=F32)))
    la_ret = jnp.repeat(log_gamma, RT_DK).reshape(1, RT_H * RT_DK)
    small3 = small.reshape(B, L, LANES)
    o_rt, s_ret_new = _scan_qk(main, MAIN_COLS_RET, tables[0], small3, W["a2"][l], la_ret, s_ret, st_li, lv, "ret")

    o_gl, s_gla_new = _scan_qk(main, MAIN_COLS_GLA, tables[0], small3, W["a2"][l],
                               P["gla_b_a"][l].reshape(1, -1), s_gla, st_li, lv, "gla")

    head_params = jnp.stack([P["ssd_dt_bias"][l], P["ssd_a_log"][l], P["ssd_d"][l]]).astype(F32)
    o_sd, s_ssm_new = _scan_ssd(ssd, small3, P["ssd_conv_w"][l], P["ssd_conv_b"][l].reshape(1, -1), head_params,
                                P["ssd_norm_w"][l].reshape(1, -1), conv_buf, s_ssm, st_li, lv)
    n_real = L if lv is None else lv
    xpad = jnp.concatenate([conv_buf, ssd[:, max(n_real - SSD_CONV + 1, 0):n_real, SSD_DI:]], axis=1)
    conv_new = xpad[:, xpad.shape[1] - (SSD_CONV - 1):]

    branches = [o.reshape(M, BR_W) for o in (o_da, o_rt, o_gl, o_sd)]
    x2d = x.reshape(M, D)
    x1, x1b = _merge_ln(branches, xb2, W["gate"][l], W["w_up"][l], W["w_o"][l], x2d, P["ln1_g"][l],
                        P["ln1_b"][l], alpha)

    cq = _mm(x1b, W["w_cq"][l]).reshape(B, L, -1)
    if past is None:
        oc = _xattn(cq, mk, mv, mem_li).reshape(M, -1)
    else:
        qt = cq.reshape(B, L, MA_H, MA_DH).transpose(0, 2, 1, 3).reshape(B, MA_H * L, MA_DH)
        oc = _xattn_dec(qt.astype(BF16), mk, mv, mem_li).reshape(M, -1).astype(BF16)
    x2, x2b = _mm_res_ln(oc, W["w_co"][l], x1, P["ln2_g"][l], P["ln2_b"][l], alpha)

    hmid = _mm(x2b, W["w_mlp1"][l], act="relu2", out_dtype=BF16)
    x3, x3b = _mm_res_ln(hmid, W["w_mlp2"][l], x2, P["ln3_g"][l], P["ln3_b"][l], alpha)
    return (x3.reshape(B, L, D), x3b.reshape(B, L, D),
            (k_rows, v_rows, s_ret_new, s_gla_new, s_ssm_new, conv_new))


def kernel(x_prompt, x_sample, mem_prompt, cache_diff_k, cache_diff_v, page_table, state_ret, state_gla, state_ssm, state_conv, cache_mem_k, cache_mem_v, w_in, da_lq1, da_lk1, da_lq2, da_lk2, gla_w_a2, gla_b_a, ssd_conv_w, ssd_conv_b, ssd_dt_bias, ssd_a_log, ssd_d, ssd_norm_w, w_up, w_o, ln1_g, ln1_b, w_cq, w_ck, w_cv, w_co, ln2_g, ln2_b, w_mlp1, w_mlp2, ln3_g, ln3_b):
    depth = w_in.shape[0]
    alpha = (2 * depth) ** 0.25
    P = dict(da_lq1=da_lq1, da_lk1=da_lk1, da_lq2=da_lq2, da_lk2=da_lk2, gla_b_a=gla_b_a,
             ssd_conv_w=ssd_conv_w, ssd_conv_b=ssd_conv_b, ssd_dt_bias=ssd_dt_bias, ssd_a_log=ssd_a_log,
             ssd_d=ssd_d, ssd_norm_w=ssd_norm_w, ln1_g=ln1_g, ln1_b=ln1_b, ln2_g=ln2_g, ln2_b=ln2_b,
             ln3_g=ln3_g, ln3_b=ln3_b)
    W = _prep_weights(w_in, gla_w_a2, w_up, w_o, w_cq, w_ck, w_cv, w_co, w_mlp1, w_mlp2)
    zero_li = jnp.zeros((1,), jnp.int32)

    B, L, D = x_prompt.shape
    pos_p = jnp.arange(L, dtype=jnp.int32)
    tab_p = (_ret_tables(pos_p), _da_tables(pos_p))
    memb =mem_prompt.reshape(-1, D).astype(BF16)
    n_mem = mem_prompt.shape[1]
    z_ret = jnp.zeros((1, B, 2, 2 * RT_DK, RT_DV), F32)
    z_ssm = jnp.zeros((1, B, SSD_H, SSD_N, SSD_P), F32)
    z_conv = jnp.zeros((B, SSD_CONV - 1, CONV_CH), F32)
    x, xb = x_prompt, x_prompt.astype(BF16)
    outs_p = [[] for _ in range(8)]
    for l in range(depth):
        mk = _mm(memb, W["w_ck"][l]).reshape(1, B, n_mem, MA_H, MA_DH)
        mv = _mm(memb, W["w_cv"][l]).reshape(1, B, n_mem, MA_H, MA_DH)
        x, xb, (kr, vr, sr, sg, ss, cn) = _layer(
            l, x, xb, pos_p, tab_p, W, P, mk, mv, zero_li, None, (z_ret, z_ret, z_ssm, z_conv, zero_li), None,
            alpha)
        new = (kr, vr, sr.reshape(B, RT_H, RT_DK, RT_DV), sg.reshape(B, GL_H, GL_DK, GL_DV), ss, cn,
               mk[0], mv[0])
        for lst, val in zip(outs_p, new):
            lst.append(val)
    y_prompt = x

    Bd, Ld, _ = x_sample.shape
    t_pad = -(-Ld // SUBLANES) * SUBLANES
    n_pages, page = page_table.shape[1], cache_diff_k.shape[2]
    past_len = n_pages * page
    pos_s = past_len + jnp.arange(t_pad, dtype=jnp.int32)
    tab_s = (_ret_tables(pos_s), None)
    x = jnp.pad(x_sample, ((0, 0), (0, t_pad - Ld), (0, 0)))
    xb = x.astype(BF16)
    n_pool = cache_diff_k.shape[1]
    ck = cache_diff_k.reshape(depth, n_pool, page * DA_H, 2 * DA_DH)
    cv = cache_diff_v.reshape(depth, n_pool, page * DA_H, DA_DV)
    cmk = cache_mem_k.reshape(depth, Bd, -1, MA_DH)
    cmv = cache_mem_v.reshape(depth, Bd, -1, MA_DH)
    s_ret_all = state_ret.reshape(depth, Bd, 2, 2 * RT_DK, RT_DV)
    s_gla_all = state_gla.reshape(depth, Bd, 2, 2 * GL_DK, GL_DV)
    outs_s = [[] for _ in range(6)]
    for l in range(depth):
        li = jnp.full((1,), l, jnp.int32)
        x, xb, (kr, vr, sr, sg, ss, cn) = _layer(
            l, x, xb, pos_s, tab_s, W, P, cmk, cmv, li, (ck, cv, page_table, li),
            (s_ret_all, s_gla_all, state_ssm, state_conv[l], li), Ld, alpha)
        new = (kr[:, :Ld], vr[:, :Ld], sr.reshape(Bd, RT_H, RT_DK, RT_DV), sg.reshape(Bd, GL_H, GL_DK, GL_DV),
               ss, cn)
        for lst, val in zip(outs_s, new):
            lst.append(val)
    y_sample = x[:, :Ld]

    st = lambda lst: jnp.stack(lst)
    return (y_prompt, y_sample) + tuple(st(o) for o in outs_p) + tuple(st(o) for o in outs_s)
```

```python
import functools
import math

import jax
import jax.numpy as jnp
from jax import lax
from jax.experimental import pallas as pl
from jax.experimental.pallas import tpu as pltpu

F32 = jnp.float32
BF16 = jnp.bfloat16

DA_H, DA_DH, DA_DV = 4, 64, 128
DA_ROT = DA_DH // 4
ROPE_THETA = 500000.0
RT_H, RT_DK, RT_DV = 4, 64, 128
RET_THETA = 10000.0
GL_H, GL_DK, GL_DV, GL_RANK = 4, 64, 128, 16
GLA_TEMP = 16.0
SSD_H, SSD_P, SSD_N, SSD_G, SSD_CONV = 8, 64, 128, 2, 4
SSD_DI = SSD_H * SSD_P
CONV_CH = SSD_DI + 2 * SSD_G * SSD_N
MA_H, MA_DH = 4, 128
N_BRANCH, BR_W = 4, 512
EPS = 1e-5
NEG = -1e30

VMEM_LIMIT_BYTES = 56 * 1024 * 1024
LANES = 128
SUBLANES = 8

_NT = (((1,), (1,)), ((), ()))
_TN = (((0,), (0,)), ((), ()))


def _cparams(*sem):
    return pltpu.CompilerParams(dimension_semantics=sem, vmem_limit_bytes=VMEM_LIMIT_BYTES)


def _pick(n, cands):
    for c in cands:
        if n % c == 0:
            return c
    return n


def _split3(x):
    h = x.astype(BF16)
    r = x - h.astype(F32)
    m = r.astype(BF16)
    lo = (r - m.astype(F32)).astype(BF16)
    return h, m, lo


def _dot(a, b, dims=None):
    if dims is None:
        return jnp.dot(a, b, preferred_element_type=F32)
    return lax.dot_general(a, b, dims, preferred_element_type=F32)


def _mm_kernel(x_ref, w_ref, o_ref, *, act):
    acc = _dot(x_ref[...], w_ref[...])
    if act == "relu2":
        acc = jnp.square(jnp.maximum(acc, 0.0))
    o_ref[...] = acc.astype(o_ref.dtype)


def _mm(x, w, act=None, out_dtype=F32):
    M, K = x.shape
    N = w.shape[1]
    tm = _pick(M, (2048, 1024, 512, 256, 128, 64, 32, 16, 8))
    tn = _pick(N, (1024, 768, 512, 384, 256, 128))
    return pl.pallas_call(
        functools.partial(_mm_kernel, act=act),
        out_shape=jax.ShapeDtypeStruct((M, N), out_dtype),
        grid=(N // tn, M // tm),
        in_specs=[pl.BlockSpec((tm, K), lambda j, i: (i, 0)),
                  pl.BlockSpec((K, tn), lambda j, i: (0, j))],
        out_specs=pl.BlockSpec((tm, tn), lambda j, i: (i, j)),
        compiler_params=_cparams("parallel", "parallel"),
        name="mm",
    )(x, w)


def _ln_rows(z, g, b):
    mu = jnp.mean(z, axis=-1, keepdims=True)
    zc = z - mu
    var = jnp.mean(zc * zc, axis=-1, keepdims=True)
    return zc * lax.rsqrt(var + EPS) * g + b


def _mm_res_ln_kernel(x_ref, w_ref, r_ref, g_ref, b_ref, o_ref, ob_ref, *, alpha):
    y = _dot(x_ref[...], w_ref[...])
    out = _ln_rows(alpha * r_ref[...] + y, g_ref[...], b_ref[...])
    o_ref[...] = out
    ob_ref[...] = out.astype(BF16)


def _mm_res_ln(x, w, res, g, b, alpha):
    M, K = x.shape
    N = w.shape[1]
    tm = _pick(M, (512, 256, 128, 64, 32, 16, 8))
    return pl.pallas_call(
        functools.partial(_mm_res_ln_kernel, alpha=alpha),
        out_shape=(jax.ShapeDtypeStruct((M, N), F32), jax.ShapeDtypeStruct((M, N), BF16)),
        grid=(M // tm,),
        in_specs=[pl.BlockSpec((tm, K), lambda i: (i, 0)),
                  pl.BlockSpec((K, N), lambda i: (0, 0)),
                  pl.BlockSpec((tm, N), lambda i: (i, 0)),
                  pl.BlockSpec((1, N), lambda i: (0, 0)),
                  pl.BlockSpec((1, N), lambda i: (0, 0))],
        out_specs=(pl.BlockSpec((tm, N), lambda i: (i, 0)), pl.BlockSpec((tm, N), lambda i: (i, 0))),
        compiler_params=_cparams("parallel"),
        name="mm_res_ln",
    )(x, w, res, g.reshape(1, N), b.reshape(1, N))


def _merge_ln_kernel(b0_ref, b1_ref, b2_ref, b3_ref, xb_ref, wg_ref, wup_ref, wo_ref, r_ref, g_ref, b_ref,
                     o_ref, ob_ref, *, alpha):
    d = wo_ref.shape[0]
    xb = xb_ref[...]
    acc = None
    for n, br_ref in enumerate((b0_ref, b1_ref, b2_ref, b3_ref)):
        up = _dot(br_ref[...], wup_ref[n])
        term = jax.nn.sigmoid(_dot(xb, wg_ref[:, n * d:(n + 1) * d])) * up
        acc = term if acc is None else acc + term
    y = _dot(acc.astype(BF16), wo_ref[...])
    out = _ln_rows(alpha * r_ref[...] + y, g_ref[...], b_ref[...])
    o_ref[...] = out
    ob_ref[...] = out.astype(BF16)


def _merge_ln(branches, xb, w_gate, w_up, w_o, res, g, b, alpha):
    M = xb.shape[0]
    D = w_o.shape[0]
    tm = _pick(M, (256, 128, 64, 32, 16, 8))
    br_spec = pl.BlockSpec((tm, BR_W), lambda i: (i, 0))
    return pl.pallas_call(
        functools.partial(_merge_ln_kernel, alpha=alpha),
        out_shape=(jax.ShapeDtypeStruct((M, D), F32), jax.ShapeDtypeStruct((M, D), BF16)),
        grid=(M // tm,),
        in_specs=[br_spec, br_spec, br_spec, br_spec,
                  pl.BlockSpec((tm, D), lambda i: (i, 0)),
                  pl.BlockSpec((D, N_BRANCH * D), lambda i: (0, 0)),
                  pl.BlockSpec((N_BRANCH, BR_W, D), lambda i: (0, 0, 0)),
                  pl.BlockSpec((D, D), lambda i: (0, 0)),
                  pl.BlockSpec((tm, D), lambda i: (i, 0)),
                  pl.BlockSpec((1, D), lambda i: (0, 0)),
                  pl.BlockSpec((1, D), lambda i: (0, 0))],
        out_specs=(pl.BlockSpec((tm, D), lambda i: (i, 0)), pl.BlockSpec((tm, D), lambda i: (i, 0))),
        compiler_params=_cparams("parallel"),
        name="merge_ln",
    )(*branches, xb, w_gate, w_up, w_o, res, g.reshape(1, D), b.reshape(1, D))


def _xattn_kernel(li_ref, q_ref, k_ref, v_ref, o_ref):
    del li_ref
    scale = MA_DH ** -0.5
    for h in range(MA_H):
        sl = slice(h * MA_DH, (h + 1) * MA_DH)
        q = q_ref[0, :, sl].astype(BF16)
        k = k_ref[0, 0, :, h, :].astype(BF16)
        v = v_ref[0, 0, :, h, :].astype(BF16)
        s = _dot(q, k, _NT) * scale
        m = jnp.max(s, axis=-1, keepdims=True)
        p = jnp.exp(s - m)
        a = p / jnp.sum(p, axis=-1, keepdims=True)
        o_ref[0, :, sl] = _dot(a.astype(BF16), v).astype(o_ref.dtype)


def _xattn(q, mk, mv, li):
    B, L, W = q.shape
    nm = mk.shape[2]
    tq = _pick(L, (512, 256, 128, 64, 32, 16, 8))
    kv_spec = pl.BlockSpec((1, 1, nm, MA_H, MA_DH), lambda b, i, li: (li[0], b, 0, 0, 0))
    return pl.pallas_call(
        _xattn_kernel,
        out_shape=jax.ShapeDtypeStruct((B, L, W), BF16),
        grid_spec=pltpu.PrefetchScalarGridSpec(
            num_scalar_prefetch=1,
            grid=(B, L // tq),
            in_specs=[pl.BlockSpec((1, tq, W), lambda b, i, li: (b, i, 0)), kv_spec, kv_spec],
            out_specs=pl.BlockSpec((1, tq, W), lambda b, i, li: (b, i, 0))),
        compiler_params=_cparams("parallel", "parallel"),
        name="xattn",
    )(li, q, mk, mv)


def _xattn_dec_kernel(li_ref, q_ref, k_ref, v_ref, o_ref, *, t_pad):
    del li_ref
    qt = q_ref[0]
    row_h = lax.broadcasted_iota(jnp.int32, (qt.shape[0], 1), 0) // t_pad
    lane_h = lax.broadcasted_iota(jnp.int32, (1, k_ref.shape[2]), 1) % MA_H
    s = _dot(qt, k_ref[0, 0].astype(BF16), _NT) * (MA_DH ** -0.5)
    s = jnp.where(lane_h == row_h, s, NEG)
    m = jnp.max(s, axis=-1, keepdims=True)
    p = jnp.exp(s - m)
    a = p / jnp.sum(p, axis=-1, keepdims=True)
    o = _dot(a.astype(BF16), v_ref[0, 0].astype(BF16))
    for h in range(MA_H):
        o_ref[0, :, h * MA_DH:(h + 1) * MA_DH] = o[h * t_pad:(h + 1) * t_pad]


def _xattn_dec(qt, mk, mv, li):
    B, nrow, dh = qt.shape
    t_pad = nrow // MA_H
    nk = mk.shape[2]
    kv_spec = pl.BlockSpec((1, 1, nk, dh), lambda b, li: (li[0], b, 0, 0))
    return pl.pallas_call(
        functools.partial(_xattn_dec_kernel, t_pad=t_pad),
        out_shape=jax.ShapeDtypeStruct((B, t_pad, MA_H * dh), F32),
        grid_spec=pltpu.PrefetchScalarGridSpec(
            num_scalar_prefetch=1,
            grid=(B,),
            in_specs=[pl.BlockSpec((1, nrow, dh), lambda b, li: (b, 0, 0)), kv_spec, kv_spec],
            out_specs=pl.BlockSpec((1, t_pad, MA_H * dh), lambda b, li: (b, 0, 0))),
        compiler_params=_cparams("parallel"),
        name="xattn_dec",
    )(li, qt, mk, mv)


def _head_rms(o):
    return o * lax.rsqrt(jnp.mean(o * o, axis=-1, keepdims=True) + EPS)


def _da_prep_kernel(q_ref, k_ref, v_ref, cos_ref, sin_ref, qo_ref, ko_ref, vt_ref, kr_ref, *, q_scale):
    lane = lax.broadcasted_iota(jnp.int32, (1, LANES), 1) % DA_DH
    first = lane < DA_ROT // 2
    cos = cos_ref[...]
    sin = sin_ref[...]

    def rotary(x):
        partner = jnp.where(first, pltpu.roll(x, LANES - DA_ROT // 2, 1), pltpu.roll(x, DA_ROT // 2, 1))
        return x * cos + partner * sin

    for h in range(DA_H):
        sl = slice(h * LANES, (h + 1) * LANES)
        kh = rotary(k_ref[0, :, sl])
        kr_ref[0, :, sl] = kh
        ko_ref[0, :, sl] = kh.astype(BF16)
        qo_ref[0, :, sl] = (rotary(q_ref[0, :, sl]) * q_scale).astype(BF16)
        vt_ref[0, sl, :] = v_ref[0, :, sl].T.astype(BF16)


def _da_prep(main, tables, q_scale):
    B, L, _ = main.shape
    w = DA_H * LANES
    tl = _pick(L, (512, 256, 128))
    cos, sin = tables

    def col_spec(idx):
        return pl.BlockSpec((1, tl, w), lambda b, t: (b, t, idx))

    tab_spec = pl.BlockSpec((tl, LANES), lambda b, t: (t, 0))
    row_spec = pl.BlockSpec((1, tl, w), lambda b, t: (b, t, 0))
    return pl.pallas_call(
        functools.partial(_da_prep_kernel, q_scale=q_scale),
        out_shape=(jax.ShapeDtypeStruct((B, L, w), BF16), jax.ShapeDtypeStruct((B, L, w), BF16),
                   jax.ShapeDtypeStruct((B, w, L), BF16), jax.ShapeDtypeStruct((B, L, w), F32)),
        grid=(B, L // tl),
        in_specs=[col_spec(0), col_spec(1), col_spec(2), tab_spec, tab_spec],
        out_specs=(row_spec, row_spec, pl.BlockSpec((1, w, tl), lambda b, t: (b, 0, t)), row_spec),
        compiler_params=_cparams("parallel", "parallel"),
        name="da_prep",
    )(main, main, main, cos, sin)


def _flash_da_kernel(sc_ref, q_ref, k_ref, vt_ref, o_ref, st0_scr, st1_scr, p0_scr, p1_scr, acc_scr, m_scr, l_scr,
                     *, tq, tk):
    i = pl.program_id(2)
    lam = sc_ref[0]
    out_scale = sc_ref[1]
    q = q_ref[0]
    lo = lax.broadcasted_iota(jnp.int32, (1, LANES), 1) < DA_DH
    zero = jnp.zeros_like(q)
    q12 = jnp.concatenate([jnp.where(lo, q, zero), jnp.where(lo, zero, q)], axis=0)
    col = lax.broadcasted_iota(jnp.int32, (1, 2 * tq), 1)
    qpos = i * tq + jnp.where(col >= tq, col - tq, col)
    last = (i * tq) // tk

    def qk(j):
        start = pl.multiple_of(j * tk, tk)
        return _dot(k_ref[0, pl.ds(start, tk), :], q12, _NT)

    def scores_to(st_ref, j):
        st = qk(j)
        st_ref[...] = st
        return jnp.max(st, axis=0, keepdims=True)

    def softmax_step(st_ref, p_ref, m, l, blk_max, mask=None):
        st = st_ref[...]
        if mask is not None:
            st = jnp.where(mask, st, NEG)
            blk_max = jnp.max(st, axis=0, keepdims=True)
        m_new = jnp.maximum(m, blk_max)
        alpha = jnp.exp2(m - m_new)
        p = jnp.exp2(st - m_new)
        p_ref[...] = p.astype(BF16)
        return m_new, alpha * l + jnp.sum(p, axis=0, keepdims=True), alpha

    def accumulate(j, p_ref, alpha):
        start = pl.multiple_of(j * tk, tk)
        acc_scr[...] = alpha * acc_scr[...] + _dot(vt_ref[0, :, pl.ds(start, tk)], p_ref[...])

    odd = last % 2
    m_scr[...] = jnp.full((1, 2 * tq), NEG, F32)
    l_scr[...] = jnp.zeros((1, 2 * tq), F32)
    acc_scr[...] = jnp.zeros((DA_DV, 2 * tq), F32)
    p1_scr[...] = jnp.zeros((tk, 2 * tq), BF16)

    @pl.when(odd == 1)
    def _():
        mx = scores_to(st0_scr, 0)
        m, l, alpha = softmax_step(st0_scr, p0_scr, m_scr[...], l_scr[...], mx)
        accumulate(0, p0_scr, alpha)
        m_scr[...] = m
        l_scr[...] = l

    mx_first = scores_to(st0_scr, odd)

    def pair(u, carry):
        a_prev, m, l, mx0 = carry
        b0 = odd + 2 * u
        mx1 = scores_to(st1_scr, b0 + 1)
        accumulate(jnp.maximum(b0 - 1, 0), p1_scr, a_prev)
        m, l, a0 = softmax_step(st0_scr, p0_scr, m, l, mx0)
        mx0 = scores_to(st0_scr, b0 + 2)
        accumulate(b0, p0_scr, a0)
        m, l, a1 = softmax_step(st1_scr, p1_scr, m, l, mx1)
        return a1, m, l, mx0

    a_prev, m, l, mx0 = lax.fori_loop(
        0, last // 2, pair, (jnp.ones((1, 2 * tq), F32), m_scr[...], l_scr[...], mx_first))
    accumulate(jnp.maximum(last - 1, 0), p1_scr, a_prev)
    kpos = last * tk + lax.broadcasted_iota(jnp.int32, (tk, 1), 0)
    _, l, alpha = softmax_step(st0_scr, p0_scr, m, l, mx0, mask=kpos <= qpos)
    accumulate(last, p0_scr, alpha)
    o12 = acc_scr[...] / l
    o = (o12[:, :tq] - lam * o12[:, tq:]).T
    o_ref[0] = (_head_rms(o) * out_scale).astype(o_ref.dtype)


def _flash_da(q, k, vt, sc):
    B, L, W = q.shape
    tq = _pick(L, (512, 256, 128, 64, 32, 16, 8))
    tk = _pick(L, (512, 256, 128, 64, 32, 16, 8))
    return pl.pallas_call(
        functools.partial(_flash_da_kernel, tq=tq, tk=tk),
        out_shape=jax.ShapeDtypeStruct((B, L, W), BF16),
        grid=(B, DA_H, L // tq),
        in_specs=[pl.BlockSpec(memory_space=pltpu.SMEM),
                  pl.BlockSpec((1, tq, LANES), lambda b, h, i: (b, i, h)),
                  pl.BlockSpec((1, L, LANES), lambda b, h, i: (b, 0, h)),
                  pl.BlockSpec((1, DA_DV, L), lambda b, h, i: (b, h, 0))],
        out_specs=pl.BlockSpec((1, tq, LANES), lambda b, h, i: (b, i, h)),
        scratch_shapes=[pltpu.VMEM((tk, 2 * tq), F32), pltpu.VMEM((tk, 2 * tq), F32),
                        pltpu.VMEM((tk, 2 * tq), BF16), pltpu.VMEM((tk, 2 * tq), BF16),
                        pltpu.VMEM((DA_DV, 2 * tq), F32), pltpu.VMEM((1, 2 * tq), F32),
                        pltpu.VMEM((1, 2 * tq), F32)],
        compiler_params=_cparams("parallel", "parallel", "arbitrary"),
        name="flash_da",
    )(sc, q, k, vt)


def _decode_da_kernel(pt_ref, li_ref, sc_ref, qb_ref, kn_ref, vn_ref, *refs, n_pages, t_pad):
    del pt_ref, li_ref
    k_refs = refs[:n_pages]
    v_refs = refs[n_pages:2 * n_pages]
    o_ref = refs[2 * n_pages]
    lam = sc_ref[0]
    out_scale = sc_ref[1]
    qt = qb_ref[0]
    nrow = qt.shape[0]
    row = lax.broadcasted_iota(jnp.int32, (nrow, 1), 0)
    row_h = (row // t_pad) % DA_H
    row_t = row % t_pad

    def lane_ids(n):
        lane = lax.broadcasted_iota(jnp.int32, (1, n), 1)
        return lane // DA_H, lane % DA_H

    _, lane_h = lane_ids(k_refs[0].shape[2])
    valid = lane_h == row_h
    s_pages = [jnp.where(valid, _dot(qt, k_refs[p][0, 0].astype(BF16), _NT), NEG) for p in range(n_pages)]
    new_t, new_h = lane_ids(kn_ref.shape[1])
    valid_new = jnp.logical_and(new_h == row_h, new_t <= row_t)
    s_new = jnp.where(valid_new, _dot(qt, kn_ref[0].astype(BF16), _NT), NEG)

    m = jnp.max(s_new, axis=-1, keepdims=True)
    for s in s_pages:
        m = jnp.maximum(m, jnp.max(s, axis=-1, keepdims=True))

    p_new = jnp.exp(s_new - m)
    den = jnp.sum(p_new, axis=-1, keepdims=True)
    acc = _dot(p_new, vn_ref[0])
    for p in range(n_pages):
        pp = jnp.exp(s_pages[p] - m)
        den = den + jnp.sum(pp, axis=-1, keepdims=True)
        acc = acc + _dot(pp.astype(BF16), v_refs[p][0, 0].astype(BF16))
    o12 = acc / den
    half = DA_H * t_pad
    d = o12[:half] - lam * o12[half:]
    for h in range(DA_H):
        o_ref[0, :, h * DA_DV:(h + 1) * DA_DV] = _head_rms(d[h * t_pad:(h + 1) * t_pad]) * out_scale


def _decode_da(qt, k_new, v_new, cache_k, cache_v, page_table, li, sc):
    bd, nrow, _ = qt.shape
    t_pad = k_new.shape[1] // DA_H
    w = DA_H * DA_DV
    n_pages = page_table.shape[1]
    prow = cache_k.shape[2]

    def page_spec(p):
        return pl.BlockSpec((1, 1, prow, DA_DV), lambda b, pt, li: (li[0], pt[b, p], 0, 0))

    in_specs = [pl.BlockSpec(memory_space=pltpu.SMEM),
                pl.BlockSpec((1, nrow, LANES), lambda b, pt, li: (b, 0, 0)),
                pl.BlockSpec((1, t_pad * DA_H, DA_DV), lambda b, pt, li: (b, 0, 0)),
                pl.BlockSpec((1, t_pad * DA_H, DA_DV), lambda b, pt, li: (b, 0, 0))]
    in_specs += [page_spec(p) for p in range(n_pages)] * 2
    return pl.pallas_call(
        functools.partial(_decode_da_kernel, n_pages=n_pages, t_pad=t_pad),
        out_shape=jax.ShapeDtypeStruct((bd, t_pad, w), F32),
        grid_spec=pltpu.PrefetchScalarGridSpec(
            num_scalar_prefetch=2,
            grid=(bd,),
            in_specs=in_specs,
            out_specs=pl.BlockSpec((1, t_pad, w), lambda b, pt, li: (b, 0, 0))),
        compiler_params=_cparams("parallel"),
        name="decode_da",
    )(page_table, li, sc, qt, k_new, v_new, *([cache_k] * n_pages), *([cache_v] * n_pages))


def _cumsum_rows(x, tri):
    h, m, lo = _split3(x)
    return _dot(tri, h) + _dot(tri, m) + _dot(tri, lo)


def _scan_qk_kernel(li_ref, q_ref, k_ref, v_ref, g_ref, cos_ref, sin_ref, lr_ref, a2_ref, ba_ref, s0_ref,
                    o_ref, sn_ref, s_scr, *, c, nch, bb, lv, mode):
    del li_ref
    t = pl.program_id(1)

    @pl.when(t == 0)
    def _():
        s_scr[...] = s0_ref[0]

    mxu = BF16 if c >= 16 else F32
    row = lax.broadcasted_iota(jnp.int32, (c, 1), 0)
    colc = lax.broadcasted_iota(jnp.int32, (1, c), 1)
    causal = row >= colc
    tri = jnp.where(causal, 1.0, 0.0).astype(BF16)
    lo = lax.broadcasted_iota(jnp.int32, (1, LANES), 1) < RT_DK
    wqk = RT_H * RT_DK
    half = RT_DK // 2
    first_half = lax.broadcasted_iota(jnp.int32, (1, wqk), 1) % RT_DK < half
    mid = c // 2 - 1

    def rotary(x, cos, sin):
        partner = jnp.where(first_half, pltpu.roll(x, wqk - half, 1), pltpu.roll(x, half, 1))
        return x * cos + partner * sin

    bodies = [(bi, ci) for ci in range(nch) for bi in range(bb)]
    head_masks = (lo, jnp.logical_not(lo))

    def load(bi, ci):
        rows = slice(ci * c, (ci + 1) * c)
        q = q_ref[bi, rows, :]
        k = k_ref[bi, rows, :]
        if mode == "ret":
            cos = cos_ref[rows, :]
            sin = sin_ref[rows, :]
            q = rotary(q, cos, sin)
            k = rotary(k, cos, sin)
            la = jnp.broadcast_to(ba_ref[...], (c, wqk))
        else:
            x = _dot(lr_ref[bi, rows, :].astype(BF16), a2_ref[...]) + ba_ref[...]
            la = jax.nn.log_sigmoid(x) * (1.0 / GLA_TEMP)
        k = k * (RT_DK ** -0.5)
        if lv < c:
            valid = row < lv
            la = jnp.where(valid, la, 0.0)
            k = jnp.where(valid, k, 0.0)
        return dict(q=q, k=k, la=la, v=v_ref[bi, rows, :].astype(mxu))

    st = {bd: load(*bd) for bd in bodies}
    for bd in bodies:
        st[bd]["b"] = _cumsum_rows(st[bd]["la"], tri)
    for bd in bodies:
        d = st[bd]
        b, q, k = d["b"], d["q"], d["k"]
        bm = b[mid:mid + 1, :]
        bl = b[c - 1:c, :]
        d["qs"] = q * jnp.exp(b - bm)
        d["ks"] = (k * jnp.exp(bm - b)).astype(mxu)
        d["kh"] = (k * jnp.exp(bl - b)).astype(mxu)
        d["qe"] = q * jnp.exp(b)
    for bd in bodies:
        d = st[bd]
        d["s"] = [jnp.where(causal, _dot(jnp.where(head_masks[h % 2], d["qs"][:, (h // 2) * LANES:(h // 2 + 1) * LANES],
                                                     0.0).astype(mxu),
                                         d["ks"][:, (h // 2) * LANES:(h // 2 + 1) * LANES], _NT), 0.0).astype(mxu)
                  for h in range(RT_H)]
        d["upd"] = [_dot(d["kh"][:, p * LANES:(p + 1) * LANES], d["v"][:, 2 * p * RT_DV:(2 * p + 2) * RT_DV], _TN)
                    for p in range(2)]
        d["dcol"] = [jnp.exp(d["b"][:, p * LANES:(p + 1) * LANES].T[:, c - 1:c]) for p in range(2)]
    for bd in bodies:
        d = st[bd]
        d["o"] = [_dot(d["s"][h], d["v"][:, h * RT_DV:(h + 1) * RT_DV]) for h in range(RT_H)]
    for bi, ci in bodies:
        d = st[(bi, ci)]
        rows = slice(ci * c, (ci + 1) * c)
        g = g_ref[bi, rows, :]
        for p in range(2):
            S = s_scr[bi, p]
            Sm = S.astype(mxu)
            qep = d["qe"][:, p * LANES:(p + 1) * LANES]
            for hh in range(2):
                h = 2 * p + hh
                hs = slice(h * RT_DV, (h + 1) * RT_DV)
                o = d["o"][h] + _dot(jnp.where(head_masks[hh], qep, 0.0).astype(mxu), Sm)
                gh = g[:, hs]
                o_ref[bi, rows, hs] = (_head_rms(o) * (gh * jax.nn.sigmoid(gh))).astype(o_ref.dtype)
            upd = d["upd"][p]
            s_scr[bi, p] = d["dcol"][p] * S + jnp.concatenate(
                [upd[0:RT_DK, 0:RT_DV], upd[RT_DK:2 * RT_DK, RT_DV:2 * RT_DV]], axis=0)

    @pl.when(t == pl.num_programs(1) - 1)
    def _():
        sn_ref[...] = s_scr[...]


def _scan_qk(main, cols, tables, lr, a2, ba, s0, li, lv, mode):
    B, L, _ = main.shape
    c = _pick(L, (128, 64, 32, 16, 8))
    tl = _pick(L, (512, 256, 128, 64, 32, 16, 8))
    bb = _pick(B, (2, 1)) if L > SUBLANES else _pick(B, (8, 4, 2, 1))
    wqk = RT_H * RT_DK
    wv = RT_H * RT_DV
    if lv is None:
        lv = c
    qi, ki, vi, gi = cols
    cos, sin = tables

    def col_spec(width, idx):
        return pl.BlockSpec((bb, tl, width), lambda b, t, li: (b, t, idx))

    tab_spec = pl.BlockSpec((tl, wqk), lambda b, t, li: (t, 0))
    return pl.pallas_call(
        functools.partial(_scan_qk_kernel, c=c, nch=tl // c, bb=bb, lv=lv, mode=mode),
        out_shape=(jax.ShapeDtypeStruct((B, L, wv), BF16),
                   jax.ShapeDtypeStruct((B, 2, 2 * RT_DK, RT_DV), F32)),
        grid_spec=pltpu.PrefetchScalarGridSpec(
            num_scalar_prefetch=1,
            grid=(B // bb, L // tl),
            in_specs=[col_spec(wqk, qi), col_spec(wqk, ki), col_spec(wv, vi), col_spec(wv, gi),
                      tab_spec, tab_spec,
                      pl.BlockSpec((bb, tl, LANES), lambda b, t, li: (b, t, 0)),
                      pl.BlockSpec((LANES, wqk), lambda b, t, li: (0, 0)),
                      pl.BlockSpec((1, wqk), lambda b, t, li: (0, 0)),
                      pl.BlockSpec((1, bb, 2, 2 * RT_DK, RT_DV), lambda b, t, li: (li[0], b, 0, 0, 0))],
            out_specs=(pl.BlockSpec((bb, tl, wv), lambda b, t, li: (b, t, 0)),
                       pl.BlockSpec((bb, 2, 2 * RT_DK, RT_DV), lambda b, t, li: (b, 0, 0, 0))),
            scratch_shapes=[pltpu.VMEM((bb, 2, 2 * RT_DK, RT_DV), F32)]),
        compiler_params=_cparams("parallel", "arbitrary"),
        name="scan_qk_" + mode,
    )(li, main, main, main, main, cos, sin, lr, a2, ba, s0)


def _scan_ssd_kernel(li_ref, z_ref, x_ref, bc_ref, lr_ref, cw_ref, cb_ref, hp_ref, nw_ref, cbuf_ref, s0_ref,
                     o_ref, sn_ref, s_scr, prev_scr, y_scr, *, c, nch, bb, lv):
    del li_ref
    t = pl.program_id(1)
    nprev = prev_scr.shape[1]
    nconv = cbuf_ref.shape[1]

    @pl.when(t == 0)
    def _():
        s_scr[...] = s0_ref[0]
        prev_scr[:, :, :] = jnp.zeros(prev_scr.shape, F32)
        prev_scr[:, nprev - nconv:, :] = cbuf_ref[...]

    mxu = BF16 if c >= 16 else F32
    row = lax.broadcasted_iota(jnp.int32, (c, 1), 0)
    colc = lax.broadcasted_iota(jnp.int32, (1, c), 1)
    causal = row >= colc
    tri = jnp.where(causal, 1.0, 0.0).astype(BF16)
    tri_t = jnp.where(row <= colc, 1.0, 0.0).astype(BF16)
    hpg = SSD_H // SSD_G
    gw = SSD_DI // SSD_G
    dt_bias = hp_ref[0:1, :]
    a_head = -jnp.exp(hp_ref[1:2, :])
    d_head = hp_ref[2:3, :]

    def conv_silu(ref, bi, ci, lo_ch):
        w = ref.shape[2]
        if ci == 0:
            ext = jnp.concatenate([prev_scr[bi, :, lo_ch:lo_ch + w], ref[bi, 0:c, :]], axis=0)
            taps = [ext[nprev - nconv + j:nprev - nconv + j + c] for j in range(nconv + 1)]
        else:
            r0 = ci * c
            taps = [ref[bi, r0 - nconv + j:r0 - nconv + j + c, :] for j in range(nconv + 1)]
        acc = cb_ref[0:1, lo_ch:lo_ch + w]
        for j, tap in enumerate(taps):
            acc = acc + tap * cw_ref[j:j + 1, lo_ch:lo_ch + w]
        return acc * jax.nn.sigmoid(acc)

    bodies = [(bi, ci) for ci in range(nch) for bi in range(bb)]

    def load(bi, ci):
        rows = slice(ci * c, (ci + 1) * c)
        xs = conv_silu(x_ref, bi, ci, 0)
        bcm = conv_silu(bc_ref, bi, ci, SSD_DI)
        bm = bcm[:, :SSD_G * SSD_N]
        dt = jax.nn.softplus(lr_ref[bi, rows, GL_RANK:GL_RANK + SSD_H] + dt_bias)
        a = dt * a_head
        if lv < c:
            valid = row < lv
            a = jnp.where(valid, a, 0.0)
            bm = jnp.where(valid, bm, 0.0)
        return dict(xs=xs, bm=bm, cm=bcm[:, SSD_G * SSD_N:].astype(mxu), dt=dt, a=a)

    st = {bd: load(*bd) for bd in bodies}
    for bd in bodies:
        d = st[bd]
        ah, am, al = _split3(d["a"])
        d["bcol"] = _dot(tri, ah) + _dot(tri, am) + _dot(tri, al)
        d["brow"] = _dot(ah, tri_t, _TN) + _dot(am, tri_t, _TN) + _dot(al, tri_t, _TN)
        d["gmat"] = [_dot(d["cm"][:, g * SSD_N:(g + 1) * SSD_N],
                          d["bm"][:, g * SSD_N:(g + 1) * SSD_N].astype(mxu), _NT) for g in range(SSD_G)]
    for bd in bodies:
        d = st[bd]
        d["xh"], d["y"], d["upd"] = [], [], []
        for h in range(SSD_H):
            g = h // hpg
            bc = d["bcol"][:, h:h + 1]
            w = jnp.exp(jnp.where(causal, bc - d["brow"][h:h + 1, :], NEG))
            xh = (d["xs"][:, h * SSD_P:(h + 1) * SSD_P] * d["dt"][:, h:h + 1]).astype(mxu)
            d["y"].append(_dot((d["gmat"][g] * w).astype(mxu), xh))
            bg = d["bm"][:, g * SSD_N:(g + 1) * SSD_N]
            d["upd"].append(_dot((bg * jnp.exp(bc[c - 1:c, :] - bc)).astype(mxu), xh, _TN))
    for bi, ci in bodies:
        d = st[(bi, ci)]
        slot = bi * nch + ci
        for h in range(SSD_H):
            g = h // hpg
            hs = slice(h * SSD_P, (h + 1) * SSD_P)
            bc = d["bcol"][:, h:h + 1]
            S = s_scr[bi, h]
            y = d["y"][h] + jnp.exp(bc) * _dot(d["cm"][:, g * SSD_N:(g + 1) * SSD_N], S.astype(mxu))
            y_scr[slot, :, hs] = y + d_head[:, h:h + 1] * d["xs"][:, hs]
            s_scr[bi, h] = jnp.exp(bc[c - 1:c, :]) * S + d["upd"][h]
    for bi, ci in bodies:
        rows = slice(ci * c, (ci + 1) * c)
        z = z_ref[bi, rows, :]
        gated = y_scr[bi * nch + ci] * (z * jax.nn.sigmoid(z))
        for g in range(SSD_G):
            gs = slice(g * gw, (g + 1) * gw)
            o_ref[bi, rows, gs] = (_head_rms(gated[:, gs]) * nw_ref[0:1, gs]).astype(o_ref.dtype)

    for bi in range(bb):
        prev_scr[bi, :, 0:SSD_DI] = x_ref[bi, x_ref.shape[1] - nprev:, :]
        prev_scr[bi, :, SSD_DI:] = bc_ref[bi, bc_ref.shape[1] - nprev:, :]

    @pl.when(t == pl.num_programs(1) - 1)
    def _():
        sn_ref[...] = s_scr[...]


def _scan_ssd(ssd, lr, conv_w, conv_b, head_params, norm_w, conv_buf, s0, li, lv):
    B, L, _ = ssd.shape
    c = _pick(L, (128, 64, 32, 16, 8))
    tl = _pick(L, (256, 128, 64, 32, 16, 8))
    bb = _pick(B, (2, 1)) if L > SUBLANES else _pick(B, (8, 4, 2, 1))
    nch = tl // c
    if lv is None:
        lv = c

    def col_spec(idx):
        return pl.BlockSpec((bb, tl, SSD_DI), lambda b, t, li: (b, t, idx))

    def full(a):
        return pl.BlockSpec(a.shape, lambda b, t, li: (0,) * a.ndim)

    return pl.pallas_call(
        functools.partial(_scan_ssd_kernel, c=c, nch=nch, bb=bb, lv=lv),
        out_shape=(jax.ShapeDtypeStruct((B, L, SSD_DI), BF16),
                   jax.ShapeDtypeStruct((B, SSD_H, SSD_N, SSD_P), F32)),
        grid_spec=pltpu.PrefetchScalarGridSpec(
            num_scalar_prefetch=1,
            grid=(B // bb, L // tl),
            in_specs=[col_spec(0), col_spec(1), col_spec(2),
                      pl.BlockSpec((bb, tl, LANES), lambda b, t, li: (b, t, 0)),
                      full(conv_w), full(conv_b), full(head_params), full(norm_w),
                      pl.BlockSpec((bb, SSD_CONV - 1, CONV_CH), lambda b, t, li: (b, 0, 0)),
                      pl.BlockSpec((1, bb, SSD_H, SSD_N, SSD_P), lambda b, t, li: (li[0], b, 0, 0, 0))],
            out_specs=(pl.BlockSpec((bb, tl, SSD_DI), lambda b, t, li: (b, t, 0)),
                       pl.BlockSpec((bb, SSD_H, SSD_N, SSD_P), lambda b, t, li: (b, 0, 0, 0))),
            scratch_shapes=[pltpu.VMEM((bb, SSD_H, SSD_N, SSD_P), F32),
                            pltpu.VMEM((bb, SUBLANES, CONV_CH), F32),
                            pltpu.VMEM((bb * nch, c, SSD_DI), F32)]),
        compiler_params=_cparams("parallel", "arbitrary"),
        name="scan_ssd",
    )(li, ssd, ssd, ssd, lr, conv_w, conv_b, head_params, norm_w, conv_buf, s0)


def _rope(x, pos, rot_dim, theta):
    half = rot_dim // 2
    inv = theta ** (-jnp.arange(half, dtype=F32) / half)
    ang = pos.astype(F32)[:, None] * inv[None, :]
    shape = (1, ang.shape[0]) + (1,) * (x.ndim - 3) + (half,)
    cos = jnp.cos(ang).reshape(shape)
    sin = jnp.sin(ang).reshape(shape)
    x1, x2, rest = x[..., :half], x[..., half:rot_dim], x[..., rot_dim:]
    return jnp.concatenate([x1 * cos - x2 * sin, x1 * sin + x2 * cos, rest], axis=-1)


def _prep_weights(w_in, gla_w_a2, w_up, w_o, w_cq, w_ck, w_cv, w_co, w_mlp1, w_mlp2):
    offs = {}
    o = 0
    names = ("da_q", "da_k", "da_v", "rt_q", "rt_k", "rt_v", "rt_g", "gl_q", "gl_k", "gl_v", "gl_r",
             "gl_lr", "sd_z", "sd_xbc", "sd_dt", "gate")
    d_model = w_in.shape[1]
    sizes = (512, 512, 512, 256, 256, 512, 512, 256, 256, 512, 512, GL_RANK, SSD_DI, CONV_CH, SSD_H,
             N_BRANCH * d_model)
    for n, s in zip(names, sizes):
        offs[n] = (o, o + s)
        o += s
    depth = w_in.shape[0]
    main = w_in[:, :, offs["da_q"][0]:offs["gl_r"][1]]
    small = jnp.concatenate(
        [w_in[:, :, offs["gl_lr"][0]:offs["gl_lr"][1]], w_in[:, :, offs["sd_dt"][0]:offs["sd_dt"][1]],
         jnp.zeros((depth, d_model, LANES - GL_RANK - SSD_H), w_in.dtype)], axis=-1)
    ssd = w_in[:, :, offs["sd_z"][0]:offs["sd_xbc"][1]]
    gate = w_in[:, :, offs["gate"][0]:offs["gate"][1]]
    a2 = jnp.concatenate(
        [gla_w_a2, jnp.zeros((depth, LANES - GL_RANK, gla_w_a2.shape[2]), gla_w_a2.dtype)], axis=1)
    c = lambda w: w.astype(BF16)
    return dict(main=c(main), small=c(small), ssd=c(ssd), gate=c(gate), a2=c(a2), w_up=c(w_up), w_o=c(w_o),
                w_cq=c(w_cq), w_ck=c(w_ck), w_cv=c(w_cv), w_co=c(w_co), w_mlp1=c(w_mlp1), w_mlp2=c(w_mlp2))


MAIN_COLS_RET = (6, 7, 4, 5)
MAIN_COLS_GLA = (12, 13, 7, 8)


def _ret_tables(pos):
    half = RT_DK // 2
    inv = RET_THETA ** (-jnp.arange(half, dtype=F32) / half)
    ang = pos.astype(F32)[:, None] * inv[None, :]
    cos, sin = jnp.cos(ang), jnp.sin(ang)
    return (jnp.tile(jnp.concatenate([cos, cos], axis=-1), (1, RT_H)),
            jnp.tile(jnp.concatenate([-sin, sin], axis=-1), (1, RT_H)))


def _da_tables(pos):
    half = DA_ROT // 2
    inv = ROPE_THETA ** (-jnp.arange(half, dtype=F32) / half)
    ang = pos.astype(F32)[:, None] * inv[None, :]
    cos, sin = jnp.cos(ang), jnp.sin(ang)
    rest = DA_DH - DA_ROT
    n = pos.shape[0]
    cos64 = jnp.concatenate([cos, cos, jnp.ones((n, rest), F32)], axis=-1)
    sin64 = jnp.concatenate([-sin, sin, jnp.zeros((n, rest), F32)], axis=-1)
    return jnp.tile(cos64, (1, LANES // DA_DH)), jnp.tile(sin64, (1, LANES // DA_DH))


def _layer(l, x, xb, pos, tables, W, P, mk, mv, mem_li, past, states, lv, alpha):
    B, L, D = x.shape
    M = B * L
    s_ret, s_gla, s_ssm, conv_buf, st_li = states
    xb2 = xb.reshape(M, D)
    main = _mm(xb2, W["main"][l]).reshape(B, L, -1)
    small = _mm(xb2, W["small"][l])
    ssd = _mm(xb2, W["ssd"][l]).reshape(B, L, -1)
    da_q, da_k, da_v = main[..., 0:512], main[..., 512:1024], main[..., 1024:1536]

    lam_init = 0.8 - 0.6 * math.exp(-0.3 * l)
    lam = (jnp.exp(jnp.sum((P["da_lq1"][l] * P["da_lk1"][l]).astype(F32)))
           - jnp.exp(jnp.sum((P["da_lq2"][l] * P["da_lk2"][l]).astype(F32))) + lam_init)
    sc = jnp.stack([lam, jnp.asarray(1.0 - lam_init, F32)]).astype(F32)
    v_rows = da_v.reshape(B, L, DA_H, DA_DV)
    if past is None:
        qb, kb, vt, k_rot = _da_prep(main, tables[1], DA_DH ** -0.5 * math.log2(math.e))
        k_rows = k_rot.reshape(B, L, DA_H, 2 * DA_DH)
        o_da = _flash_da(qb, kb, vt, sc)
    else:
        q = _rope(da_q.reshape(B, L, DA_H, 2, DA_DH), pos, DA_ROT, ROPE_THETA)
        k = _rope(da_k.reshape(B, L, DA_H, 2, DA_DH), pos, DA_ROT, ROPE_THETA)
        k_rows = k.reshape(B, L, DA_H, 2 * DA_DH)
        qs = q.reshape(B, L, 512) * (DA_DH ** -0.5)
        cache_k, cache_v, page_table, cache_li = past
        q5 = qs.reshape(B, L, DA_H, 2, DA_DH)
        qt = jnp.einsum("bthcd,ce->bchted", q5, jnp.eye(2, dtype=F32)).reshape(B, 2 * DA_H * L, 2 * DA_DH)
        o_da = _decode_da(qt.astype(BF16), k.reshape(B, L * DA_H, 2 * DA_DH), da_v.reshape(B, L * DA_H, DA_DV),
                          cache_k, cache_v, page_table, cache_li, sc)
    o_da = o_da.astype(BF16)

    log_gamma = jnp.log1p(-jnp.exp2(-5.0 - jnp.arange(RT_H, dtype=F32)))
    la_ret = jnp.repeat(log_gamma, RT_DK).reshape(1, RT_H * RT_DK)
    small3 = small.reshape(B, L, LANES)
    o_rt, s_ret_new = _scan_qk(main, MAIN_COLS_RET, tables[0], small3, W["a2"][l], la_ret, s_ret, st_li, lv, "ret")

    o_gl, s_gla_new = _scan_qk(main, MAIN_COLS_GLA, tables[0], small3, W["a2"][l],
                               P["gla_b_a"][l].reshape(1, -1), s_gla, st_li, lv, "gla")

    head_params = jnp.stack([P["ssd_dt_bias"][l], P["ssd_a_log"][l], P["ssd_d"][l]]).astype(F32)
    o_sd, s_ssm_new = _scan_ssd(ssd, small3, P["ssd_conv_w"][l], P["ssd_conv_b"][l].reshape(1, -1), head_params,
                                P["ssd_norm_w"][l].reshape(1, -1), conv_buf, s_ssm, st_li, lv)
    n_real = L if lv is None else lv
    xpad = jnp.concatenate([conv_buf, ssd[:, max(n_real - SSD_CONV + 1, 0):n_real, SSD_DI:]], axis=1)
    conv_new = xpad[:, xpad.shape[1] - (SSD_CONV - 1):]

    branches = [o.reshape(M, BR_W) for o in (o_da, o_rt, o_gl, o_sd)]
    x2d = x.reshape(M, D)
    x1, x1b = _merge_ln(branches, xb2, W["gate"][l], W["w_up"][l], W["w_o"][l], x2d, P["ln1_g"][l],
                        P["ln1_b"][l], alpha)

    cq = _mm(x1b, W["w_cq"][l]).reshape(B, L, -1)
    if past is None:
        oc = _xattn(cq, mk, mv, mem_li).reshape(M, -1)
    else:
        qt = cq.reshape(B, L, MA_H, MA_DH).transpose(0, 2, 1, 3).reshape(B, MA_H * L, MA_DH)
        oc = _xattn_dec(qt.astype(BF16), mk, mv, mem_li).reshape(M, -1).astype(BF16)
    x2, x2b = _mm_res_ln(oc, W["w_co"][l], x1, P["ln2_g"][l], P["ln2_b"][l], alpha)

    hmid = _mm(x2b, W["w_mlp1"][l], act="relu2", out_dtype=BF16)
    x3, x3b = _mm_res_ln(hmid, W["w_mlp2"][l], x2, P["ln3_g"][l], P["ln3_b"][l], alpha)
    return (x3.reshape(B, L, D), x3b.reshape(B, L, D),
            (k_rows, v_rows, s_ret_new, s_gla_new, s_ssm_new, conv_new))


def kernel(x_prompt, x_sample, mem_prompt, cache_diff_k, cache_diff_v, page_table, state_ret, state_gla, state_ssm, state_conv, cache_mem_k, cache_mem_v, w_in, da_lq1, da_lk1, da_lq2, da_lk2, gla_w_a2, gla_b_a, ssd_conv_w, ssd_conv_b, ssd_dt_bias, ssd_a_log, ssd_d, ssd_norm_w, w_up, w_o, ln1_g, ln1_b, w_cq, w_ck, w_cv, w_co, ln2_g, ln2_b, w_mlp1, w_mlp2, ln3_g, ln3_b):
    depth = w_in.shape[0]
    alpha = (2 * depth) ** 0.25
    P = dict(da_lq1=da_lq1, da_lk1=da_lk1, da_lq2=da_lq2, da_lk2=da_lk2, gla_b_a=gla_b_a,
             ssd_conv_w=ssd_conv_w, ssd_conv_b=ssd_conv_b, ssd_dt_bias=ssd_dt_bias, ssd_a_log=ssd_a_log,
             ssd_d=ssd_d, ssd_norm_w=ssd_norm_w, ln1_g=ln1_g, ln1_b=ln1_b, ln2_g=ln2_g, ln2_b=ln2_b,
             ln3_g=ln3_g, ln3_b=ln3_b)
    W = _prep_weights(w_in, gla_w_a2, w_up, w_o, w_cq, w_ck, w_cv, w_co, w_mlp1, w_mlp2)
    zero_li = jnp.zeros((1,), jnp.int32)

    B, L, D = x_prompt.shape
    pos_p = jnp.arange(L, dtype=jnp.int32)
    tab_p = (_ret_tables(pos_p), _da_tables(pos_p))
    memb =mem_prompt.reshape(-1, D).astype(BF16)
    n_mem = mem_prompt.shape[1]
    z_ret = jnp.zeros((1, B, 2, 2 * RT_DK, RT_DV), F32)
    z_ssm = jnp.zeros((1, B, SSD_H, SSD_N, SSD_P), F32)
    z_conv = jnp.zeros((B, SSD_CONV - 1, CONV_CH), F32)
    x, xb = x_prompt, x_prompt.astype(BF16)
    outs_p = [[] for _ in range(8)]
    for l in range(depth):
        mk = _mm(memb, W["w_ck"][l]).reshape(1, B, n_mem, MA_H, MA_DH)
        mv = _mm(memb, W["w_cv"][l]).reshape(1, B, n_mem, MA_H, MA_DH)
        x, xb, (kr, vr, sr, sg, ss, cn) = _layer(
            l, x, xb, pos_p, tab_p, W, P, mk, mv, zero_li, None, (z_ret, z_ret, z_ssm, z_conv, zero_li), None,
            alpha)
        new = (kr, vr, sr.reshape(B, RT_H, RT_DK, RT_DV), sg.reshape(B, GL_H, GL_DK, GL_DV), ss, cn,
               mk[0], mv[0])
        for lst, val in zip(outs_p, new):
            lst.append(val)
    y_prompt = x

    Bd, Ld, _ = x_sample.shape
    t_pad = -(-Ld // SUBLANES) * SUBLANES
    n_pages, page = page_table.shape[1], cache_diff_k.shape[2]
    past_len = n_pages * page
    pos_s = past_len + jnp.arange(t_pad, dtype=jnp.int32)
    tab_s = (_ret_tables(pos_s), None)
    x = jnp.pad(x_sample, ((0, 0), (0, t_pad - Ld), (0, 0)))
    xb = x.astype(BF16)
    n_pool = cache_diff_k.shape[1]
    ck = cache_diff_k.reshape(depth, n_pool, page * DA_H, 2 * DA_DH)
    cv = cache_diff_v.reshape(depth, n_pool, page * DA_H, DA_DV)
    cmk = cache_mem_k.reshape(depth, Bd, -1, MA_DH)
    cmv = cache_mem_v.reshape(depth, Bd, -1, MA_DH)
    s_ret_all = state_ret.reshape(depth, Bd, 2, 2 * RT_DK, RT_DV)
    s_gla_all = state_gla.reshape(depth, Bd, 2, 2 * GL_DK, GL_DV)
    outs_s = [[] for _ in range(6)]
    for l in range(depth):
        li = jnp.full((1,), l, jnp.int32)
        x, xb, (kr, vr, sr, sg, ss, cn) = _layer(
            l, x, xb, pos_s, tab_s, W, P, cmk, cmv, li, (ck, cv, page_table, li),
            (s_ret_all, s_gla_all, state_ssm, state_conv[l], li), Ld, alpha)
        new = (kr[:, :Ld], vr[:, :Ld], sr.reshape(Bd, RT_H, RT_DK, RT_DV), sg.reshape(Bd, GL_H, GL_DK, GL_DV),
               ss, cn)
        for lst, val in zip(outs_s, new):
            lst.append(val)
    y_sample = x[:, :Ld]

    st = lambda lst: jnp.stack(lst)
    return (y_prompt, y_sample) + tuple(st(o) for o in outs_p) + tuple(st(o) for o in outs_s)
```

```python
import functools
import math

import jax
import jax.numpy as jnp
from jax import lax
from jax.experimental import pallas as pl
from jax.experimental.pallas import tpu as pltpu

F32 = jnp.float32
BF16 = jnp.bfloat16

DA_H, DA_DH, DA_DV = 4, 64, 128
DA_ROT = DA_DH // 4
ROPE_THETA = 500000.0
RT_H, RT_DK, RT_DV = 4, 64, 128
RET_THETA = 10000.0
GL_H, GL_DK, GL_DV, GL_RANK = 4, 64, 128, 16
GLA_TEMP = 16.0
SSD_H, SSD_P, SSD_N, SSD_G, SSD_CONV = 8, 64, 128, 2, 4
SSD_DI = SSD_H * SSD_P
CONV_CH = SSD_DI + 2 * SSD_G * SSD_N
MA_H, MA_DH = 4, 128
N_BRANCH, BR_W = 4, 512
EPS = 1e-5
NEG = -1e30

VMEM_LIMIT_BYTES = 56 * 1024 * 1024
LANES = 128
SUBLANES = 8

_NT = (((1,), (1,)), ((), ()))
_TN = (((0,), (0,)), ((), ()))


def _cparams(*sem):
    return pltpu.CompilerParams(dimension_semantics=sem, vmem_limit_bytes=VMEM_LIMIT_BYTES)


def _pick(n, cands):
    for c in cands:
        if n % c == 0:
            return c
    return n


def _split3(x):
    h = x.astype(BF16)
    r = x - h.astype(F32)
    m = r.astype(BF16)
    lo = (r - m.astype(F32)).astype(BF16)
    return h, m, lo


def _dot(a, b, dims=None):
    if dims is None:
        return jnp.dot(a, b, preferred_element_type=F32)
    return lax.dot_general(a, b, dims, preferred_element_type=F32)


def _mm_kernel(x_ref, w_ref, o_ref, *, act):
    acc = _dot(x_ref[...], w_ref[...])
    if act == "relu2":
        acc = jnp.square(jnp.maximum(acc, 0.0))
    o_ref[...] = acc.astype(o_ref.dtype)


def _mm(x, w, act=None, out_dtype=F32):
    M, K = x.shape
    N = w.shape[1]
    tm = _pick(M, (2048, 1024, 512, 256, 128, 64, 32, 16, 8))
    tn = _pick(N, (1024, 768, 512, 384, 256, 128))
    return pl.pallas_call(
        functools.partial(_mm_kernel, act=act),
        out_shape=jax.ShapeDtypeStruct((M, N), out_dtype),
        grid=(N // tn, M // tm),
        in_specs=[pl.BlockSpec((tm, K), lambda j, i: (i, 0)),
                  pl.BlockSpec((K, tn), lambda j, i: (0, j))],
        out_specs=pl.BlockSpec((tm, tn), lambda j, i: (i, j)),
        compiler_params=_cparams("parallel", "parallel"),
        name="mm",
    )(x, w)


def _ln_rows(z, g, b):
    mu = jnp.mean(z, axis=-1, keepdims=True)
    zc = z - mu
    var = jnp.mean(zc * zc, axis=-1, keepdims=True)
    return zc * lax.rsqrt(var + EPS) * g + b


def _mm_res_ln_kernel(x_ref, w_ref, r_ref, g_ref, b_ref, o_ref, ob_ref, *, alpha):
    y = _dot(x_ref[...], w_ref[...])
    out = _ln_rows(alpha * r_ref[...] + y, g_ref[...], b_ref[...])
    o_ref[...] = out
    ob_ref[...] = out.astype(BF16)


def _mm_res_ln(x, w, res, g, b, alpha):
    M, K = x.shape
    N = w.shape[1]
    tm = _pick(M, (512, 256, 128, 64, 32, 16, 8))
    return pl.pallas_call(
        functools.partial(_mm_res_ln_kernel, alpha=alpha),
        out_shape=(jax.ShapeDtypeStruct((M, N), F32), jax.ShapeDtypeStruct((M, N), BF16)),
        grid=(M // tm,),
        in_specs=[pl.BlockSpec((tm, K), lambda i: (i, 0)),
                  pl.BlockSpec((K, N), lambda i: (0, 0)),
                  pl.BlockSpec((tm, N), lambda i: (i, 0)),
                  pl.BlockSpec((1, N), lambda i: (0, 0)),
                  pl.BlockSpec((1, N), lambda i: (0, 0))],
        out_specs=(pl.BlockSpec((tm, N), lambda i: (i, 0)), pl.BlockSpec((tm, N), lambda i: (i, 0))),
        compiler_params=_cparams("parallel"),
        name="mm_res_ln",
    )(x, w, res, g.reshape(1, N), b.reshape(1, N))


def _merge_ln_kernel(b0_ref, b1_ref, b2_ref, b3_ref, xb_ref, wg_ref, wup_ref, wo_ref, r_ref, g_ref, b_ref,
                     o_ref, ob_ref, *, alpha):
    d = wo_ref.shape[0]
    xb = xb_ref[...]
    acc = None
    for n, br_ref in enumerate((b0_ref, b1_ref, b2_ref, b3_ref)):
        up = _dot(br_ref[...], wup_ref[n])
        term = jax.nn.sigmoid(_dot(xb, wg_ref[:, n * d:(n + 1) * d])) * up
        acc = term if acc is None else acc + term
    y = _dot(acc.astype(BF16), wo_ref[...])
    out = _ln_rows(alpha * r_ref[...] + y, g_ref[...], b_ref[...])
    o_ref[...] = out
    ob_ref[...] = out.astype(BF16)


def _merge_ln(branches, xb, w_gate, w_up, w_o, res, g, b, alpha):
    M = xb.shape[0]
    D = w_o.shape[0]
    tm = _pick(M, (256, 128, 64, 32, 16, 8))
    br_spec = pl.BlockSpec((tm, BR_W), lambda i: (i, 0))
    return pl.pallas_call(
        functools.partial(_merge_ln_kernel, alpha=alpha),
        out_shape=(jax.ShapeDtypeStruct((M, D), F32), jax.ShapeDtypeStruct((M, D), BF16)),
        grid=(M // tm,),
        in_specs=[br_spec, br_spec, br_spec, br_spec,
                  pl.BlockSpec((tm, D), lambda i: (i, 0)),
                  pl.BlockSpec((D, N_BRANCH * D), lambda i: (0, 0)),
                  pl.BlockSpec((N_BRANCH, BR_W, D), lambda i: (0, 0, 0)),
                  pl.BlockSpec((D, D), lambda i: (0, 0)),
                  pl.BlockSpec((tm, D), lambda i: (i, 0)),
                  pl.BlockSpec((1, D), lambda i: (0, 0)),
                  pl.BlockSpec((1, D), lambda i: (0, 0))],
        out_specs=(pl.BlockSpec((tm, D), lambda i: (i, 0)), pl.BlockSpec((tm, D), lambda i: (i, 0))),
        compiler_params=_cparams("parallel"),
        name="merge_ln",
    )(*branches, xb, w_gate, w_up, w_o, res, g.reshape(1, D), b.reshape(1, D))


def _xattn_kernel(li_ref, q_ref, k_ref, v_ref, o_ref):
    del li_ref
    scale = MA_DH ** -0.5
    for h in range(MA_H):
        sl = slice(h * MA_DH, (h + 1) * MA_DH)
        q = q_ref[0, :, sl].astype(BF16)
        k = k_ref[0, 0, :, h, :].astype(BF16)
        v = v_ref[0, 0, :, h, :].astype(BF16)
        s = _dot(q, k, _NT) * scale
        m = jnp.max(s, axis=-1, keepdims=True)
        p = jnp.exp(s - m)
        a = p / jnp.sum(p, axis=-1, keepdims=True)
        o_ref[0, :, sl] = _dot(a.astype(BF16), v).astype(o_ref.dtype)


def _xattn(q, mk, mv, li):
    B, L, W = q.shape
    nm = mk.shape[2]
    tq = _pick(L, (512, 256, 128, 64, 32, 16, 8))
    kv_spec = pl.BlockSpec((1, 1, nm, MA_H, MA_DH), lambda b, i, li: (li[0], b, 0, 0, 0))
    return pl.pallas_call(
        _xattn_kernel,
        out_shape=jax.ShapeDtypeStruct((B, L, W), BF16),
        grid_spec=pltpu.PrefetchScalarGridSpec(
            num_scalar_prefetch=1,
            grid=(B, L // tq),
            in_specs=[pl.BlockSpec((1, tq, W), lambda b, i, li: (b, i, 0)), kv_spec, kv_spec],
            out_specs=pl.BlockSpec((1, tq, W), lambda b, i, li: (b, i, 0))),
        compiler_params=_cparams("parallel", "parallel"),
        name="xattn",
    )(li, q, mk, mv)


def _xattn_dec_kernel(li_ref, q_ref, k_ref, v_ref, o_ref, *, t_pad):
    del li_ref
    qt = q_ref[0]
    row_h = lax.broadcasted_iota(jnp.int32, (qt.shape[0], 1), 0) // t_pad
    lane_h = lax.broadcasted_iota(jnp.int32, (1, k_ref.shape[2]), 1) % MA_H
    s = _dot(qt, k_ref[0, 0].astype(BF16), _NT) * (MA_DH ** -0.5)
    s = jnp.where(lane_h == row_h, s, NEG)
    m = jnp.max(s, axis=-1, keepdims=True)
    p = jnp.exp(s - m)
    a = p / jnp.sum(p, axis=-1, keepdims=True)
    o = _dot(a.astype(BF16), v_ref[0, 0].astype(BF16))
    for h in range(MA_H):
        o_ref[0, :, h * MA_DH:(h + 1) * MA_DH] = o[h * t_pad:(h + 1) * t_pad]


def _xattn_dec(qt, mk, mv, li):
    B, nrow, dh = qt.shape
    t_pad = nrow // MA_H
    nk = mk.shape[2]
    kv_spec = pl.BlockSpec((1, 1, nk, dh), lambda b, li: (li[0], b, 0, 0))
    return pl.pallas_call(
        functools.partial(_xattn_dec_kernel, t_pad=t_pad),
        out_shape=jax.ShapeDtypeStruct((B, t_pad, MA_H * dh), F32),
        grid_spec=pltpu.PrefetchScalarGridSpec(
            num_scalar_prefetch=1,
            grid=(B,),
            in_specs=[pl.BlockSpec((1, nrow, dh), lambda b, li: (b, 0, 0)), kv_spec, kv_spec],
            out_specs=pl.BlockSpec((1, t_pad, MA_H * dh), lambda b, li: (b, 0, 0))),
        compiler_params=_cparams("parallel"),
        name="xattn_dec",
    )(li, qt, mk, mv)


def _head_rms(o):
    return o * lax.rsqrt(jnp.mean(o * o, axis=-1, keepdims=True) + EPS)


def _da_prep_kernel(q_ref, k_ref, v_ref, cos_ref, sin_ref, qo_ref, ko_ref, vt_ref, kr_ref, vr_ref, *, q_scale):
    lane = lax.broadcasted_iota(jnp.int32, (1, LANES), 1) % DA_DH
    first = lane < DA_ROT // 2
    cos = cos_ref[...]
    sin = sin_ref[...]

    def rotary(x):
        partner = jnp.where(first, pltpu.roll(x, LANES - DA_ROT // 2, 1), pltpu.roll(x, DA_ROT // 2, 1))
        return x * cos + partner * sin

    for h in range(DA_H):
        sl = slice(h * LANES, (h + 1) * LANES)
        kh = rotary(k_ref[0, :, sl])
        vh = v_ref[0, :, sl]
        kr_ref[0, :, h, :] = kh
        vr_ref[0, :, h, :] = vh
        ko_ref[0, :, sl] = kh.astype(BF16)
        qo_ref[0, :, sl] = (rotary(q_ref[0, :, sl]) * q_scale).astype(BF16)
        vt_ref[0, sl, :] = vh.T.astype(BF16)


def _da_prep(main, tables, q_scale):
    B, L, _ = main.shape
    w = DA_H * LANES
    tl = _pick(L, (512, 256, 128))
    cos, sin = tables

    def col_spec(idx):
        return pl.BlockSpec((1, tl, w), lambda b, t: (b, t, idx))

    tab_spec = pl.BlockSpec((tl, LANES), lambda b, t: (t, 0))
    row_spec = pl.BlockSpec((1, tl, w), lambda b, t: (b, t, 0))
    head_spec = pl.BlockSpec((1, tl, DA_H, LANES), lambda b, t: (b, t, 0, 0))
    head_shape = jax.ShapeDtypeStruct((B, L, DA_H, LANES), F32)
    return pl.pallas_call(
        functools.partial(_da_prep_kernel, q_scale=q_scale),
        out_shape=(jax.ShapeDtypeStruct((B, L, w), BF16), jax.ShapeDtypeStruct((B, L, w), BF16),
                   jax.ShapeDtypeStruct((B, w, L), BF16), head_shape, head_shape),
        grid=(B, L // tl),
        in_specs=[col_spec(0), col_spec(1), col_spec(2), tab_spec, tab_spec],
        out_specs=(row_spec, row_spec, pl.BlockSpec((1, w, tl), lambda b, t: (b, 0, t)), head_spec, head_spec),
        compiler_params=_cparams("parallel", "parallel"),
        name="da_prep",
    )(main, main, main, cos, sin)


def _flash_da_kernel(sc_ref, q_ref, k_ref, vt_ref, o_ref, st0_scr, st1_scr, p0_scr, p1_scr, acc_scr, m_scr, l_scr,
                     *, tq, tk):
    i = pl.program_id(2)
    lam = sc_ref[0]
    out_scale = sc_ref[1]
    q = q_ref[0]
    lo = lax.broadcasted_iota(jnp.int32, (1, LANES), 1) < DA_DH
    zero = jnp.zeros_like(q)
    q12 = jnp.concatenate([jnp.where(lo, q, zero), jnp.where(lo, zero, q)], axis=0)
    col = lax.broadcasted_iota(jnp.int32, (1, 2 * tq), 1)
    qpos = i * tq + jnp.where(col >= tq, col - tq, col)
    last = (i * tq) // tk

    def qk(j):
        start = pl.multiple_of(j * tk, tk)
        return _dot(k_ref[0, pl.ds(start, tk), :], q12, _NT)

    def scores_to(st_ref, j):
        st = qk(j)
        st_ref[...] = st
        return jnp.max(st, axis=0, keepdims=True)

    def softmax_step(st_ref, p_ref, m, l, blk_max, mask=None):
        st = st_ref[...]
        if mask is not None:
            st = jnp.where(mask, st, NEG)
            blk_max = jnp.max(st, axis=0, keepdims=True)
        m_new = jnp.maximum(m, blk_max)
        alpha = jnp.exp2(m - m_new)
        p = jnp.exp2(st - m_new)
        p_ref[...] = p.astype(BF16)
        return m_new, alpha * l + jnp.sum(p, axis=0, keepdims=True), alpha

    def accumulate(j, p_ref, alpha):
        start = pl.multiple_of(j * tk, tk)
        acc_scr[...] = alpha * acc_scr[...] + _dot(vt_ref[0, :, pl.ds(start, tk)], p_ref[...])

    odd = last % 2
    m_scr[...] = jnp.full((1, 2 * tq), NEG, F32)
    l_scr[...] = jnp.zeros((1, 2 * tq), F32)
    acc_scr[...] = jnp.zeros((DA_DV, 2 * tq), F32)
    p1_scr[...] = jnp.zeros((tk, 2 * tq), BF16)

    @pl.when(odd == 1)
    def _():
        mx = scores_to(st0_scr, 0)
        m, l, alpha = softmax_step(st0_scr, p0_scr, m_scr[...], l_scr[...], mx)
        accumulate(0, p0_scr, alpha)
        m_scr[...] = m
        l_scr[...] = l

    mx_first = scores_to(st0_scr, odd)

    def pair(u, carry):
        a_prev, m, l, mx0 = carry
        b0 = odd + 2 * u
        mx1 = scores_to(st1_scr, b0 + 1)
        accumulate(jnp.maximum(b0 - 1, 0), p1_scr, a_prev)
        m, l, a0 = softmax_step(st0_scr, p0_scr, m, l, mx0)
        mx0 = scores_to(st0_scr, b0 + 2)
        accumulate(b0, p0_scr, a0)
        m, l, a1 = softmax_step(st1_scr, p1_scr, m, l, mx1)
        return a1, m, l, mx0

    a_prev, m, l, mx0 = lax.fori_loop(
        0, last // 2, pair, (jnp.ones((1, 2 * tq), F32), m_scr[...], l_scr[...], mx_first))
    accumulate(jnp.maximum(last - 1, 0), p1_scr, a_prev)
    kpos = last * tk + lax.broadcasted_iota(jnp.int32, (tk, 1), 0)
    _, l, alpha = softmax_step(st0_scr, p0_scr, m, l, mx0, mask=kpos <= qpos)
    accumulate(last, p0_scr, alpha)
    o12 = acc_scr[...] / l
    o = (o12[:, :tq] - lam * o12[:, tq:]).T
    o_ref[0] = (_head_rms(o) * out_scale).astype(o_ref.dtype)


def _flash_da(q, k, vt, sc):
    B, L, W = q.shape
    tq = _pick(L, (512, 256, 128, 64, 32, 16, 8))
    tk = _pick(L, (512, 256, 128, 64, 32, 16, 8))
    return pl.pallas_call(
        functools.partial(_flash_da_kernel, tq=tq, tk=tk),
        out_shape=jax.ShapeDtypeStruct((B, L, W), BF16),
        grid=(B, DA_H, L // tq),
        in_specs=[pl.BlockSpec(memory_space=pltpu.SMEM),
                  pl.BlockSpec((1, tq, LANES), lambda b, h, i: (b, i, h)),
                  pl.BlockSpec((1, L, LANES), lambda b, h, i: (b, 0, h)),
                  pl.BlockSpec((1, DA_DV, L), lambda b, h, i: (b, h, 0))],
        out_specs=pl.BlockSpec((1, tq, LANES), lambda b, h, i: (b, i, h)),
        scratch_shapes=[pltpu.VMEM((tk, 2 * tq), F32), pltpu.VMEM((tk, 2 * tq), F32),
                        pltpu.VMEM((tk, 2 * tq), BF16), pltpu.VMEM((tk, 2 * tq), BF16),
                        pltpu.VMEM((DA_DV, 2 * tq), F32), pltpu.VMEM((1, 2 * tq), F32),
                        pltpu.VMEM((1, 2 * tq), F32)],
        compiler_params=_cparams("parallel", "parallel", "arbitrary"),
        name="flash_da",
    )(sc, q, k, vt)


def _decode_da_kernel(pt_ref, li_ref, sc_ref, qb_ref, kn_ref, vn_ref, *refs, n_pages, t_pad):
    del pt_ref, li_ref
    k_refs = refs[:n_pages]
    v_refs = refs[n_pages:2 * n_pages]
    o_ref = refs[2 * n_pages]
    lam = sc_ref[0]
    out_scale = sc_ref[1]
    qt = qb_ref[0]
    nrow = qt.shape[0]
    row = lax.broadcasted_iota(jnp.int32, (nrow, 1), 0)
    row_h = (row // t_pad) % DA_H
    row_t = row % t_pad

    def lane_ids(n):
        lane = lax.broadcasted_iota(jnp.int32, (1, n), 1)
        return lane // DA_H, lane % DA_H

    _, lane_h = lane_ids(k_refs[0].shape[2])
    valid = lane_h == row_h
    s_pages = [jnp.where(valid, _dot(qt, k_refs[p][0, 0].astype(BF16), _NT), NEG) for p in range(n_pages)]
    new_t, new_h = lane_ids(kn_ref.shape[1])
    valid_new = jnp.logical_and(new_h == row_h, new_t <= row_t)
    s_new = jnp.where(valid_new, _dot(qt, kn_ref[0].astype(BF16), _NT), NEG)

    m = jnp.max(s_new, axis=-1, keepdims=True)
    for s in s_pages:
        m = jnp.maximum(m, jnp.max(s, axis=-1, keepdims=True))

    p_new = jnp.exp(s_new - m)
    den = jnp.sum(p_new, axis=-1, keepdims=True)
    acc = _dot(p_new, vn_ref[0])
    for p in range(n_pages):
        pp = jnp.exp(s_pages[p] - m)
        den = den + jnp.sum(pp, axis=-1, keepdims=True)
        acc = acc + _dot(pp.astype(BF16), v_refs[p][0, 0].astype(BF16))
    o12 = acc / den
    half = DA_H * t_pad
    d = o12[:half] - lam * o12[half:]
    for h in range(DA_H):
        o_ref[0, :, h * DA_DV:(h + 1) * DA_DV] = _head_rms(d[h * t_pad:(h + 1) * t_pad]) * out_scale


def _decode_da(qt, k_new, v_new, cache_k, cache_v, page_table, li, sc):
    bd, nrow, _ = qt.shape
    t_pad = k_new.shape[1] // DA_H
    w = DA_H * DA_DV
    n_pages = page_table.shape[1]
    prow = cache_k.shape[2]

    def page_spec(p):
        return pl.BlockSpec((1, 1, prow, DA_DV), lambda b, pt, li: (li[0], pt[b, p], 0, 0))

    in_specs = [pl.BlockSpec(memory_space=pltpu.SMEM),
                pl.BlockSpec((1, nrow, LANES), lambda b, pt, li: (b, 0, 0)),
                pl.BlockSpec((1, t_pad * DA_H, DA_DV), lambda b, pt, li: (b, 0, 0)),
                pl.BlockSpec((1, t_pad * DA_H, DA_DV), lambda b, pt, li: (b, 0, 0))]
    in_specs += [page_spec(p) for p in range(n_pages)] * 2
    return pl.pallas_call(
        functools.partial(_decode_da_kernel, n_pages=n_pages, t_pad=t_pad),
        out_shape=jax.ShapeDtypeStruct((bd, t_pad, w), F32),
        grid_spec=pltpu.PrefetchScalarGridSpec(
            num_scalar_prefetch=2,
            grid=(bd,),
            in_specs=in_specs,
            out_specs=pl.BlockSpec((1, t_pad, w), lambda b, pt, li: (b, 0, 0))),
        compiler_params=_cparams("parallel"),
        name="decode_da",
    )(page_table, li, sc, qt, k_new, v_new, *([cache_k] * n_pages), *([cache_v] * n_pages))


def _cumsum_rows(x, tri):
    h, m, lo = _split3(x)
    return _dot(tri, h) + _dot(tri, m) + _dot(tri, lo)


def _scan_qk_kernel(li_ref, q_ref, k_ref, v_ref, g_ref, cos_ref, sin_ref, lr_ref, a2_ref, ba_ref, s0_ref,
                    o_ref, sn_ref, s_scr, *, c, nch, bb, lv, mode):
    del li_ref
    t = pl.program_id(1)

    @pl.when(t == 0)
    def _():
        s_scr[...] = s0_ref[0]

    mxu = BF16 if c >= 16 else F32
    row = lax.broadcasted_iota(jnp.int32, (c, 1), 0)
    colc = lax.broadcasted_iota(jnp.int32, (1, c), 1)
    causal = row >= colc
    tri = jnp.where(causal, 1.0, 0.0).astype(BF16)
    lo = lax.broadcasted_iota(jnp.int32, (1, LANES), 1) < RT_DK
    wqk = RT_H * RT_DK
    half = RT_DK // 2
    first_half = lax.broadcasted_iota(jnp.int32, (1, wqk), 1) % RT_DK < half
    mid = c // 2 - 1

    def rotary(x, cos, sin):
        partner = jnp.where(first_half, pltpu.roll(x, wqk - half, 1), pltpu.roll(x, half, 1))
        return x * cos + partner * sin

    bodies = [(bi, ci) for ci in range(nch) for bi in range(bb)]
    head_masks = (lo, jnp.logical_not(lo))

    def load(bi, ci):
        rows = slice(ci * c, (ci + 1) * c)
        q = q_ref[bi, rows, :]
        k = k_ref[bi, rows, :]
        if mode == "ret":
            cos = cos_ref[rows, :]
            sin = sin_ref[rows, :]
            q = rotary(q, cos, sin)
            k = rotary(k, cos, sin)
            la = jnp.broadcast_to(ba_ref[...], (c, wqk))
        else:
            x = _dot(lr_ref[bi, rows, :].astype(BF16), a2_ref[...]) + ba_ref[...]
            la = jax.nn.log_sigmoid(x) * (1.0 / GLA_TEMP)
        k = k * (RT_DK ** -0.5)
        if lv < c:
            valid = row < lv
            la = jnp.where(valid, la, 0.0)
            k = jnp.where(valid, k, 0.0)
        return dict(q=q, k=k, la=la, v=v_ref[bi, rows, :].astype(mxu))

    st = {bd: load(*bd) for bd in bodies}
    for bd in bodies:
        st[bd]["b"] = _cumsum_rows(st[bd]["la"], tri)
    for bd in bodies:
        d = st[bd]
        b, q, k = d["b"], d["q"], d["k"]
        bm = b[mid:mid + 1, :]
        bl = b[c - 1:c, :]
        d["qs"] = q * jnp.exp(b - bm)
        d["ks"] = (k * jnp.exp(bm - b)).astype(mxu)
        d["kh"] = (k * jnp.exp(bl - b)).astype(mxu)
        d["qe"] = q * jnp.exp(b)
    for bd in bodies:
        d = st[bd]
        d["s"] = [jnp.where(causal, _dot(jnp.where(head_masks[h % 2], d["qs"][:, (h // 2) * LANES:(h // 2 + 1) * LANES],
                                                     0.0).astype(mxu),
                                         d["ks"][:, (h // 2) * LANES:(h // 2 + 1) * LANES], _NT), 0.0).astype(mxu)
                  for h in range(RT_H)]
        d["upd"] = [_dot(d["kh"][:, p * LANES:(p + 1) * LANES], d["v"][:, 2 * p * RT_DV:(2 * p + 2) * RT_DV], _TN)
                    for p in range(2)]
        d["dcol"] = [jnp.exp(d["b"][:, p * LANES:(p + 1) * LANES].T[:, c - 1:c]) for p in range(2)]
    for bd in bodies:
        d = st[bd]
        d["o"] = [_dot(d["s"][h], d["v"][:, h * RT_DV:(h + 1) * RT_DV]) for h in range(RT_H)]
    for bi, ci in bodies:
        d = st[(bi, ci)]
        rows = slice(ci * c, (ci + 1) * c)
        g = g_ref[bi, rows, :]
        for p in range(2):
            S = s_scr[bi, p]
            Sm = S.astype(mxu)
            qep = d["qe"][:, p * LANES:(p + 1) * LANES]
            for hh in range(2):
                h = 2 * p + hh
                hs = slice(h * RT_DV, (h + 1) * RT_DV)
                o = d["o"][h] + _dot(jnp.where(head_masks[hh], qep, 0.0).astype(mxu), Sm)
                gh = g[:, hs]
                o_ref[bi, rows, hs] = (_head_rms(o) * (gh * jax.nn.sigmoid(gh))).astype(o_ref.dtype)
            upd = d["upd"][p]
            s_scr[bi, p] = d["dcol"][p] * S + jnp.concatenate(
                [upd[0:RT_DK, 0:RT_DV], upd[RT_DK:2 * RT_DK, RT_DV:2 * RT_DV]], axis=0)

    @pl.when(t == pl.num_programs(1) - 1)
    def _():
        sn_ref[...] = s_scr[...]


def _scan_qk(main, cols, tables, lr, a2, ba, s0, li, lv, mode):
    B, L, _ = main.shape
    c = _pick(L, (128, 64, 32, 16, 8))
    tl = _pick(L, (512, 256, 128, 64, 32, 16, 8))
    bb = _pick(B, (2, 1)) if L > SUBLANES else _pick(B, (8, 4, 2, 1))
    wqk = RT_H * RT_DK
    wv = RT_H * RT_DV
    if lv is None:
        lv = c
    qi, ki, vi, gi = cols
    cos, sin = tables

    def col_spec(width, idx):
        return pl.BlockSpec((bb, tl, width), lambda b, t, li: (b, t, idx))

    tab_spec = pl.BlockSpec((tl, wqk), lambda b, t, li: (t, 0))
    return pl.pallas_call(
        functools.partial(_scan_qk_kernel, c=c, nch=tl // c, bb=bb, lv=lv, mode=mode),
        out_shape=(jax.ShapeDtypeStruct((B, L, wv), BF16),
                   jax.ShapeDtypeStruct((B, 2, 2 * RT_DK, RT_DV), F32)),
        grid_spec=pltpu.PrefetchScalarGridSpec(
            num_scalar_prefetch=1,
            grid=(B // bb, L // tl),
            in_specs=[col_spec(wqk, qi), col_spec(wqk, ki), col_spec(wv, vi), col_spec(wv, gi),
                      tab_spec, tab_spec,
                      pl.BlockSpec((bb, tl, LANES), lambda b, t, li: (b, t, 0)),
                      pl.BlockSpec((LANES, wqk), lambda b, t, li: (0, 0)),
                      pl.BlockSpec((1, wqk), lambda b, t, li: (0, 0)),
                      pl.BlockSpec((1, bb, 2, 2 * RT_DK, RT_DV), lambda b, t, li: (li[0], b, 0, 0, 0))],
            out_specs=(pl.BlockSpec((bb, tl, wv), lambda b, t, li: (b, t, 0)),
                       pl.BlockSpec((bb, 2, 2 * RT_DK, RT_DV), lambda b, t, li: (b, 0, 0, 0))),
            scratch_shapes=[pltpu.VMEM((bb, 2, 2 * RT_DK, RT_DV), F32)]),
        compiler_params=_cparams("parallel", "arbitrary"),
        name="scan_qk_" + mode,
    )(li, main, main, main, main, cos, sin, lr, a2, ba, s0)


def _scan_ssd_kernel(li_ref, z_ref, x_ref, bc_ref, lr_ref, cw_ref, cb_ref, hp_ref, nw_ref, cbuf_ref, s0_ref,
                     o_ref, sn_ref, s_scr, prev_scr, y_scr, *, c, nch, bb, lv):
    del li_ref
    t = pl.program_id(1)
    nprev = prev_scr.shape[1]
    nconv = cbuf_ref.shape[1]

    @pl.when(t == 0)
    def _():
        s_scr[...] = s0_ref[0]
        prev_scr[:, :, :] = jnp.zeros(prev_scr.shape, F32)
        prev_scr[:, nprev - nconv:, :] = cbuf_ref[...]

    mxu = BF16 if c >= 16 else F32
    row = lax.broadcasted_iota(jnp.int32, (c, 1), 0)
    colc = lax.broadcasted_iota(jnp.int32, (1, c), 1)
    causal = row >= colc
    tri = jnp.where(causal, 1.0, 0.0).astype(BF16)
    tri_t = jnp.where(row <= colc, 1.0, 0.0).astype(BF16)
    hpg = SSD_H // SSD_G
    gw = SSD_DI // SSD_G
    dt_bias = hp_ref[0:1, :]
    a_head = -jnp.exp(hp_ref[1:2, :])
    d_head = hp_ref[2:3, :]

    def conv_silu(ref, bi, ci, lo_ch):
        w = ref.shape[2]
        if ci == 0:
            ext = jnp.concatenate([prev_scr[bi, :, lo_ch:lo_ch + w], ref[bi, 0:c, :]], axis=0)
            taps = [ext[nprev - nconv + j:nprev - nconv + j + c] for j in range(nconv + 1)]
        else:
            r0 = ci * c
            taps = [ref[bi, r0 - nconv + j:r0 - nconv + j + c, :] for j in range(nconv + 1)]
        acc = cb_ref[0:1, lo_ch:lo_ch + w]
        for j, tap in enumerate(taps):
            acc = acc + tap * cw_ref[j:j + 1, lo_ch:lo_ch + w]
        return acc * jax.nn.sigmoid(acc)

    bodies = [(bi, ci) for ci in range(nch) for bi in range(bb)]

    def load(bi, ci):
        rows = slice(ci * c, (ci + 1) * c)
        xs = conv_silu(x_ref, bi, ci, 0)
        bcm = conv_silu(bc_ref, bi, ci, SSD_DI)
        bm = bcm[:, :SSD_G * SSD_N]
        dt = jax.nn.softplus(lr_ref[bi, rows, GL_RANK:GL_RANK + SSD_H] + dt_bias)
        a = dt * a_head
        if lv < c:
            valid = row < lv
            a = jnp.where(valid, a, 0.0)
            bm = jnp.where(valid, bm, 0.0)
        return dict(xs=xs, bm=bm, cm=bcm[:, SSD_G * SSD_N:].astype(mxu), dt=dt, a=a)

    st = {bd: load(*bd) for bd in bodies}
    for bd in bodies:
        d = st[bd]
        ah, am, al = _split3(d["a"])
        d["bcol"] = _dot(tri, ah) + _dot(tri, am) + _dot(tri, al)
        d["brow"] = _dot(ah, tri_t, _TN) + _dot(am, tri_t, _TN) + _dot(al, tri_t, _TN)
        d["gmat"] = [_dot(d["cm"][:, g * SSD_N:(g + 1) * SSD_N],
                          d["bm"][:, g * SSD_N:(g + 1) * SSD_N].astype(mxu), _NT) for g in range(SSD_G)]
    for bd in bodies:
        d = st[bd]
        d["xh"], d["y"], d["upd"] = [], [], []
        for h in range(SSD_H):
            g = h // hpg
            bc = d["bcol"][:, h:h + 1]
            w = jnp.exp(jnp.where(causal, bc - d["brow"][h:h + 1, :], NEG))
            xh = (d["xs"][:, h * SSD_P:(h + 1) * SSD_P] * d["dt"][:, h:h + 1]).astype(mxu)
            d["y"].append(_dot((d["gmat"][g] * w).astype(mxu), xh))
            bg = d["bm"][:, g * SSD_N:(g + 1) * SSD_N]
            d["upd"].append(_dot((bg * jnp.exp(bc[c - 1:c, :] - bc)).astype(mxu), xh, _TN))
    for bi, ci in bodies:
        d = st[(bi, ci)]
        slot = bi * nch + ci
        for h in range(SSD_H):
            g = h // hpg
            hs = slice(h * SSD_P, (h + 1) * SSD_P)
            bc = d["bcol"][:, h:h + 1]
            S = s_scr[bi, h]
            y = d["y"][h] + jnp.exp(bc) * _dot(d["cm"][:, g * SSD_N:(g + 1) * SSD_N], S.astype(mxu))
            y_scr[slot, :, hs] = y + d_head[:, h:h + 1] * d["xs"][:, hs]
            s_scr[bi, h] = jnp.exp(bc[c - 1:c, :]) * S + d["upd"][h]
    for bi, ci in bodies:
        rows = slice(ci * c, (ci + 1) * c)
        z = z_ref[bi, rows, :]
        gated = y_scr[bi * nch + ci] * (z * jax.nn.sigmoid(z))
        for g in range(SSD_G):
            gs = slice(g * gw, (g + 1) * gw)
            o_ref[bi, rows, gs] = (_head_rms(gated[:, gs]) * nw_ref[0:1, gs]).astype(o_ref.dtype)

    for bi in range(bb):
        prev_scr[bi, :, 0:SSD_DI] = x_ref[bi, x_ref.shape[1] - nprev:, :]
        prev_scr[bi, :, SSD_DI:] = bc_ref[bi, bc_ref.shape[1] - nprev:, :]

    @pl.when(t == pl.num_programs(1) - 1)
    def _():
        sn_ref[...] = s_scr[...]


def _scan_ssd(ssd, lr, conv_w, conv_b, head_params, norm_w, conv_buf, s0, li, lv):
    B, L, _ = ssd.shape
    c = _pick(L, (128, 64, 32, 16, 8))
    tl = _pick(L, (256, 128, 64, 32, 16, 8))
    bb = _pick(B, (2, 1)) if L > SUBLANES else _pick(B, (8, 4, 2, 1))
    nch = tl // c
    if lv is None:
        lv = c

    def col_spec(idx):
        return pl.BlockSpec((bb, tl, SSD_DI), lambda b, t, li: (b, t, idx))

    def full(a):
        return pl.BlockSpec(a.shape, lambda b, t, li: (0,) * a.ndim)

    return pl.pallas_call(
        functools.partial(_scan_ssd_kernel, c=c, nch=nch, bb=bb, lv=lv),
        out_shape=(jax.ShapeDtypeStruct((B, L, SSD_DI), BF16),
                   jax.ShapeDtypeStruct((B, SSD_H, SSD_N, SSD_P), F32)),
        grid_spec=pltpu.PrefetchScalarGridSpec(
            num_scalar_prefetch=1,
            grid=(B // bb, L // tl),
            in_specs=[col_spec(0), col_spec(1), col_spec(2),
                      pl.BlockSpec((bb, tl, LANES), lambda b, t, li: (b, t, 0)),
                      full(conv_w), full(conv_b), full(head_params), full(norm_w),
                      pl.BlockSpec((bb, SSD_CONV - 1, CONV_CH), lambda b, t, li: (b, 0, 0)),
                      pl.BlockSpec((1, bb, SSD_H, SSD_N, SSD_P), lambda b, t, li: (li[0], b, 0, 0, 0))],
            out_specs=(pl.BlockSpec((bb, tl, SSD_DI), lambda b, t, li: (b, t, 0)),
                       pl.BlockSpec((bb, SSD_H, SSD_N, SSD_P), lambda b, t, li: (b, 0, 0, 0))),
            scratch_shapes=[pltpu.VMEM((bb, SSD_H, SSD_N, SSD_P), F32),
                            pltpu.VMEM((bb, SUBLANES, CONV_CH), F32),
                            pltpu.VMEM((bb * nch, c, SSD_DI), F32)]),
        compiler_params=_cparams("parallel", "arbitrary"),
        name="scan_ssd",
    )(li, ssd, ssd, ssd, lr, conv_w, conv_b, head_params, norm_w, conv_buf, s0)


def _rope(x, pos, rot_dim, theta):
    half = rot_dim // 2
    inv = theta ** (-jnp.arange(half, dtype=F32) / half)
    ang = pos.astype(F32)[:, None] * inv[None, :]
    shape = (1, ang.shape[0]) + (1,) * (x.ndim - 3) + (half,)
    cos = jnp.cos(ang).reshape(shape)
    sin = jnp.sin(ang).reshape(shape)
    x1, x2, rest = x[..., :half], x[..., half:rot_dim], x[..., rot_dim:]
    return jnp.concatenate([x1 * cos - x2 * sin, x1 * sin + x2 * cos, rest], axis=-1)


def _prep_weights(w_in, gla_w_a2, w_up, w_o, w_cq, w_ck, w_cv, w_co, w_mlp1, w_mlp2):
    offs = {}
    o = 0
    names = ("da_q", "da_k", "da_v", "rt_q", "rt_k", "rt_v", "rt_g", "gl_q", "gl_k", "gl_v", "gl_r",
             "gl_lr", "sd_z", "sd_xbc", "sd_dt", "gate")
    d_model = w_in.shape[1]
    sizes = (512, 512, 512, 256, 256, 512, 512, 256, 256, 512, 512, GL_RANK, SSD_DI, CONV_CH, SSD_H,
             N_BRANCH * d_model)
    for n, s in zip(names, sizes):
        offs[n] = (o, o + s)
        o += s
    depth = w_in.shape[0]
    main = w_in[:, :, offs["da_q"][0]:offs["gl_r"][1]]
    small = jnp.concatenate(
        [w_in[:, :, offs["gl_lr"][0]:offs["gl_lr"][1]], w_in[:, :, offs["sd_dt"][0]:offs["sd_dt"][1]],
         jnp.zeros((depth, d_model, LANES - GL_RANK - SSD_H), w_in.dtype)], axis=-1)
    ssd = w_in[:, :, offs["sd_z"][0]:offs["sd_xbc"][1]]
    gate = w_in[:, :, offs["gate"][0]:offs["gate"][1]]
    a2 = jnp.concatenate(
        [gla_w_a2, jnp.zeros((depth, LANES - GL_RANK, gla_w_a2.shape[2]), gla_w_a2.dtype)], axis=1)
    c = lambda w: w.astype(BF16)
    return dict(main=c(main), small=c(small), ssd=c(ssd), gate=c(gate), a2=c(a2), w_up=c(w_up), w_o=c(w_o),
                w_cq=c(w_cq), w_ck=c(w_ck), w_cv=c(w_cv), w_co=c(w_co), w_mlp1=c(w_mlp1), w_mlp2=c(w_mlp2))


MAIN_COLS_RET = (6, 7, 4, 5)
MAIN_COLS_GLA = (12, 13, 7, 8)


def _ret_tables(pos):
    half = RT_DK // 2
    inv = RET_THETA ** (-jnp.arange(half, dtype=F32) / half)
    ang = pos.astype(F32)[:, None] * inv[None, :]
    cos, sin = jnp.cos(ang), jnp.sin(ang)
    return (jnp.tile(jnp.concatenate([cos, cos], axis=-1), (1, RT_H)),
            jnp.tile(jnp.concatenate([-sin, sin], axis=-1), (1, RT_H)))


def _da_tables(pos):
    half = DA_ROT // 2
    inv = ROPE_THETA ** (-jnp.arange(half, dtype=F32) / half)
    ang = pos.astype(F32)[:, None] * inv[None, :]
    cos, sin = jnp.cos(ang), jnp.sin(ang)
    rest = DA_DH - DA_ROT
    n = pos.shape[0]
    cos64 = jnp.concatenate([cos, cos, jnp.ones((n, rest), F32)], axis=-1)
    sin64 = jnp.concatenate([-sin, sin, jnp.zeros((n, rest), F32)], axis=-1)
    return jnp.tile(cos64, (1, LANES // DA_DH)), jnp.tile(sin64, (1, LANES // DA_DH))


def _layer(l, x, xb, pos, tables, W, P, mk, mv, mem_li, past, states, lv, alpha):
    B, L, D = x.shape
    M = B * L
    s_ret, s_gla, s_ssm, conv_buf, st_li = states
    xb2 = xb.reshape(M, D)
    main = _mm(xb2, W["main"][l]).reshape(B, L, -1)
    small = _mm(xb2, W["small"][l])
    ssd = _mm(xb2, W["ssd"][l]).reshape(B, L, -1)
    da_q, da_k, da_v = main[..., 0:512], main[..., 512:1024], main[..., 1024:1536]

    lam_init = 0.8 - 0.6 * math.exp(-0.3 * l)
    lam = (jnp.exp(jnp.sum((P["da_lq1"][l] * P["da_lk1"][l]).astype(F32)))
           - jnp.exp(jnp.sum((P["da_lq2"][l] * P["da_lk2"][l]).astype(F32))) + lam_init)
    sc = jnp.stack([lam, jnp.asarray(1.0 - lam_init, F32)]).astype(F32)
    if past is None:
        qb, kb, vt, k_rows, v_rows = _da_prep(main, tables[1], DA_DH ** -0.5 * math.log2(math.e))
        o_da = _flash_da(qb, kb, vt, sc)
    else:
        v_rows = da_v.reshape(B, L, DA_H, DA_DV)
        q = _rope(da_q.reshape(B, L, DA_H, 2, DA_DH), pos, DA_ROT, ROPE_THETA)
        k = _rope(da_k.reshape(B, L, DA_H, 2, DA_DH), pos, DA_ROT, ROPE_THETA)
        k_rows = k.reshape(B, L, DA_H, 2 * DA_DH)
        qs = q.reshape(B, L, 512) * (DA_DH ** -0.5)
        cache_k, cache_v, page_table, cache_li = past
        q5 = qs.reshape(B, L, DA_H, 2, DA_DH)
        qt = jnp.einsum("bthcd,ce->bchted", q5, jnp.eye(2, dtype=F32)).reshape(B, 2 * DA_H * L, 2 * DA_DH)
        o_da = _decode_da(qt.astype(BF16), k.reshape(B, L * DA_H, 2 * DA_DH), da_v.reshape(B, L * DA_H, DA_DV),
                          cache_k, cache_v, page_table, cache_li, sc)
    o_da = o_da.astype(BF16)

    log_gamma = jnp.log1p(-jnp.exp2(-5.0 - jnp.arange(RT_H, dtype=F32)))
    la_ret = jnp.repeat(log_gamma, RT_DK).reshape(1, RT_H * RT_DK)
    small3 = small.reshape(B, L, LANES)
    o_rt, s_ret_new = _scan_qk(main, MAIN_COLS_RET, tables[0], small3, W["a2"][l], la_ret, s_ret, st_li, lv, "ret")

    o_gl, s_gla_new = _scan_qk(main, MAIN_COLS_GLA, tables[0], small3, W["a2"][l],
                               P["gla_b_a"][l].reshape(1, -1), s_gla, st_li, lv, "gla")

    head_params = jnp.stack([P["ssd_dt_bias"][l], P["ssd_a_log"][l], P["ssd_d"][l]]).astype(F32)
    o_sd, s_ssm_new = _scan_ssd(ssd, small3, P["ssd_conv_w"][l], P["ssd_conv_b"][l].reshape(1, -1), head_params,
                                P["ssd_norm_w"][l].reshape(1, -1), conv_buf, s_ssm, st_li, lv)
    n_real = L if lv is None else lv
    xpad = jnp.concatenate([conv_buf, ssd[:, max(n_real - SSD_CONV + 1, 0):n_real, SSD_DI:]], axis=1)
    conv_new = xpad[:, xpad.shape[1] - (SSD_CONV - 1):]

    branches = [o.reshape(M, BR_W) for o in (o_da, o_rt, o_gl, o_sd)]
    x2d = x.reshape(M, D)
    x1, x1b = _merge_ln(branches, xb2, W["gate"][l], W["w_up"][l], W["w_o"][l], x2d, P["ln1_g"][l],
                        P["ln1_b"][l], alpha)

    cq = _mm(x1b, W["w_cq"][l]).reshape(B, L, -1)
    if past is None:
        oc = _xattn(cq, mk, mv, mem_li).reshape(M, -1)
    else:
        qt = cq.reshape(B, L, MA_H, MA_DH).transpose(0, 2, 1, 3).reshape(B, MA_H * L, MA_DH)
        oc = _xattn_dec(qt.astype(BF16), mk, mv, mem_li).reshape(M, -1).astype(BF16)
    x2, x2b = _mm_res_ln(oc, W["w_co"][l], x1, P["ln2_g"][l], P["ln2_b"][l], alpha)

    hmid = _mm(x2b, W["w_mlp1"][l], act="relu2", out_dtype=BF16)
    x3, x3b = _mm_res_ln(hmid, W["w_mlp2"][l], x2, P["ln3_g"][l], P["ln3_b"][l], alpha)
    return (x3.reshape(B, L, D), x3b.reshape(B, L, D),
            (k_rows, v_rows, s_ret_new, s_gla_new, s_ssm_new, conv_new))


def kernel(x_prompt, x_sample, mem_prompt, cache_diff_k, cache_diff_v, page_table, state_ret, state_gla, state_ssm, state_conv, cache_mem_k, cache_mem_v, w_in, da_lq1, da_lk1, da_lq2, da_lk2, gla_w_a2, gla_b_a, ssd_conv_w, ssd_conv_b, ssd_dt_bias, ssd_a_log, ssd_d, ssd_norm_w, w_up, w_o, ln1_g, ln1_b, w_cq, w_ck, w_cv, w_co, ln2_g, ln2_b, w_mlp1, w_mlp2, ln3_g, ln3_b):
    depth = w_in.shape[0]
    alpha = (2 * depth) ** 0.25
    P = dict(da_lq1=da_lq1, da_lk1=da_lk1, da_lq2=da_lq2, da_lk2=da_lk2, gla_b_a=gla_b_a,
             ssd_conv_w=ssd_conv_w, ssd_conv_b=ssd_conv_b, ssd_dt_bias=ssd_dt_bias, ssd_a_log=ssd_a_log,
             ssd_d=ssd_d, ssd_norm_w=ssd_norm_w, ln1_g=ln1_g, ln1_b=ln1_b, ln2_g=ln2_g, ln2_b=ln2_b,
             ln3_g=ln3_g, ln3_b=ln3_b)
    W = _prep_weights(w_in, gla_w_a2, w_up, w_o, w_cq, w_ck, w_cv, w_co, w_mlp1, w_mlp2)
    zero_li = jnp.zeros((1,), jnp.int32)

    B, L, D = x_prompt.shape
    pos_p = jnp.arange(L, dtype=jnp.int32)
    tab_p = (_ret_tables(pos_p), _da_tables(pos_p))
    memb =mem_prompt.reshape(-1, D).astype(BF16)
    n_mem = mem_prompt.shape[1]
    z_ret = jnp.zeros((1, B, 2, 2 * RT_DK, RT_DV), F32)
    z_ssm = jnp.zeros((1, B, SSD_H, SSD_N, SSD_P), F32)
    z_conv = jnp.zeros((B, SSD_CONV - 1, CONV_CH), F32)
    x, xb = x_prompt, x_prompt.astype(BF16)
    outs_p = [[] for _ in range(8)]
    for l in range(depth):
        mk = _mm(memb, W["w_ck"][l]).reshape(1, B, n_mem, MA_H, MA_DH)
        mv = _mm(memb, W["w_cv"][l]).reshape(1, B, n_mem, MA_H, MA_DH)
        x, xb, (kr, vr, sr, sg, ss, cn) = _layer(
            l, x, xb, pos_p, tab_p, W, P, mk, mv, zero_li, None, (z_ret, z_ret, z_ssm, z_conv, zero_li), None,
            alpha)
        new = (kr, vr, sr.reshape(B, RT_H, RT_DK, RT_DV), sg.reshape(B, GL_H, GL_DK, GL_DV), ss, cn,
               mk[0], mv[0])
        for lst, val in zip(outs_p, new):
            lst.append(val)
    y_prompt = x

    Bd, Ld, _ = x_sample.shape
    t_pad = -(-Ld // SUBLANES) * SUBLANES
    n_pages, page = page_table.shape[1], cache_diff_k.shape[2]
    past_len = n_pages * page
    pos_s = past_len + jnp.arange(t_pad, dtype=jnp.int32)
    tab_s = (_ret_tables(pos_s), None)
    x = jnp.pad(x_sample, ((0, 0), (0, t_pad - Ld), (0, 0)))
    xb = x.astype(BF16)
    n_pool = cache_diff_k.shape[1]
    ck = cache_diff_k.reshape(depth, n_pool, page * DA_H, 2 * DA_DH)
    cv = cache_diff_v.reshape(depth, n_pool, page * DA_H, DA_DV)
    cmk = cache_mem_k.reshape(depth, Bd, -1, MA_DH)
    cmv = cache_mem_v.reshape(depth, Bd, -1, MA_DH)
    s_ret_all = state_ret.reshape(depth, Bd, 2, 2 * RT_DK, RT_DV)
    s_gla_all = state_gla.reshape(depth, Bd, 2, 2 * GL_DK, GL_DV)
    outs_s = [[] for _ in range(6)]
    for l in range(depth):
        li = jnp.full((1,), l, jnp.int32)
        x, xb, (kr, vr, sr, sg, ss, cn) = _layer(
            l, x, xb, pos_s, tab_s, W, P, cmk, cmv, li, (ck, cv, page_table, li),
            (s_ret_all, s_gla_all, state_ssm, state_conv[l], li), Ld, alpha)
        new = (kr[:, :Ld], vr[:, :Ld], sr.reshape(Bd, RT_H, RT_DK, RT_DV), sg.reshape(Bd, GL_H, GL_DK, GL_DV),
               ss, cn)
        for lst, val in zip(outs_s, new):
            lst.append(val)
    y_sample = x[:, :Ld]

    st = lambda lst: jnp.stack(lst)
    return (y_prompt, y_sample) + tuple(st(o) for o in outs_p) + tuple(st(o) for o in outs_s)
```

```python
import functools
import math

import jax
import jax.numpy as jnp
from jax import lax
from jax.experimental import pallas as pl
from jax.experimental.pallas import tpu as pltpu

F32 = jnp.float32
BF16 = jnp.bfloat16

DA_H, DA_DH, DA_DV = 4, 64, 128
DA_ROT = DA_DH // 4
ROPE_THETA = 500000.0
RT_H, RT_DK, RT_DV = 4, 64, 128
RET_THETA = 10000.0
GL_H, GL_DK, GL_DV, GL_RANK = 4, 64, 128, 16
GLA_TEMP = 16.0
SSD_H, SSD_P, SSD_N, SSD_G, SSD_CONV = 8, 64, 128, 2, 4
SSD_DI = SSD_H * SSD_P
CONV_CH = SSD_DI + 2 * SSD_G * SSD_N
MA_H, MA_DH = 4, 128
N_BRANCH, BR_W = 4, 512
EPS = 1e-5
NEG = -1e30

VMEM_LIMIT_BYTES = 56 * 1024 * 1024
LANES = 128
SUBLANES = 8

_NT = (((1,), (1,)), ((), ()))
_TN = (((0,), (0,)), ((), ()))


def _cparams(*sem):
    return pltpu.CompilerParams(dimension_semantics=sem, vmem_limit_bytes=VMEM_LIMIT_BYTES)


def _pick(n, cands):
    for c in cands:
        if n % c == 0:
            return c
    return n


def _split3(x):
    h = x.astype(BF16)
    r = x - h.astype(F32)
    m = r.astype(BF16)
    lo = (r - m.astype(F32)).astype(BF16)
    return h, m, lo


def _dot(a, b, dims=None):
    if dims is None:
        return jnp.dot(a, b, preferred_element_type=F32)
    return lax.dot_general(a, b, dims, preferred_element_type=F32)


def _mm_kernel(x_ref, w_ref, o_ref, *, act):
    w = w_ref[...] if len(w_ref.shape) == 2 else w_ref[0]
    acc = _dot(x_ref[...], w.astype(BF16))
    if act == "relu2":
        acc = jnp.square(jnp.maximum(acc, 0.0))
    o_ref[...] = acc.astype(o_ref.dtype)


def _mm(x, w, act=None, out_dtype=F32, layer=None, n_cols=None):
    M, K = x.shape
    N = w.shape[-1] if n_cols is None else n_cols
    tm = _pick(M, (2048, 1024, 512, 256, 128, 64, 32, 16, 8))
    tn = _pick(N, (1024, 768, 512, 384, 256, 128))
    if layer is None:
        w_spec = pl.BlockSpec((K, tn), lambda j, i: (0, j))
    else:
        w_spec = pl.BlockSpec((1, K, tn), lambda j, i: (layer, 0, j))
    return pl.pallas_call(
        functools.partial(_mm_kernel, act=act),
        out_shape=jax.ShapeDtypeStruct((M, N), out_dtype),
        grid=(N // tn, M // tm),
        in_specs=[pl.BlockSpec((tm, K), lambda j, i: (i, 0)), w_spec],
        out_specs=pl.BlockSpec((tm, tn), lambda j, i: (i, j)),
        compiler_params=_cparams("parallel", "parallel"),
        name="mm",
    )(x, w)


def _ln_rows(z, g, b):
    mu = jnp.mean(z, axis=-1, keepdims=True)
    zc = z - mu
    var = jnp.mean(zc * zc, axis=-1, keepdims=True)
    return zc * lax.rsqrt(var + EPS) * g + b


def _mm_res_ln_kernel(x_ref, w_ref, r_ref, g_ref, b_ref, o_ref, ob_ref, *, alpha):
    y = _dot(x_ref[...], w_ref[...])
    out = _ln_rows(alpha * r_ref[...] + y, g_ref[...], b_ref[...])
    o_ref[...] = out
    ob_ref[...] = out.astype(BF16)


def _mm_res_ln(x, w, res, g, b, alpha):
    M, K = x.shape
    N = w.shape[1]
    tm = _pick(M, (512, 256, 128, 64, 32, 16, 8))
    return pl.pallas_call(
        functools.partial(_mm_res_ln_kernel, alpha=alpha),
        out_shape=(jax.ShapeDtypeStruct((M, N), F32), jax.ShapeDtypeStruct((M, N), BF16)),
        grid=(M // tm,),
        in_specs=[pl.BlockSpec((tm, K), lambda i: (i, 0)),
                  pl.BlockSpec((K, N), lambda i: (0, 0)),
                  pl.BlockSpec((tm, N), lambda i: (i, 0)),
                  pl.BlockSpec((1, N), lambda i: (0, 0)),
                  pl.BlockSpec((1, N), lambda i: (0, 0))],
        out_specs=(pl.BlockSpec((tm, N), lambda i: (i, 0)), pl.BlockSpec((tm, N), lambda i: (i, 0))),
        compiler_params=_cparams("parallel"),
        name="mm_res_ln",
    )(x, w, res, g.reshape(1, N), b.reshape(1, N))


def _merge_ln_kernel(b0_ref, b1_ref, b2_ref, b3_ref, xb_ref, wg_ref, wup_ref, wo_ref, r_ref, g_ref, b_ref,
                     o_ref, ob_ref, *, alpha):
    d = wo_ref.shape[0]
    xb = xb_ref[...]
    acc = None
    for n, br_ref in enumerate((b0_ref, b1_ref, b2_ref, b3_ref)):
        up = _dot(br_ref[...], wup_ref[n])
        term = jax.nn.sigmoid(_dot(xb, wg_ref[:, n * d:(n + 1) * d])) * up
        acc = term if acc is None else acc + term
    y = _dot(acc.astype(BF16), wo_ref[...])
    out = _ln_rows(alpha * r_ref[...] + y, g_ref[...], b_ref[...])
    o_ref[...] = out
    ob_ref[...] = out.astype(BF16)


def _merge_ln(branches, xb, w_gate, w_up, w_o, res, g, b, alpha):
    M = xb.shape[0]
    D = w_o.shape[0]
    tm = _pick(M, (256, 128, 64, 32, 16, 8))
    br_spec = pl.BlockSpec((tm, BR_W), lambda i: (i, 0))
    return pl.pallas_call(
        functools.partial(_merge_ln_kernel, alpha=alpha),
        out_shape=(jax.ShapeDtypeStruct((M, D), F32), jax.ShapeDtypeStruct((M, D), BF16)),
        grid=(M // tm,),
        in_specs=[br_spec, br_spec, br_spec, br_spec,
                  pl.BlockSpec((tm, D), lambda i: (i, 0)),
                  pl.BlockSpec((D, N_BRANCH * D), lambda i: (0, 0)),
                  pl.BlockSpec((N_BRANCH, BR_W, D), lambda i: (0, 0, 0)),
                  pl.BlockSpec((D, D), lambda i: (0, 0)),
                  pl.BlockSpec((tm, D), lambda i: (i, 0)),
                  pl.BlockSpec((1, D), lambda i: (0, 0)),
                  pl.BlockSpec((1, D), lambda i: (0, 0))],
        out_specs=(pl.BlockSpec((tm, D), lambda i: (i, 0)), pl.BlockSpec((tm, D), lambda i: (i, 0))),
        compiler_params=_cparams("parallel"),
        name="merge_ln",
    )(*branches, xb, w_gate, w_up, w_o, res, g.reshape(1, D), b.reshape(1, D))


def _xattn_kernel(li_ref, q_ref, k_ref, v_ref, o_ref):
    del li_ref
    scale = MA_DH ** -0.5
    for h in range(MA_H):
        sl = slice(h * MA_DH, (h + 1) * MA_DH)
        q = q_ref[0, :, sl].astype(BF16)
        k = k_ref[0, 0, :, h, :].astype(BF16)
        v = v_ref[0, 0, :, h, :].astype(BF16)
        s = _dot(q, k, _NT) * scale
        m = jnp.max(s, axis=-1, keepdims=True)
        p = jnp.exp(s - m)
        a = p / jnp.sum(p, axis=-1, keepdims=True)
        o_ref[0, :, sl] = _dot(a.astype(BF16), v).astype(o_ref.dtype)


def _xattn(q, mk, mv, li):
    B, L, W = q.shape
    nm = mk.shape[2]
    tq = _pick(L, (512, 256, 128, 64, 32, 16, 8))
    kv_spec = pl.BlockSpec((1, 1, nm, MA_H, MA_DH), lambda b, i, li: (li[0], b, 0, 0, 0))
    return pl.pallas_call(
        _xattn_kernel,
        out_shape=jax.ShapeDtypeStruct((B, L, W), BF16),
        grid_spec=pltpu.PrefetchScalarGridSpec(
            num_scalar_prefetch=1,
            grid=(B, L // tq),
            in_specs=[pl.BlockSpec((1, tq, W), lambda b, i, li: (b, i, 0)), kv_spec, kv_spec],
            out_specs=pl.BlockSpec((1, tq, W), lambda b, i, li: (b, i, 0))),
        compiler_params=_cparams("parallel", "parallel"),
        name="xattn",
    )(li, q, mk, mv)


def _xattn_dec_kernel(li_ref, q_ref, k_ref, v_ref, o_ref, *, t_pad):
    del li_ref
    qt = q_ref[0]
    row_h = lax.broadcasted_iota(jnp.int32, (qt.shape[0], 1), 0) // t_pad
    lane_h = lax.broadcasted_iota(jnp.int32, (1, k_ref.shape[2]), 1) % MA_H
    s = _dot(qt, k_ref[0, 0].astype(BF16), _NT) * (MA_DH ** -0.5)
    s = jnp.where(lane_h == row_h, s, NEG)
    m = jnp.max(s, axis=-1, keepdims=True)
    p = jnp.exp(s - m)
    a = p / jnp.sum(p, axis=-1, keepdims=True)
    o = _dot(a.astype(BF16), v_ref[0, 0].astype(BF16))
    for h in range(MA_H):
        o_ref[0, :, h * MA_DH:(h + 1) * MA_DH] = o[h * t_pad:(h + 1) * t_pad]


def _xattn_dec(qt, mk, mv, li):
    B, nrow, dh = qt.shape
    t_pad = nrow // MA_H
    nk = mk.shape[2]
    kv_spec = pl.BlockSpec((1, 1, nk, dh), lambda b, li: (li[0], b, 0, 0))
    return pl.pallas_call(
        functools.partial(_xattn_dec_kernel, t_pad=t_pad),
        out_shape=jax.ShapeDtypeStruct((B, t_pad, MA_H * dh), F32),
        grid_spec=pltpu.PrefetchScalarGridSpec(
            num_scalar_prefetch=1,
            grid=(B,),
            in_specs=[pl.BlockSpec((1, nrow, dh), lambda b, li: (b, 0, 0)), kv_spec, kv_spec],
            out_specs=pl.BlockSpec((1, t_pad, MA_H * dh), lambda b, li: (b, 0, 0))),
        compiler_params=_cparams("parallel"),
        name="xattn_dec",
    )(li, qt, mk, mv)


def _head_rms(o):
    return o * lax.rsqrt(jnp.mean(o * o, axis=-1, keepdims=True) + EPS)


def _da_prep_kernel(q_ref, k_ref, v_ref, cos_ref, sin_ref, qo_ref, ko_ref, vt_ref, kr_ref, vr_ref, *, q_scale):
    lane = lax.broadcasted_iota(jnp.int32, (1, LANES), 1) % DA_DH
    first = lane < DA_ROT // 2
    cos = cos_ref[...]
    sin = sin_ref[...]

    def rotary(x):
        partner = jnp.where(first, pltpu.roll(x, LANES - DA_ROT // 2, 1), pltpu.roll(x, DA_ROT // 2, 1))
        return x * cos + partner * sin

    for h in range(DA_H):
        sl = slice(h * LANES, (h + 1) * LANES)
        kh = rotary(k_ref[0, :, sl])
        vh = v_ref[0, :, sl]
        kr_ref[0, :, h, :] = kh
        vr_ref[0, :, h, :] = vh
        ko_ref[0, :, sl] = kh.astype(BF16)
        qo_ref[0, :, sl] = (rotary(q_ref[0, :, sl]) * q_scale).astype(BF16)
        vt_ref[0, sl, :] = vh.T.astype(BF16)


def _da_prep(main, tables, q_scale):
    B, L, _ = main.shape
    w = DA_H * LANES
    tl = _pick(L, (512, 256, 128))
    cos, sin = tables

    def col_spec(idx):
        return pl.BlockSpec((1, tl, w), lambda b, t: (b, t, idx))

    tab_spec = pl.BlockSpec((tl, LANES), lambda b, t: (t, 0))
    row_spec = pl.BlockSpec((1, tl, w), lambda b, t: (b, t, 0))
    head_spec = pl.BlockSpec((1, tl, DA_H, LANES), lambda b, t: (b, t, 0, 0))
    head_shape = jax.ShapeDtypeStruct((B, L, DA_H, LANES), F32)
    return pl.pallas_call(
        functools.partial(_da_prep_kernel, q_scale=q_scale),
        out_shape=(jax.ShapeDtypeStruct((B, L, w), BF16), jax.ShapeDtypeStruct((B, L, w), BF16),
                   jax.ShapeDtypeStruct((B, w, L), BF16), head_shape, head_shape),
        grid=(B, L // tl),
        in_specs=[col_spec(0), col_spec(1), col_spec(2), tab_spec, tab_spec],
        out_specs=(row_spec, row_spec, pl.BlockSpec((1, w, tl), lambda b, t: (b, 0, t)), head_spec, head_spec),
        compiler_params=_cparams("parallel", "parallel"),
        name="da_prep",
    )(main, main, main, cos, sin)


def _flash_da_kernel(sc_ref, q_ref, k_ref, vt_ref, o_ref, st0_scr, st1_scr, p0_scr, p1_scr, acc_scr, m_scr, l_scr,
                     *, tq, tk):
    i = pl.program_id(2)
    lam = sc_ref[0]
    out_scale = sc_ref[1]
    q = q_ref[0]
    lo = lax.broadcasted_iota(jnp.int32, (1, LANES), 1) < DA_DH
    zero = jnp.zeros_like(q)
    q12 = jnp.concatenate([jnp.where(lo, q, zero), jnp.where(lo, zero, q)], axis=0)
    col = lax.broadcasted_iota(jnp.int32, (1, 2 * tq), 1)
    qpos = i * tq + jnp.where(col >= tq, col - tq, col)
    last = (i * tq) // tk

    def qk(j):
        start = pl.multiple_of(j * tk, tk)
        return _dot(k_ref[0, pl.ds(start, tk), :], q12, _NT)

    def scores_to(st_ref, j):
        st = qk(j)
        st_ref[...] = st
        return jnp.max(st, axis=0, keepdims=True)

    def softmax_step(st_ref, p_ref, m, l, blk_max, mask=None):
        st = st_ref[...]
        if mask is not None:
            st = jnp.where(mask, st, NEG)
            blk_max = jnp.max(st, axis=0, keepdims=True)
        m_new = jnp.maximum(m, blk_max)
        alpha = jnp.exp2(m - m_new)
        p = jnp.exp2(st - m_new)
        p_ref[...] = p.astype(BF16)
        return m_new, alpha * l + jnp.sum(p, axis=0, keepdims=True), alpha

    def accumulate(j, p_ref, alpha):
        start = pl.multiple_of(j * tk, tk)
        acc_scr[...] = alpha * acc_scr[...] + _dot(vt_ref[0, :, pl.ds(start, tk)], p_ref[...])

    odd = last % 2
    m_scr[...] = jnp.full((1, 2 * tq), NEG, F32)
    l_scr[...] = jnp.zeros((1, 2 * tq), F32)
    acc_scr[...] = jnp.zeros((DA_DV, 2 * tq), F32)
    p1_scr[...] = jnp.zeros((tk, 2 * tq), BF16)

    @pl.when(odd == 1)
    def _():
        mx = scores_to(st0_scr, 0)
        m, l, alpha = softmax_step(st0_scr, p0_scr, m_scr[...], l_scr[...], mx)
        accumulate(0, p0_scr, alpha)
        m_scr[...] = m
        l_scr[...] = l

    mx_first = scores_to(st0_scr, odd)

    def pair(u, carry):
        a_prev, m, l, mx0 = carry
        b0 = odd + 2 * u
        mx1 = scores_to(st1_scr, b0 + 1)
        accumulate(jnp.maximum(b0 - 1, 0), p1_scr, a_prev)
        m, l, a0 = softmax_step(st0_scr, p0_scr, m, l, mx0)
        mx0 = scores_to(st0_scr, b0 + 2)
        accumulate(b0, p0_scr, a0)
        m, l, a1 = softmax_step(st1_scr, p1_scr, m, l, mx1)
        return a1, m, l, mx0

    a_prev, m, l, mx0 = lax.fori_loop(
        0, last // 2, pair, (jnp.ones((1, 2 * tq), F32), m_scr[...], l_scr[...], mx_first))
    accumulate(jnp.maximum(last - 1, 0), p1_scr, a_prev)
    kpos = last * tk + lax.broadcasted_iota(jnp.int32, (tk, 1), 0)
    _, l, alpha = softmax_step(st0_scr, p0_scr, m, l, mx0, mask=kpos <= qpos)
    accumulate(last, p0_scr, alpha)
    o12 = acc_scr[...] / l
    o = (o12[:, :tq] - lam * o12[:, tq:]).T
    o_ref[0] = (_head_rms(o) * out_scale).astype(o_ref.dtype)


def _flash_da(q, k, vt, sc):
    B, L, W = q.shape
    tq = _pick(L, (512, 256, 128, 64, 32, 16, 8))
    tk = _pick(L, (512, 256, 128, 64, 32, 16, 8))
    return pl.pallas_call(
        functools.partial(_flash_da_kernel, tq=tq, tk=tk),
        out_shape=jax.ShapeDtypeStruct((B, L, W), BF16),
        grid=(B, DA_H, L // tq),
        in_specs=[pl.BlockSpec(memory_space=pltpu.SMEM),
                  pl.BlockSpec((1, tq, LANES), lambda b, h, i: (b, i, h)),
                  pl.BlockSpec((1, L, LANES), lambda b, h, i: (b, 0, h)),
                  pl.BlockSpec((1, DA_DV, L), lambda b, h, i: (b, h, 0))],
        out_specs=pl.BlockSpec((1, tq, LANES), lambda b, h, i: (b, i, h)),
        scratch_shapes=[pltpu.VMEM((tk, 2 * tq), F32), pltpu.VMEM((tk, 2 * tq), F32),
                        pltpu.VMEM((tk, 2 * tq), BF16), pltpu.VMEM((tk, 2 * tq), BF16),
                        pltpu.VMEM((DA_DV, 2 * tq), F32), pltpu.VMEM((1, 2 * tq), F32),
                        pltpu.VMEM((1, 2 * tq), F32)],
        compiler_params=_cparams("parallel", "parallel", "arbitrary"),
        name="flash_da",
    )(sc, q, k, vt)


def _decode_da_kernel(pt_ref, li_ref, sc_ref, qb_ref, kn_ref, vn_ref, *refs, n_pages, t_pad):
    del pt_ref, li_ref
    k_refs = refs[:n_pages]
    v_refs = refs[n_pages:2 * n_pages]
    o_ref = refs[2 * n_pages]
    lam = sc_ref[0]
    out_scale = sc_ref[1]
    qt = qb_ref[0]
    nrow = qt.shape[0]
    row = lax.broadcasted_iota(jnp.int32, (nrow, 1), 0)
    row_h = (row // t_pad) % DA_H
    row_t = row % t_pad

    def lane_ids(n):
        lane = lax.broadcasted_iota(jnp.int32, (1, n), 1)
        return lane // DA_H, lane % DA_H

    _, lane_h = lane_ids(k_refs[0].shape[2])
    valid = lane_h == row_h
    s_pages = [jnp.where(valid, _dot(qt, k_refs[p][0, 0].astype(BF16), _NT), NEG) for p in range(n_pages)]
    new_t, new_h = lane_ids(kn_ref.shape[1])
    valid_new = jnp.logical_and(new_h == row_h, new_t <= row_t)
    s_new = jnp.where(valid_new, _dot(qt, kn_ref[0].astype(BF16), _NT), NEG)

    m = jnp.max(s_new, axis=-1, keepdims=True)
    for s in s_pages:
        m = jnp.maximum(m, jnp.max(s, axis=-1, keepdims=True))

    p_new = jnp.exp(s_new - m)
    den = jnp.sum(p_new, axis=-1, keepdims=True)
    acc = _dot(p_new, vn_ref[0])
    for p in range(n_pages):
        pp = jnp.exp(s_pages[p] - m)
        den = den + jnp.sum(pp, axis=-1, keepdims=True)
        acc = acc + _dot(pp.astype(BF16), v_refs[p][0, 0].astype(BF16))
    o12 = acc / den
    half = DA_H * t_pad
    d = o12[:half] - lam * o12[half:]
    for h in range(DA_H):
        o_ref[0, :, h * DA_DV:(h + 1) * DA_DV] = _head_rms(d[h * t_pad:(h + 1) * t_pad]) * out_scale


def _decode_da(qt, k_new, v_new, cache_k, cache_v, page_table, li, sc):
    bd, nrow, _ = qt.shape
    t_pad = k_new.shape[1] // DA_H
    w = DA_H * DA_DV
    n_pages = page_table.shape[1]
    prow = cache_k.shape[2]

    def page_spec(p):
        return pl.BlockSpec((1, 1, prow, DA_DV), lambda b, pt, li: (li[0], pt[b, p], 0, 0))

    in_specs = [pl.BlockSpec(memory_space=pltpu.SMEM),
                pl.BlockSpec((1, nrow, LANES), lambda b, pt, li: (b, 0, 0)),
                pl.BlockSpec((1, t_pad * DA_H, DA_DV), lambda b, pt, li: (b, 0, 0)),
                pl.BlockSpec((1, t_pad * DA_H, DA_DV), lambda b, pt, li: (b, 0, 0))]
    in_specs += [page_spec(p) for p in range(n_pages)] * 2
    return pl.pallas_call(
        functools.partial(_decode_da_kernel, n_pages=n_pages, t_pad=t_pad),
        out_shape=jax.ShapeDtypeStruct((bd, t_pad, w), F32),
        grid_spec=pltpu.PrefetchScalarGridSpec(
            num_scalar_prefetch=2,
            grid=(bd,),
            in_specs=in_specs,
            out_specs=pl.BlockSpec((1, t_pad, w), lambda b, pt, li: (b, 0, 0))),
        compiler_params=_cparams("parallel"),
        name="decode_da",
    )(page_table, li, sc, qt, k_new, v_new, *([cache_k] * n_pages), *([cache_v] * n_pages))


def _cumsum_rows(x, tri):
    h, m, lo = _split3(x)
    return _dot(tri, h) + _dot(tri, m) + _dot(tri, lo)


def _scan_qk_kernel(li_ref, q_ref, k_ref, v_ref, g_ref, cos_ref, sin_ref, lr_ref, a2_ref, ba_ref, s0_ref,
                    o_ref, sn_ref, s_scr, *, c, nch, bb, lv, mode):
    del li_ref
    t = pl.program_id(1)

    @pl.when(t == 0)
    def _():
        s_scr[...] = s0_ref[0]

    mxu = BF16 if c >= 16 else F32
    row = lax.broadcasted_iota(jnp.int32, (c, 1), 0)
    colc = lax.broadcasted_iota(jnp.int32, (1, c), 1)
    causal = row >= colc
    tri = jnp.where(causal, 1.0, 0.0).astype(BF16)
    lo = lax.broadcasted_iota(jnp.int32, (1, LANES), 1) < RT_DK
    wqk = RT_H * RT_DK
    half = RT_DK // 2
    first_half = lax.broadcasted_iota(jnp.int32, (1, wqk), 1) % RT_DK < half
    mid = c // 2 - 1

    def rotary(x, cos, sin):
        partner = jnp.where(first_half, pltpu.roll(x, wqk - half, 1), pltpu.roll(x, half, 1))
        return x * cos + partner * sin

    bodies = [(bi, ci) for ci in range(nch) for bi in range(bb)]
    head_masks = (lo, jnp.logical_not(lo))

    def load(bi, ci):
        rows = slice(ci * c, (ci + 1) * c)
        q = q_ref[bi, rows, :]
        k = k_ref[bi, rows, :]
        if mode == "ret":
            cos = cos_ref[rows, :]
            sin = sin_ref[rows, :]
            q = rotary(q, cos, sin)
            k = rotary(k, cos, sin)
            la = jnp.broadcast_to(ba_ref[...], (c, wqk))
        else:
            x = _dot(lr_ref[bi, rows, :].astype(BF16), a2_ref[...]) + ba_ref[...]
            la = jax.nn.log_sigmoid(x) * (1.0 / GLA_TEMP)
        k = k * (RT_DK ** -0.5)
        if lv < c:
            valid = row < lv
            la = jnp.where(valid, la, 0.0)
            k = jnp.where(valid, k, 0.0)
        return dict(q=q, k=k, la=la, v=v_ref[bi, rows, :].astype(mxu))

    st = {bd: load(*bd) for bd in bodies}
    for bd in bodies:
        st[bd]["b"] = _cumsum_rows(st[bd]["la"], tri)
    for bd in bodies:
        d = st[bd]
        b, q, k = d["b"], d["q"], d["k"]
        bm = b[mid:mid + 1, :]
        bl = b[c - 1:c, :]
        d["qs"] = q * jnp.exp(b - bm)
        d["ks"] = (k * jnp.exp(bm - b)).astype(mxu)
        d["kh"] = (k * jnp.exp(bl - b)).astype(mxu)
        d["qe"] = q * jnp.exp(b)
    for bd in bodies:
        d = st[bd]
        d["s"] = [jnp.where(causal, _dot(jnp.where(head_masks[h % 2], d["qs"][:, (h // 2) * LANES:(h // 2 + 1) * LANES],
                                                     0.0).astype(mxu),
                                         d["ks"][:, (h // 2) * LANES:(h // 2 + 1) * LANES], _NT), 0.0).astype(mxu)
                  for h in range(RT_H)]
        d["upd"] = [_dot(d["kh"][:, p * LANES:(p + 1) * LANES], d["v"][:, 2 * p * RT_DV:(2 * p + 2) * RT_DV], _TN)
                    for p in range(2)]
        d["dcol"] = [jnp.exp(d["b"][:, p * LANES:(p + 1) * LANES].T[:, c - 1:c]) for p in range(2)]
    for bd in bodies:
        d = st[bd]
        d["o"] = [_dot(d["s"][h], d["v"][:, h * RT_DV:(h + 1) * RT_DV]) for h in range(RT_H)]
    for bi, ci in bodies:
        d = st[(bi, ci)]
        rows = slice(ci * c, (ci + 1) * c)
        g = g_ref[bi, rows, :]
        for p in range(2):
            S = s_scr[bi, p]
            Sm = S.astype(mxu)
            qep = d["qe"][:, p * LANES:(p + 1) * LANES]
            for hh in range(2):
                h = 2 * p + hh
                hs = slice(h * RT_DV, (h + 1) * RT_DV)
                o = d["o"][h] + _dot(jnp.where(head_masks[hh], qep, 0.0).astype(mxu), Sm)
                gh = g[:, hs]
                o_ref[bi, rows, hs] = (_head_rms(o) * (gh * jax.nn.sigmoid(gh))).astype(o_ref.dtype)
            upd = d["upd"][p]
            s_scr[bi, p] = d["dcol"][p] * S + jnp.concatenate(
                [upd[0:RT_DK, 0:RT_DV], upd[RT_DK:2 * RT_DK, RT_DV:2 * RT_DV]], axis=0)

    @pl.when(t == pl.num_programs(1) - 1)
    def _():
        sn_ref[...] = s_scr[...]


def _scan_qk(main, cols, tables, lr, a2, ba, s0, li, lv, mode):
    B, L, _ = main.shape
    c = _pick(L, (128, 64, 32, 16, 8))
    tl = _pick(L, (512, 256, 128, 64, 32, 16, 8))
    bb = _pick(B, (2, 1)) if L > SUBLANES else _pick(B, (8, 4, 2, 1))
    wqk = RT_H * RT_DK
    wv = RT_H * RT_DV
    if lv is None:
        lv = c
    qi, ki, vi, gi = cols
    cos, sin = tables

    def col_spec(width, idx):
        return pl.BlockSpec((bb, tl, width), lambda b, t, li: (b, t, idx))

    tab_spec = pl.BlockSpec((tl, wqk), lambda b, t, li: (t, 0))
    return pl.pallas_call(
        functools.partial(_scan_qk_kernel, c=c, nch=tl // c, bb=bb, lv=lv, mode=mode),
        out_shape=(jax.ShapeDtypeStruct((B, L, wv), BF16),
                   jax.ShapeDtypeStruct((B, 2, 2 * RT_DK, RT_DV), F32)),
        grid_spec=pltpu.PrefetchScalarGridSpec(
            num_scalar_prefetch=1,
            grid=(B // bb, L // tl),
            in_specs=[col_spec(wqk, qi), col_spec(wqk, ki), col_spec(wv, vi), col_spec(wv, gi),
                      tab_spec, tab_spec,
                      pl.BlockSpec((bb, tl, LANES), lambda b, t, li: (b, t, 0)),
                      pl.BlockSpec((LANES, wqk), lambda b, t, li: (0, 0)),
                      pl.BlockSpec((1, wqk), lambda b, t, li: (0, 0)),
                      pl.BlockSpec((1, bb, 2, 2 * RT_DK, RT_DV), lambda b, t, li: (li[0], b, 0, 0, 0))],
            out_specs=(pl.BlockSpec((bb, tl, wv), lambda b, t, li: (b, t, 0)),
                       pl.BlockSpec((bb, 2, 2 * RT_DK, RT_DV), lambda b, t, li: (b, 0, 0, 0))),
            scratch_shapes=[pltpu.VMEM((bb, 2, 2 * RT_DK, RT_DV), F32)]),
        compiler_params=_cparams("parallel", "arbitrary"),
        name="scan_qk_" + mode,
    )(li, main, main, main, main, cos, sin, lr, a2, ba, s0)


def _scan_ssd_kernel(li_ref, z_ref, x_ref, bc_ref, lr_ref, cw_ref, cb_ref, hp_ref, nw_ref, cbuf_ref, s0_ref,
                     o_ref, sn_ref, s_scr, prev_scr, y_scr, *, c, nch, bb, lv):
    del li_ref
    t = pl.program_id(1)
    nprev = prev_scr.shape[1]
    nconv = cbuf_ref.shape[1]

    @pl.when(t == 0)
    def _():
        s_scr[...] = s0_ref[0]
        prev_scr[:, :, :] = jnp.zeros(prev_scr.shape, F32)
        prev_scr[:, nprev - nconv:, :] = cbuf_ref[...]

    mxu = BF16 if c >= 16 else F32
    row = lax.broadcasted_iota(jnp.int32, (c, 1), 0)
    colc = lax.broadcasted_iota(jnp.int32, (1, c), 1)
    causal = row >= colc
    tri = jnp.where(causal, 1.0, 0.0).astype(BF16)
    tri_t = jnp.where(row <= colc, 1.0, 0.0).astype(BF16)
    hpg = SSD_H // SSD_G
    gw = SSD_DI // SSD_G
    dt_bias = hp_ref[0:1, :]
    a_head = -jnp.exp(hp_ref[1:2, :])
    d_head = hp_ref[2:3, :]

    def conv_silu(ref, bi, ci, lo_ch):
        w = ref.shape[2]
        if ci == 0:
            ext = jnp.concatenate([prev_scr[bi, :, lo_ch:lo_ch + w], ref[bi, 0:c, :]], axis=0)
            taps = [ext[nprev - nconv + j:nprev - nconv + j + c] for j in range(nconv + 1)]
        else:
            r0 = ci * c
            taps = [ref[bi, r0 - nconv + j:r0 - nconv + j + c, :] for j in range(nconv + 1)]
        acc = cb_ref[0:1, lo_ch:lo_ch + w]
        for j, tap in enumerate(taps):
            acc = acc + tap * cw_ref[j:j + 1, lo_ch:lo_ch + w]
        return acc * jax.nn.sigmoid(acc)

    bodies = [(bi, ci) for ci in range(nch) for bi in range(bb)]

    def load(bi, ci):
        rows = slice(ci * c, (ci + 1) * c)
        xs = conv_silu(x_ref, bi, ci, 0)
        bcm = conv_silu(bc_ref, bi, ci, SSD_DI)
        bm = bcm[:, :SSD_G * SSD_N]
        dt = jax.nn.softplus(lr_ref[bi, rows, GL_RANK:GL_RANK + SSD_H] + dt_bias)
        a = dt * a_head
        if lv < c:
            valid = row < lv
            a = jnp.where(valid, a, 0.0)
            bm = jnp.where(valid, bm, 0.0)
        return dict(xs=xs, bm=bm, cm=bcm[:, SSD_G * SSD_N:].astype(mxu), dt=dt, a=a)

    st = {bd: load(*bd) for bd in bodies}
    for bd in bodies:
        d = st[bd]
        ah, am, al = _split3(d["a"])
        d["bcol"] = _dot(tri, ah) + _dot(tri, am) + _dot(tri, al)
        d["brow"] = _dot(ah, tri_t, _TN) + _dot(am, tri_t, _TN) + _dot(al, tri_t, _TN)
        d["gmat"] = [_dot(d["cm"][:, g * SSD_N:(g + 1) * SSD_N],
                          d["bm"][:, g * SSD_N:(g + 1) * SSD_N].astype(mxu), _NT) for g in range(SSD_G)]
    for bd in bodies:
        d = st[bd]
        d["xh"], d["y"], d["upd"] = [], [], []
        for h in range(SSD_H):
            g = h // hpg
            bc = d["bcol"][:, h:h + 1]
            w = jnp.exp(jnp.where(causal, bc - d["brow"][h:h + 1, :], NEG))
            xh = (d["xs"][:, h * SSD_P:(h + 1) * SSD_P] * d["dt"][:, h:h + 1]).astype(mxu)
            d["y"].append(_dot((d["gmat"][g] * w).astype(mxu), xh))
            bg = d["bm"][:, g * SSD_N:(g + 1) * SSD_N]
            d["upd"].append(_dot((bg * jnp.exp(bc[c - 1:c, :] - bc)).astype(mxu), xh, _TN))
    for bi, ci in bodies:
        d = st[(bi, ci)]
        slot = bi * nch + ci
        for h in range(SSD_H):
            g = h // hpg
            hs = slice(h * SSD_P, (h + 1) * SSD_P)
            bc = d["bcol"][:, h:h + 1]
            S = s_scr[bi, h]
            y = d["y"][h] + jnp.exp(bc) * _dot(d["cm"][:, g * SSD_N:(g + 1) * SSD_N], S.astype(mxu))
            y_scr[slot, :, hs] = y + d_head[:, h:h + 1] * d["xs"][:, hs]
            s_scr[bi, h] = jnp.exp(bc[c - 1:c, :]) * S + d["upd"][h]
    for bi, ci in bodies:
        rows = slice(ci * c, (ci + 1) * c)
        z = z_ref[bi, rows, :]
        gated = y_scr[bi * nch + ci] * (z * jax.nn.sigmoid(z))
        for g in range(SSD_G):
            gs = slice(g * gw, (g + 1) * gw)
            o_ref[bi, rows, gs] = (_head_rms(gated[:, gs]) * nw_ref[0:1, gs]).astype(o_ref.dtype)

    for bi in range(bb):
        prev_scr[bi, :, 0:SSD_DI] = x_ref[bi, x_ref.shape[1] - nprev:, :]
        prev_scr[bi, :, SSD_DI:] = bc_ref[bi, bc_ref.shape[1] - nprev:, :]

    @pl.when(t == pl.num_programs(1) - 1)
    def _():
        sn_ref[...] = s_scr[...]


def _scan_ssd(ssd, lr, conv_w, conv_b, head_params, norm_w, conv_buf, s0, li, lv):
    B, L, _ = ssd.shape
    c = _pick(L, (128, 64, 32, 16, 8))
    tl = _pick(L, (256, 128, 64, 32, 16, 8))
    bb = _pick(B, (2, 1)) if L > SUBLANES else _pick(B, (8, 4, 2, 1))
    nch = tl // c
    if lv is None:
        lv = c

    def col_spec(idx):
        return pl.BlockSpec((bb, tl, SSD_DI), lambda b, t, li: (b, t, idx))

    def full(a):
        return pl.BlockSpec(a.shape, lambda b, t, li: (0,) * a.ndim)

    return pl.pallas_call(
        functools.partial(_scan_ssd_kernel, c=c, nch=nch, bb=bb, lv=lv),
        out_shape=(jax.ShapeDtypeStruct((B, L, SSD_DI), BF16),
                   jax.ShapeDtypeStruct((B, SSD_H, SSD_N, SSD_P), F32)),
        grid_spec=pltpu.PrefetchScalarGridSpec(
            num_scalar_prefetch=1,
            grid=(B // bb, L // tl),
            in_specs=[col_spec(0), col_spec(1), col_spec(2),
                      pl.BlockSpec((bb, tl, LANES), lambda b, t, li: (b, t, 0)),
                      full(conv_w), full(conv_b), full(head_params), full(norm_w),
                      pl.BlockSpec((bb, SSD_CONV - 1, CONV_CH), lambda b, t, li: (b, 0, 0)),
                      pl.BlockSpec((1, bb, SSD_H, SSD_N, SSD_P), lambda b, t, li: (li[0], b, 0, 0, 0))],
            out_specs=(pl.BlockSpec((bb, tl, SSD_DI), lambda b, t, li: (b, t, 0)),
                       pl.BlockSpec((bb, SSD_H, SSD_N, SSD_P), lambda b, t, li: (b, 0, 0, 0))),
            scratch_shapes=[pltpu.VMEM((bb, SSD_H, SSD_N, SSD_P), F32),
                            pltpu.VMEM((bb, SUBLANES, CONV_CH), F32),
                            pltpu.VMEM((bb * nch, c, SSD_DI), F32)]),
        compiler_params=_cparams("parallel", "arbitrary"),
        name="scan_ssd",
    )(li, ssd, ssd, ssd, lr, conv_w, conv_b, head_params, norm_w, conv_buf, s0)


def _rope(x, pos, rot_dim, theta):
    half = rot_dim // 2
    inv = theta ** (-jnp.arange(half, dtype=F32) / half)
    ang = pos.astype(F32)[:, None] * inv[None, :]
    shape = (1, ang.shape[0]) + (1,) * (x.ndim - 3) + (half,)
    cos = jnp.cos(ang).reshape(shape)
    sin = jnp.sin(ang).reshape(shape)
    x1, x2, rest = x[..., :half], x[..., half:rot_dim], x[..., rot_dim:]
    return jnp.concatenate([x1 * cos - x2 * sin, x1 * sin + x2 * cos, rest], axis=-1)


def _prep_weights(w_in, gla_w_a2, w_up, w_o, w_cq, w_ck, w_cv, w_co, w_mlp1, w_mlp2):
    offs = {}
    o = 0
    names = ("da_q", "da_k", "da_v", "rt_q", "rt_k", "rt_v", "rt_g", "gl_q", "gl_k", "gl_v", "gl_r",
             "gl_lr", "sd_z", "sd_xbc", "sd_dt", "gate")
    d_model = w_in.shape[1]
    sizes = (512, 512, 512, 256, 256, 512, 512, 256, 256, 512, 512, GL_RANK, SSD_DI, CONV_CH, SSD_H,
             N_BRANCH * d_model)
    for n, s in zip(names, sizes):
        offs[n] = (o, o + s)
        o += s
    depth = w_in.shape[0]
    small = jnp.concatenate(
        [w_in[:, :, offs["gl_lr"][0]:offs["gl_lr"][1]], w_in[:, :, offs["sd_dt"][0]:offs["sd_dt"][1]],
         jnp.zeros((depth, d_model, LANES - GL_RANK - SSD_H), w_in.dtype)], axis=-1)
    ssd = w_in[:, :, offs["sd_z"][0]:offs["sd_xbc"][1]]
    gate = w_in[:, :, offs["gate"][0]:offs["gate"][1]]
    a2 = jnp.concatenate(
        [gla_w_a2, jnp.zeros((depth, LANES - GL_RANK, gla_w_a2.shape[2]), gla_w_a2.dtype)], axis=1)
    c = lambda w: w.astype(BF16)
    return dict(w_in=w_in, n_main=offs["gl_r"][1], w_cq=w_cq, w_ck=w_ck, w_cv=w_cv, w_mlp1=w_mlp1,
                small=c(small), ssd=c(ssd), gate=c(gate), a2=c(a2), w_up=c(w_up), w_o=c(w_o),
                w_co=c(w_co), w_mlp2=c(w_mlp2))


MAIN_COLS_RET = (6, 7, 4, 5)
MAIN_COLS_GLA = (12, 13, 7, 8)


def _ret_tables(pos):
    half = RT_DK // 2
    inv = RET_THETA ** (-jnp.arange(half, dtype=F32) / half)
    ang = pos.astype(F32)[:, None] * inv[None, :]
    cos, sin = jnp.cos(ang), jnp.sin(ang)
    return (jnp.tile(jnp.concatenate([cos, cos], axis=-1), (1, RT_H)),
            jnp.tile(jnp.concatenate([-sin, sin], axis=-1), (1, RT_H)))


def _da_tables(pos):
    half = DA_ROT // 2
    inv = ROPE_THETA ** (-jnp.arange(half, dtype=F32) / half)
    ang = pos.astype(F32)[:, None] * inv[None, :]
    cos, sin = jnp.cos(ang), jnp.sin(ang)
    rest = DA_DH - DA_ROT
    n = pos.shape[0]
    cos64 = jnp.concatenate([cos, cos, jnp.ones((n, rest), F32)], axis=-1)
    sin64 = jnp.concatenate([-sin, sin, jnp.zeros((n, rest), F32)], axis=-1)
    return jnp.tile(cos64, (1, LANES // DA_DH)), jnp.tile(sin64, (1, LANES // DA_DH))


def _layer(l, x, xb, pos, tables, W, P, mk, mv, mem_li, past, states, lv, alpha):
    B, L, D = x.shape
    M = B * L
    s_ret, s_gla, s_ssm, conv_buf, st_li = states
    xb2 = xb.reshape(M, D)
    main = _mm(xb2, W["w_in"], layer=l, n_cols=W["n_main"]).reshape(B, L, -1)
    small = _mm(xb2, W["small"][l])
    ssd = _mm(xb2, W["ssd"][l]).reshape(B, L, -1)
    da_q, da_k, da_v = main[..., 0:512], main[..., 512:1024], main[..., 1024:1536]

    lam_init = 0.8 - 0.6 * math.exp(-0.3 * l)
    lam = (jnp.exp(jnp.sum((P["da_lq1"][l] * P["da_lk1"][l]).astype(F32)))
           - jnp.exp(jnp.sum((P["da_lq2"][l] * P["da_lk2"][l]).astype(F32))) + lam_init)
    sc = jnp.stack([lam, jnp.asarray(1.0 - lam_init, F32)]).astype(F32)
    if past is None:
        qb, kb, vt, k_rows, v_rows = _da_prep(main, tables[1], DA_DH ** -0.5 * math.log2(math.e))
        o_da = _flash_da(qb, kb, vt, sc)
    else:
        v_rows = da_v.reshape(B, L, DA_H, DA_DV)
        q = _rope(da_q.reshape(B, L, DA_H, 2, DA_DH), pos, DA_ROT, ROPE_THETA)
        k = _rope(da_k.reshape(B, L, DA_H, 2, DA_DH), pos, DA_ROT, ROPE_THETA)
        k_rows = k.reshape(B, L, DA_H, 2 * DA_DH)
        qs = q.reshape(B, L, 512) * (DA_DH ** -0.5)
        cache_k, cache_v, page_table, cache_li = past
        q5 = qs.reshape(B, L, DA_H, 2, DA_DH)
        qt = jnp.einsum("bthcd,ce->bchted", q5, jnp.eye(2, dtype=F32)).reshape(B, 2 * DA_H * L, 2 * DA_DH)
        o_da = _decode_da(qt.astype(BF16), k.reshape(B, L * DA_H, 2 * DA_DH), da_v.reshape(B, L * DA_H, DA_DV),
                          cache_k, cache_v, page_table, cache_li, sc)
    o_da = o_da.astype(BF16)

    log_gamma = jnp.log1p(-jnp.exp2(-5.0 - jnp.arange(RT_H, dtype=F32)))
    la_ret = jnp.repeat(log_gamma, RT_DK).reshape(1, RT_H * RT_DK)
    small3 = small.reshape(B, L, LANES)
    o_rt, s_ret_new = _scan_qk(main, MAIN_COLS_RET, tables[0], small3, W["a2"][l], la_ret, s_ret, st_li, lv, "ret")

    o_gl, s_gla_new = _scan_qk(main, MAIN_COLS_GLA, tables[0], small3, W["a2"][l],
                               P["gla_b_a"][l].reshape(1, -1), s_gla, st_li, lv, "gla")

    head_params = jnp.stack([P["ssd_dt_bias"][l], P["ssd_a_log"][l], P["ssd_d"][l]]).astype(F32)
    o_sd, s_ssm_new = _scan_ssd(ssd, small3, P["ssd_conv_w"][l], P["ssd_conv_b"][l].reshape(1, -1), head_params,
                                P["ssd_norm_w"][l].reshape(1, -1), conv_buf, s_ssm, st_li, lv)
    n_real = L if lv is None else lv
    xpad = jnp.concatenate([conv_buf, ssd[:, max(n_real - SSD_CONV + 1, 0):n_real, SSD_DI:]], axis=1)
    conv_new = xpad[:, xpad.shape[1] - (SSD_CONV - 1):]

    branches = [o.reshape(M, BR_W) for o in (o_da, o_rt, o_gl, o_sd)]
    x2d = x.reshape(M, D)
    x1, x1b = _merge_ln(branches, xb2, W["gate"][l], W["w_up"][l], W["w_o"][l], x2d, P["ln1_g"][l],
                        P["ln1_b"][l], alpha)

    cq = _mm(x1b, W["w_cq"], layer=l).reshape(B, L, -1)
    if past is None:
        oc = _xattn(cq, mk, mv, mem_li).reshape(M, -1)
    else:
        qt = cq.reshape(B, L, MA_H, MA_DH).transpose(0, 2, 1, 3).reshape(B, MA_H * L, MA_DH)
        oc = _xattn_dec(qt.astype(BF16), mk, mv, mem_li).reshape(M, -1).astype(BF16)
    x2, x2b = _mm_res_ln(oc, W["w_co"][l], x1, P["ln2_g"][l], P["ln2_b"][l], alpha)

    hmid = _mm(x2b, W["w_mlp1"], act="relu2", out_dtype=BF16, layer=l)
    x3, x3b = _mm_res_ln(hmid, W["w_mlp2"][l], x2, P["ln3_g"][l], P["ln3_b"][l], alpha)
    return (x3.reshape(B, L, D), x3b.reshape(B, L, D),
            (k_rows, v_rows, s_ret_new, s_gla_new, s_ssm_new, conv_new))


def kernel(x_prompt, x_sample, mem_prompt, cache_diff_k, cache_diff_v, page_table, state_ret, state_gla, state_ssm, state_conv, cache_mem_k, cache_mem_v, w_in, da_lq1, da_lk1, da_lq2, da_lk2, gla_w_a2, gla_b_a, ssd_conv_w, ssd_conv_b, ssd_dt_bias, ssd_a_log, ssd_d, ssd_norm_w, w_up, w_o, ln1_g, ln1_b, w_cq, w_ck, w_cv, w_co, ln2_g, ln2_b, w_mlp1, w_mlp2, ln3_g, ln3_b):
    depth = w_in.shape[0]
    alpha = (2 * depth) ** 0.25
    P = dict(da_lq1=da_lq1, da_lk1=da_lk1, da_lq2=da_lq2, da_lk2=da_lk2, gla_b_a=gla_b_a,
             ssd_conv_w=ssd_conv_w, ssd_conv_b=ssd_conv_b, ssd_dt_bias=ssd_dt_bias, ssd_a_log=ssd_a_log,
             ssd_d=ssd_d, ssd_norm_w=ssd_norm_w, ln1_g=ln1_g, ln1_b=ln1_b, ln2_g=ln2_g, ln2_b=ln2_b,
             ln3_g=ln3_g, ln3_b=ln3_b)
    W = _prep_weights(w_in, gla_w_a2, w_up, w_o, w_cq, w_ck, w_cv, w_co, w_mlp1, w_mlp2)
    zero_li = jnp.zeros((1,), jnp.int32)

    B, L, D = x_prompt.shape
    pos_p = jnp.arange(L, dtype=jnp.int32)
    tab_p = (_ret_tables(pos_p), _da_tables(pos_p))
    memb =mem_prompt.reshape(-1, D).astype(BF16)
    n_mem = mem_prompt.shape[1]
    z_ret = jnp.zeros((1, B, 2, 2 * RT_DK, RT_DV), F32)
    z_ssm = jnp.zeros((1, B, SSD_H, SSD_N, SSD_P), F32)
    z_conv = jnp.zeros((B, SSD_CONV - 1, CONV_CH), F32)
    x, xb = x_prompt, x_prompt.astype(BF16)
    outs_p = [[] for _ in range(8)]
    for l in range(depth):
        mk = _mm(memb, W["w_ck"], layer=l).reshape(1, B, n_mem, MA_H, MA_DH)
        mv = _mm(memb, W["w_cv"], layer=l).reshape(1, B, n_mem, MA_H, MA_DH)
        x, xb, (kr, vr, sr, sg, ss, cn) = _layer(
            l, x, xb, pos_p, tab_p, W, P, mk, mv, zero_li, None, (z_ret, z_ret, z_ssm, z_conv, zero_li), None,
            alpha)
        new = (kr, vr, sr.reshape(B, RT_H, RT_DK, RT_DV), sg.reshape(B, GL_H, GL_DK, GL_DV), ss, cn,
               mk[0], mv[0])
        for lst, val in zip(outs_p, new):
            lst.append(val)
    y_prompt = x

    Bd, Ld, _ = x_sample.shape
    t_pad = -(-Ld // SUBLANES) * SUBLANES
    n_pages, page = page_table.shape[1], cache_diff_k.shape[2]
    past_len = n_pages * page
    pos_s = past_len + jnp.arange(t_pad, dtype=jnp.int32)
    tab_s = (_ret_tables(pos_s), None)
    x = jnp.pad(x_sample, ((0, 0), (0, t_pad - Ld), (0, 0)))
    xb = x.astype(BF16)
    n_pool = cache_diff_k.shape[1]
    ck = cache_diff_k.reshape(depth, n_pool, page * DA_H, 2 * DA_DH)
    cv = cache_diff_v.reshape(depth, n_pool, page * DA_H, DA_DV)
    cmk = cache_mem_k.reshape(depth, Bd, -1, MA_DH)
    cmv = cache_mem_v.reshape(depth, Bd, -1, MA_DH)
    s_ret_all = state_ret.reshape(depth, Bd, 2, 2 * RT_DK, RT_DV)
    s_gla_all = state_gla.reshape(depth, Bd, 2, 2 * GL_DK, GL_DV)
    outs_s = [[] for _ in range(6)]
    for l in range(depth):
        li = jnp.full((1,), l, jnp.int32)
        x, xb, (kr, vr, sr, sg, ss, cn) = _layer(
            l, x, xb, pos_s, tab_s, W, P, cmk, cmv, li, (ck, cv, page_table, li),
            (s_ret_all, s_gla_all, state_ssm, state_conv[l], li), Ld, alpha)
        new = (kr[:, :Ld], vr[:, :Ld], sr.reshape(Bd, RT_H, RT_DK, RT_DV), sg.reshape(Bd, GL_H, GL_DK, GL_DV),
               ss, cn)
        for lst, val in zip(outs_s, new):
            lst.append(val)
    y_sample = x[:, :Ld]

    st = lambda lst: jnp.stack(lst)
    return (y_prompt, y_sample) + tuple(st(o) for o in outs_p) + tuple(st(o) for o in outs_s)
```
